```python
import math
import jax, jax.numpy as jnp
from jax import lax
import numpy as np

D_MODEL = 1024
BATCH = 8
SEQ = 4096
DEPTH = 2

N_A = DEPTH // 2
N_B = DEPTH - N_A
PLE_DIM = 256
SSM_WIDTH = D_MODEL
SSM_GROUP = 16
SSM_GROUPS = SSM_WIDTH // SSM_GROUP
SSM_STATE = 64
SSM_CHUNK = 128
DT_MIN = 0.001
DT_MAX = 0.1
HEAD_DIM = 64
V_DIM = 2 * HEAD_DIM
N_HEADS = D_MODEL // V_DIM
ATTN_WIDTH = N_HEADS * V_DIM
Q_BLOCK = 128
REL_BUCKETS = 32
REL_MAX_EXACT = REL_BUCKETS // 2
REL_MAX_DIST = 128
EPS = 1e-6
NEG_INF = -1e30

kernel_name = "yoco_s5_diffattn_hybrid"


def rmsnorm(x, g=None):
    xf = x.astype(jnp.float32)
    y = xf * lax.rsqrt(jnp.mean(xf * xf, axis=-1, keepdims=True) + EPS)
    if g is not None:
        y = y * g.astype(jnp.float32)
    return y.astype(x.dtype)


def cmul(ar, ai, br, bi):
    return ar * br - ai * bi, ar * bi + ai * br


def s5_scan(u, lam_re, lam_im, log_dt, b_re, b_im, c_re, c_im, d):
    f32 = jnp.float32
    bsz, seq, _ = u.shape
    uf = u.astype(f32).reshape(bsz, seq, SSM_GROUPS, SSM_GROUP)
    dt = jnp.exp(log_dt.astype(f32))[:, None]
    lr, li = lam_re.astype(f32), lam_im.astype(f32)
    mag = jnp.exp(lr * dt)
    ab_re, ab_im = mag * jnp.cos(li * dt), mag * jnp.sin(li * dt)
    den = lr * lr + li * li
    nr, ni = ab_re - 1.0, ab_im
    f_re = (nr * lr + ni * li) / den
    f_im = (ni * lr - nr * li) / den
    bb_re, bb_im = cmul(f_re[..., None], f_im[..., None], b_re.astype(f32), b_im.astype(f32))
    cr, ci = c_re.astype(f32), c_im.astype(f32)
    n_chunks = seq // SSM_CHUNK
    u_chunks = uf.reshape(bsz, n_chunks, SSM_CHUNK, SSM_GROUPS, SSM_GROUP).transpose(1, 0, 2, 3, 4)
    a_shape = (bsz, SSM_CHUNK, SSM_GROUPS, SSM_STATE)
    a_re = jnp.broadcast_to(ab_re, a_shape)
    a_im = jnp.broadcast_to(ab_im, a_shape)

    def combine(e1, e2):
        a1r, a1i, b1r, b1i = e1
        a2r, a2i, b2r, b2i = e2
        ar, ai = cmul(a2r, a2i, a1r, a1i)
        br, bi = cmul(a2r, a2i, b1r, b1i)
        return ar, ai, br + b2r, bi + b2i

    def chunk_step(carry, uc):
        h_re, h_im = carry
        bu_re = jnp.einsum('btgc,gpc->btgp', uc, bb_re)
        bu_im = jnp.einsum('btgc,gpc->btgp', uc, bb_im)
        pr, pi, sr, si = lax.associative_scan(combine, (a_re, a_im, bu_re, bu_im), axis=1)
        hr, hi = cmul(pr, pi, h_re[:, None], h_im[:, None])
        sr, si = sr + hr, si + hi
        y = jnp.einsum('btgp,gcp->btgc', sr, cr) - jnp.einsum('btgp,gcp->btgc', si, ci)
        return (sr[:, -1], si[:, -1]), y

    init = (jnp.zeros((bsz, SSM_GROUPS, SSM_STATE), f32), jnp.zeros((bsz, SSM_GROUPS, SSM_STATE), f32))
    _, ys = lax.scan(chunk_step, init, u_chunks)
    y = ys.transpose(1, 0, 2, 3, 4).reshape(bsz, seq, SSM_WIDTH)
    y = y + d.astype(f32) * uf.reshape(bsz, seq, SSM_WIDTH)
    return y.astype(u.dtype)


def ssm_layer(x, norm_g, w_in, lam_re, lam_im, log_dt, b_re, b_im, c_re, c_im, d, w_glu, w_out):
    h = rmsnorm(x, norm_g)
    u, z = jnp.split(h @ w_in, 2, axis=-1)
    y = jax.nn.gelu(s5_scan(u, lam_re, lam_im, log_dt, b_re, b_im, c_re, c_im, d))
    ga, gb = jnp.split(y @ w_glu, 2, axis=-1)
    y = ga * jax.nn.sigmoid(gb)
    return (y * jax.nn.silu(z)) @ w_out


def shared_kv(x, kv_norm_g, w_k, w_v, k_norm_g):
    bsz, seq, _ = x.shape
    h = rmsnorm(x, kv_norm_g)
    k = rmsnorm((h @ w_k).reshape(bsz, seq, N_HEADS, 2, HEAD_DIM), k_norm_g)
    v = (h @ w_v).reshape(bsz, seq, N_HEADS, V_DIM)
    return k, v


def rel_bucket(rel):
    n = jnp.maximum(rel, 0)
    nf = jnp.maximum(n, 1).astype(jnp.float32)
    large = REL_MAX_EXACT + (jnp.log(nf / REL_MAX_EXACT) / math.log(REL_MAX_DIST / REL_MAX_EXACT)
                             * (REL_BUCKETS - REL_MAX_EXACT)).astype(jnp.int32)
    large = jnp.minimum(large, REL_BUCKETS - 1)
    return jnp.where(n < REL_MAX_EXACT, n, large)


def diff_attn_layer(x, k, v, rel_bias, layer_idx, norm_g, w_in, q_norm_g,
                    lam_q1, lam_k1, lam_q2, lam_k2, subln_g, w_out):
    f32 = jnp.float32
    bsz, seq, _ = x.shape
    h = rmsnorm(x, norm_g)
    q, z = jnp.split(h @ w_in, 2, axis=-1)
    q = rmsnorm(q.reshape(bsz, seq, N_HEADS, 2, HEAD_DIM), q_norm_g)
    lam_init = 0.8 - 0.6 * math.exp(-0.3 * layer_idx)
    lam = (jnp.exp(jnp.sum(lam_q1.astype(f32) * lam_k1.astype(f32)))
           - jnp.exp(jnp.sum(lam_q2.astype(f32) * lam_k2.astype(f32))) + lam_init)
    n_blocks = seq // Q_BLOCK
    qb = q.reshape(bsz, n_blocks, Q_BLOCK, N_HEADS, 2, HEAD_DIM).transpose(1, 0, 2, 3, 4, 5)
    k_pos = jnp.arange(seq, dtype=jnp.int32)
    scale = HEAD_DIM ** -0.5
    table = rel_bias.astype(f32)

    def block(args):
        q_blk, blk = args
        q_pos = blk * Q_BLOCK + jnp.arange(Q_BLOCK, dtype=jnp.int32)
        rel = q_pos[:, None] - k_pos[None, :]
        bias = table[rel_bucket(rel)].transpose(2, 0, 1)
        s = jnp.einsum('bqhcd,bkhcd->bchqk', q_blk, k, preferred_element_type=f32) * scale + bias
        s = jnp.where(rel >= 0, s, NEG_INF)
        pm = jax.nn.softmax(s, axis=-1)
        attn = pm[:, 0] - lam * pm[:, 1]
        return jnp.einsum('bhqk,bkhe->bqhe', attn.astype(v.dtype), v)

    o = lax.map(block, (qb, jnp.arange(n_blocks, dtype=jnp.int32)))
    o = o.transpose(1, 0, 2, 3, 4).reshape(bsz, seq, N_HEADS, V_DIM)
    o = (rmsnorm(o, subln_g) * (1.0 - lam_init)).reshape(bsz, seq, ATTN_WIDTH)
    return (o * jax.nn.silu(z)) @ w_out


def per_layer_embed(x, p_i, w_proj, w_gate):
    return x + jax.nn.sigmoid(rmsnorm(x) @ w_gate) * (p_i @ w_proj)


def setup_inputs(seed: int = 0) -> dict:
    key = jax.random.key(seed)
    ks = iter(jax.random.split(key, 40))

    def nrm(shape, std):
        return jax.random.normal(next(ks), shape, jnp.float32) * std

    def gain(shape):
        return 1.0 + nrm(shape, 0.02)

    E, G, P, C = SSM_WIDTH, SSM_GROUPS, SSM_STATE, SSM_GROUP
    inp = {}
    inp['x'] = nrm((BATCH, SEQ, D_MODEL), 1.0)
    inp['p'] = nrm((DEPTH, BATCH, SEQ, PLE_DIM), 1.0)
    inp['a_norm_g'] = gain((N_A, D_MODEL))
    inp['a_w_in'] = nrm((N_A, D_MODEL, 2 * E), D_MODEL ** -0.5)
    inp['a_lam_re'] = -0.5 + nrm((N_A, G, P), 0.01)
    inp['a_lam_im'] = jnp.float32(math.pi) * jnp.arange(P, dtype=jnp.float32) + nrm((N_A, G, P), 0.01)
    inp['a_log_dt'] = jax.random.uniform(next(ks), (N_A, G), jnp.float32, math.log(DT_MIN), math.log(DT_MAX))
    inp['a_b_re'] = nrm((N_A, G, P, C), (2 * C) ** -0.5)
    inp['a_b_im'] = nrm((N_A, G, P, C), (2 * C) ** -0.5)
    inp['a_c_re'] = nrm((N_A, G, C, P), 0.5)
    inp['a_c_im'] = nrm((N_A, G, C, P), 0.5)
    inp['a_d'] = nrm((N_A, E), 1.0)
    inp['a_w_glu'] = nrm((N_A, E, 2 * E), E ** -0.5)
    inp['a_w_out'] = nrm((N_A, E, D_MODEL), E ** -0.5)
    inp['kv_norm_g'] = gain((D_MODEL,))
    inp['w_k'] = nrm((D_MODEL, N_HEADS * 2 * HEAD_DIM), D_MODEL ** -0.5)
    inp['w_v'] = nrm((D_MODEL, N_HEADS * V_DIM), D_MODEL ** -0.5)
    inp['k_norm_g'] = gain((HEAD_DIM,))
    inp['b_norm_g'] = gain((N_B, D_MODEL))
    inp['b_w_in'] = nrm((N_B, D_MODEL, N_HEADS * 2 * HEAD_DIM + ATTN_WIDTH), D_MODEL ** -0.5)
    inp['b_q_norm_g'] = gain((N_B, HEAD_DIM))
    inp['b_lam_q1'] = nrm((N_B, HEAD_DIM), 0.1)
    inp['b_lam_k1'] = nrm((N_B, HEAD_DIM), 0.1)
    inp['b_lam_q2'] = nrm((N_B, HEAD_DIM), 0.1)
    inp['b_lam_k2'] = nrm((N_B, HEAD_DIM), 0.1)
    inp['b_subln_g'] = gain((N_B, V_DIM))
    inp['b_w_out'] = nrm((N_B, ATTN_WIDTH, D_MODEL), ATTN_WIDTH ** -0.5)
    inp['rel_bias'] = nrm((REL_BUCKETS, N_HEADS), 0.5)
    inp['ple_w_proj'] = nrm((DEPTH, PLE_DIM, D_MODEL), 0.5 * PLE_DIM ** -0.5)
    inp['ple_w_gate'] = nrm((DEPTH, D_MODEL, D_MODEL), D_MODEL ** -0.5)
    return inp


def reference(x, p, a_norm_g, a_w_in, a_lam_re, a_lam_im, a_log_dt, a_b_re, a_b_im,
              a_c_re, a_c_im, a_d, a_w_glu, a_w_out, kv_norm_g, w_k, w_v, k_norm_g,
              b_norm_g, b_w_in, b_q_norm_g, b_lam_q1, b_lam_k1, b_lam_q2, b_lam_k2,
              b_subln_g, b_w_out, rel_bias, ple_w_proj, ple_w_gate):
    k = v = None
    for i in range(DEPTH):
        if i < N_A:
            j = i
            x = x + ssm_layer(x, a_norm_g[j], a_w_in[j], a_lam_re[j], a_lam_im[j], a_log_dt[j],
                              a_b_re[j], a_b_im[j], a_c_re[j], a_c_im[j], a_d[j], a_w_glu[j], a_w_out[j])
        else:
            if i == N_A:
                k, v = shared_kv(x, kv_norm_g, w_k, w_v, k_norm_g)
            j = i - N_A
            x = x + diff_attn_layer(x, k, v, rel_bias, i, b_norm_g[j], b_w_in[j], b_q_norm_g[j],
                                    b_lam_q1[j], b_lam_k1[j], b_lam_q2[j], b_lam_k2[j],
                                    b_subln_g[j], b_w_out[j])
        x = per_layer_embed(x, p[i], ple_w_proj[i], ple_w_gate[i])
    return x
```

```python
import functools
import math

import jax
import jax.numpy as jnp
import numpy as np
from jax import lax
from jax.experimental import pallas as pl
from jax.experimental.pallas import tpu as pltpu

F32 = jnp.float32
BF16 = jnp.bfloat16

SUBLANES = 8
LANES = 128
VMEM_LIMIT_BYTES = 56 * 1024 * 1024

EPS = 1e-6
NEG_INF = -1e30
LOG2E = 1.4426950408889634

SSM_GROUP = 16
SSM_STATE = 64
GROUPS_PER_TILE = LANES // SSM_GROUP
STATE_LANES = GROUPS_PER_TILE * SSM_STATE
HEAD_DIM = 64
V_DIM = 2 * HEAD_DIM
REL_BUCKETS = 32
REL_MAX_EXACT = REL_BUCKETS // 2
REL_MAX_DIST = 128

SSM_TIME_CHUNK = 128
ROW_TILE = 256
ATTN_TQ = 256
ATTN_TK = 256


def _rms(x, g=None):
    y = x * lax.rsqrt(jnp.mean(x * x, axis=-1, keepdims=True) + EPS)
    return y if g is None else y * g


def _dot(a, b):
    return jnp.dot(a, b, preferred_element_type=F32)


def _sigmoid(x):
    return 1.0 / (1.0 + jnp.exp(-x))


def _gelu_tanh(x):
    c = math.sqrt(2.0 / math.pi)
    return 0.5 * x * (1.0 + jnp.tanh(c * (x + 0.044715 * (x * x * x))))


def _const_spec(shape):
    nd = len(shape)
    return pl.BlockSpec(shape, lambda *_: (0,) * nd, pipeline_mode=pl.Buffered(1))


def _ssm_kernel(x_ref, g_ref, win_ref, bbig_ref, cbig_ref, are_ref, aim_ref, d_ref,
                y_ref, zg_ref,
                hr_ref, hi_ref, us_ref, xs_ref, hs_ref, ys_ref, *, pitch):
    bsz, tc, dm = x_ref.shape
    e = y_ref.shape[-1]
    n_tiles = e // LANES
    rows = bsz * tc

    @pl.when(pl.program_id(0) == 0)
    def _():
        hr_ref[...] = jnp.zeros_like(hr_ref)
        hi_ref[...] = jnp.zeros_like(hi_ref)

    x = x_ref[...].reshape(rows, dm)
    h = _rms(x, g_ref[...]).astype(BF16)
    z = _dot(h, win_ref[:, e:])
    zg_ref[...] = (z * _sigmoid(z)).astype(BF16).reshape(bsz, tc, e)
    u = _dot(h, win_ref[:, :e])
    for j in range(n_tiles):
        for b in range(bsz):
            us_ref[j, b * pitch:b * pitch + tc, :] = u[b * tc:(b + 1) * tc, j * LANES:(j + 1) * LANES]

    for j in range(n_tiles):
        u_tb = jnp.concatenate(
            [us_ref[j, pl.ds(t, bsz, stride=pitch), :] for t in range(tc)], axis=0)
        xs_ref[...] = _dot(u_tb.astype(BF16), bbig_ref[j])
        ar = jnp.broadcast_to(are_ref[j], (bsz, STATE_LANES))
        ai = jnp.broadcast_to(aim_ref[j], (bsz, STATE_LANES))

        def step(t, carry):
            sr, si = carry
            r0 = pl.multiple_of(t * bsz, SUBLANES)
            xr = xs_ref[pl.ds(r0, bsz), :STATE_LANES]
            xi = xs_ref[pl.ds(r0, bsz), STATE_LANES:]
            nr = ar * sr - ai * si + xr
            ni = ar * si + ai * sr + xi
            hs_ref[pl.ds(r0, bsz), :STATE_LANES] = nr
            hs_ref[pl.ds(r0, bsz), STATE_LANES:] = ni
            return nr, ni

        sr, si = lax.fori_loop(0, tc, step, (hr_ref[j], hi_ref[j]), unroll=8)
        hr_ref[j] = sr
        hi_ref[j] = si

        y_tb = _dot(hs_ref[...].astype(BF16), cbig_ref[j])
        y_tb = y_tb + d_ref[:, j * LANES:(j + 1) * LANES] * u_tb
        ys_ref[...] = _gelu_tanh(y_tb)
        for b in range(bsz):
            y_ref[b, :, j * LANES:(j + 1) * LANES] = ys_ref[pl.ds(b, tc, stride=bsz), :].astype(BF16)


def _ssm_call(x, norm_g, w_in, bbig, cbig, a_re, a_im, d):
    bsz, seq, dm = x.shape
    e = d.shape[-1]
    tc = SSM_TIME_CHUNK
    assert bsz == SUBLANES and seq % tc == 0 and e % LANES == 0 and tc % (2 * SUBLANES) == 0
    pitch = tc + SUBLANES
    n_tiles = e // LANES
    rows = bsz * tc
    out_shape = [jax.ShapeDtypeStruct((bsz, seq, e), BF16)] * 2
    blk = lambda i: (0, i, 0)
    return pl.pallas_call(
        functools.partial(_ssm_kernel, pitch=pitch),
        grid=(seq // tc,),
        in_specs=[pl.BlockSpec((bsz, tc, dm), blk),
                  _const_spec(norm_g.shape), _const_spec(w_in.shape), _const_spec(bbig.shape),
                  _const_spec(cbig.shape), _const_spec(a_re.shape), _const_spec(a_im.shape),
                  _const_spec(d.shape)],
        out_specs=[pl.BlockSpec((bsz, tc, e), blk)] * 2,
        out_shape=out_shape,
        scratch_shapes=[pltpu.VMEM((n_tiles, bsz, STATE_LANES), F32),
                        pltpu.VMEM((n_tiles, bsz, STATE_LANES), F32),
                        pltpu.VMEM((n_tiles, bsz * pitch, LANES), F32),
                        pltpu.VMEM((rows, 2 * STATE_LANES), F32),
                        pltpu.VMEM((rows, 2 * STATE_LANES), F32),
                        pltpu.VMEM((rows, LANES), F32)],
        compiler_params=pltpu.CompilerParams(dimension_semantics=("arbitrary",),
                                             vmem_limit_bytes=VMEM_LIMIT_BYTES),
        name="ssm_front",
    )(x, norm_g, w_in, bbig, cbig, a_re, a_im, d)


def _head_norm(t, gain_row, ones_ref, expand_ref):
    ss = _dot((t * t).astype(BF16), ones_ref[...])
    inv = lax.rsqrt(ss * (1.0 / HEAD_DIM) + EPS)
    hi = inv.astype(BF16)
    lo = (inv - hi.astype(F32)).astype(BF16)
    inv_full = _dot(jnp.concatenate([hi, lo], axis=1), expand_ref[...])
    return t * inv_full * gain_row


def _ple(x1, p, wgate_ref, wproj_ref):
    gate = _sigmoid(_dot(_rms(x1).astype(BF16), wgate_ref[...]))
    return x1 + gate * _dot(p.astype(BF16), wproj_ref[...])


def _mid_kernel(y_ref, zg_ref, x_ref, p_ref,
                wglu_ref, wout_ref, wgate_ref, wproj_ref,
                kvg_ref, wk_ref, wv_ref, kng_ref,
                bng_ref, bwin_ref, qng_ref, ones_ref, expand_ref,
                x2_ref, qt_ref, k_ref, vt_ref, zg1_ref):
    e = y_ref.shape[-1]
    n_heads = k_ref.shape[0]
    aw = n_heads * V_DIM
    gab = _dot(y_ref[...], wglu_ref[...])
    gl = gab[:, :e] * _sigmoid(gab[:, e:]) * zg_ref[...].astype(F32)
    x1 = x_ref[...] + _dot(gl.astype(BF16), wout_ref[...])
    x2 = _ple(x1, p_ref[...], wgate_ref, wproj_ref)
    x2_ref[...] = x2

    hk = _rms(x2, kvg_ref[...]).astype(BF16)
    k = _head_norm(_dot(hk, wk_ref[...]), kng_ref[...], ones_ref, expand_ref)
    v = _dot(hk, wv_ref[...])
    hq = _rms(x2, bng_ref[...]).astype(BF16)
    q = _head_norm(_dot(hq, bwin_ref[:, :aw]), qng_ref[...], ones_ref, expand_ref)
    z1 = _dot(hq, bwin_ref[:, aw:])
    zg1_ref[...] = (z1 * _sigmoid(z1)).astype(BF16)
    for hh in range(n_heads):
        sl = slice(hh * V_DIM, (hh + 1) * V_DIM)
        k_ref[hh] = k[:, sl].astype(BF16)
        qt_ref[hh] = q[:, sl].T.astype(BF16)
        vt_ref[hh] = v[:, sl].T.astype(BF16)


def _mid_call(y, zg, x, p, wglu, wout, wgate, wproj, kvg, wk, wv, kng, bng, bwin, qng, ones, expand):
    bsz, seq, dm = x.shape
    e = y.shape[-1]
    aw = wk.shape[-1]
    n_heads = aw // V_DIM
    tm = ROW_TILE
    assert seq % tm == 0
    row = lambda b, i: (b, i, 0)
    out_shape = [jax.ShapeDtypeStruct((bsz, seq, dm), F32),
                 jax.ShapeDtypeStruct((bsz, n_heads, V_DIM, seq), BF16),
                 jax.ShapeDtypeStruct((bsz, n_heads, seq, V_DIM), BF16),
                 jax.ShapeDtypeStruct((bsz, n_heads, V_DIM, seq), BF16),
                 jax.ShapeDtypeStruct((bsz, seq, aw), BF16)]
    t_spec = pl.BlockSpec((None, n_heads, V_DIM, tm), lambda b, i: (b, 0, 0, i))
    weights = (wglu, wout, wgate, wproj, kvg, wk, wv, kng, bng, bwin, qng, ones, expand)
    return pl.pallas_call(
        _mid_kernel,
        grid=(bsz, seq // tm),
        in_specs=[pl.BlockSpec((None, tm, e), row), pl.BlockSpec((None, tm, e), row),
                  pl.BlockSpec((None, tm, dm), row), pl.BlockSpec((None, tm, p.shape[-1]), row)]
                 + [_const_spec(w.shape) for w in weights],
        out_specs=[pl.BlockSpec((None, tm, dm), row), t_spec,
                   pl.BlockSpec((None, n_heads, tm, V_DIM), lambda b, i: (b, 0, i, 0)), t_spec,
                   pl.BlockSpec((None, tm, aw), row)],
        out_shape=out_shape,
        compiler_params=pltpu.CompilerParams(dimension_semantics=("parallel", "parallel"),
                                             vmem_limit_bytes=VMEM_LIMIT_BYTES),
        name="mid_rowwise",
    )(y, zg, x, p, *weights)


def _bucket_of(rel):
    n = jnp.maximum(rel, 0)
    nf = jnp.maximum(n, 1).astype(F32)
    large = REL_MAX_EXACT + (jnp.log(nf / REL_MAX_EXACT) / math.log(REL_MAX_DIST / REL_MAX_EXACT)
                             * (REL_BUCKETS - REL_MAX_EXACT)).astype(jnp.int32)
    large = jnp.minimum(large, REL_BUCKETS - 1)
    return jnp.where(n < REL_MAX_EXACT, n, large)


def _bias_kernel(table_ref, ids_ref, out_ref):
    hh = pl.program_id(0)
    ids = ids_ref[...]
    far = table_ref[REL_BUCKETS - 1, hh]
    acc = jnp.full(ids.shape, NEG_INF, F32)
    for bkt in range(REL_BUCKETS):
        acc = jnp.where(ids == bkt, (table_ref[bkt, hh] - far) * LOG2E, acc)
    out_ref[...] = acc


def _bias_call(table, ids):
    n_heads = table.shape[1]
    return pl.pallas_call(
        _bias_kernel,
        grid=(n_heads,),
        in_specs=[pl.BlockSpec(memory_space=pltpu.SMEM), _const_spec(ids.shape)],
        out_specs=pl.BlockSpec((None,) + ids.shape, lambda hh: (hh, 0, 0, 0)),
        out_shape=jax.ShapeDtypeStruct((n_heads,) + ids.shape, F32),
        name="rel_bias_tiles",
    )(table, ids)


def _attn_kernel(lam_ref, qt_ref, k_ref, vt_ref, bias_ref, sg_ref, o_ref,
                 m_ref, l_ref, acc_ref, *, out_scale):
    tq = qt_ref.shape[-1]
    tk = ATTN_TK
    i = pl.program_id(2)
    qt = qt_ref[...]
    top = lax.broadcasted_iota(jnp.int32, qt.shape, 0) < HEAD_DIM
    zero = jnp.zeros_like(qt)
    qz = (jnp.where(top, qt, zero), jnp.where(top, zero, qt))

    m_ref[...] = jnp.full(m_ref.shape, NEG_INF, F32)
    l_ref[...] = jnp.zeros_like(l_ref)
    acc_ref[...] = jnp.zeros_like(acc_ref)

    def block(start, bias):
        kb = k_ref[pl.ds(start, tk), :]
        vb = vt_ref[:, pl.ds(start, tk)]
        for c in range(2):
            s = _dot(kb, qz[c])
            if bias is not None:
                s = s + bias
            m_old = m_ref[c]
            m_new = jnp.maximum(m_old, jnp.max(s, axis=0, keepdims=True))
            alpha = jnp.exp2(m_old - m_new)
            p = jnp.exp2(s - m_new)
            l_ref[c] = alpha * l_ref[c] + jnp.sum(p, axis=0, keepdims=True)
            acc_ref[c] = alpha * acc_ref[c] + _dot(vb, p.astype(BF16))
            m_ref[c] = m_new

    def far_body(j, carry):
        block(pl.multiple_of(j * tk, tk), None)
        return carry

    lax.fori_loop(0, jnp.maximum(i - 1, 0), far_body, 0)

    @pl.when(i > 0)
    def _():
        block(pl.multiple_of((i - 1) * tk, tk), bias_ref[1])

    block(pl.multiple_of(i * tk, tk), bias_ref[0])

    o1 = acc_ref[0] / l_ref[0]
    o2 = acc_ref[1] / l_ref[1]
    o = o1 - lam_ref[...] * o2
    o = o * lax.rsqrt(jnp.mean(o * o, axis=0, keepdims=True) + EPS) * (sg_ref[...] * out_scale)
    o_ref[...] = o.T.astype(BF16)


def _attn_call(lam, qt, k, vt, bias, subln_g, out_scale):
    bsz, n_heads, _, seq = qt.shape
    tq = ATTN_TQ
    assert ATTN_TK == tq and seq % tq == 0
    return pl.pallas_call(
        functools.partial(_attn_kernel, out_scale=out_scale),
        grid=(bsz, n_heads, seq // tq),
        in_specs=[_const_spec(lam.shape),
                  pl.BlockSpec((None, None, V_DIM, tq), lambda b, hh, i: (b, hh, 0, i)),
                  pl.BlockSpec((None, None, seq, V_DIM), lambda b, hh, i: (b, hh, 0, 0)),
                  pl.BlockSpec((None, None, V_DIM, seq), lambda b, hh, i: (b, hh, 0, 0)),
                  pl.BlockSpec((None,) + bias.shape[1:], lambda b, hh, i: (hh, 0, 0, 0)),
                  _const_spec(subln_g.shape)],
        out_specs=pl.BlockSpec((None, tq, V_DIM), lambda b, hh, i: (b, i, hh)),
        out_shape=jax.ShapeDtypeStruct((bsz, seq, n_heads * V_DIM), BF16),
        scratch_shapes=[pltpu.VMEM((2, 1, tq), F32), pltpu.VMEM((2, 1, tq), F32),
                        pltpu.VMEM((2, V_DIM, tq), F32)],
        compiler_params=pltpu.CompilerParams(dimension_semantics=("parallel", "parallel", "arbitrary"),
                                             vmem_limit_bytes=VMEM_LIMIT_BYTES),
        name="diff_attention",
    )(lam, qt, k, vt, bias, subln_g)


def _final_kernel(o_ref, zg_ref, x_ref, p_ref, wout_ref, wgate_ref, wproj_ref, out_ref):
    g = (o_ref[...].astype(F32) * zg_ref[...].astype(F32)).astype(BF16)
    x1 = x_ref[...] + _dot(g, wout_ref[...])
    out_ref[...] = _ple(x1, p_ref[...], wgate_ref, wproj_ref)


def _final_call(o, zg, x, p, wout, wgate, wproj):
    bsz, seq, dm = x.shape
    tm = ROW_TILE
    row = lambda b, i: (b, i, 0)
    return pl.pallas_call(
        _final_kernel,
        grid=(bsz, seq // tm),
        in_specs=[pl.BlockSpec((None, tm, o.shape[-1]), row), pl.BlockSpec((None, tm, zg.shape[-1]), row),
                  pl.BlockSpec((None, tm, dm), row), pl.BlockSpec((None, tm, p.shape[-1]), row),
                  _const_spec(wout.shape), _const_spec(wgate.shape), _const_spec(wproj.shape)],
        out_specs=pl.BlockSpec((None, tm, dm), row),
        out_shape=jax.ShapeDtypeStruct((bsz, seq, dm), F32),
        compiler_params=pltpu.CompilerParams(dimension_semantics=("parallel", "parallel"),
                                             vmem_limit_bytes=VMEM_LIMIT_BYTES),
        name="final_rowwise",
    )(o, zg, x, p, wout, wgate, wproj)


def _ssm_params(lam_re, lam_im, log_dt, b_re, b_im, c_re, c_im):
    lr, li = lam_re.astype(F32), lam_im.astype(F32)
    dt = jnp.exp(log_dt.astype(F32))[:, None]
    mag = jnp.exp(lr * dt)
    ab_re, ab_im = mag * jnp.cos(li * dt), mag * jnp.sin(li * dt)
    den = lr * lr + li * li
    nr, ni = ab_re - 1.0, ab_im
    f_re = ((nr * lr + ni * li) / den)[..., None]
    f_im = ((ni * lr - nr * li) / den)[..., None]
    br, bi = b_re.astype(F32), b_im.astype(F32)
    bb_re, bb_im = f_re * br - f_im * bi, f_re * bi + f_im * br
    n_tiles = lam_re.shape[0] // GROUPS_PER_TILE
    eye = jnp.eye(GROUPS_PER_TILE, dtype=F32)

    def pack_b(m):
        m = m.reshape(n_tiles, GROUPS_PER_TILE, SSM_STATE, SSM_GROUP)
        return jnp.einsum('jgpc,gh->jgchp', m, eye).reshape(n_tiles, LANES, STATE_LANES)

    def pack_c(m):
        m = m.reshape(n_tiles, GROUPS_PER_TILE, SSM_GROUP, SSM_STATE)
        return jnp.einsum('jgcp,gh->jgphc', m, eye).reshape(n_tiles, STATE_LANES, LANES)

    bbig = jnp.concatenate([pack_b(bb_re), pack_b(bb_im)], axis=2).astype(BF16)
    cbig = jnp.concatenate([pack_c(c_re.astype(F32)), pack_c(-c_im.astype(F32))], axis=1).astype(BF16)
    a_re = ab_re.reshape(n_tiles, 1, STATE_LANES)
    a_im = ab_im.reshape(n_tiles, 1, STATE_LANES)
    return bbig, cbig, a_re, a_im


def _bias_ids(tk, tq, seq):
    kk = jnp.arange(tk, dtype=jnp.int32)[:, None]
    qq = jnp.arange(tq, dtype=jnp.int32)[None, :]
    rel0 = qq - kk
    rel1 = rel0 + tk
    ids0 = jnp.where(rel0 >= 0, _bucket_of(rel0), -1)
    far = np.arange(tk + 1, max(seq, tk + 2), dtype=np.float32)
    far = REL_MAX_EXACT + (np.log(far / REL_MAX_EXACT) / math.log(REL_MAX_DIST / REL_MAX_EXACT)
                           * (REL_BUCKETS - REL_MAX_EXACT)).astype(np.int32)
    assert far.min() > REL_BUCKETS
    return jnp.stack([ids0, _bucket_of(rel1)])


def kernel(x, p, a_norm_g, a_w_in, a_lam_re, a_lam_im, a_log_dt, a_b_re, a_b_im, a_c_re, a_c_im, a_d,
           a_w_glu, a_w_out, kv_norm_g, w_k, w_v, k_norm_g, b_norm_g, b_w_in, b_q_norm_g, b_lam_q1,
           b_lam_k1, b_lam_q2, b_lam_k2, b_subln_g, b_w_out, rel_bias, ple_w_proj, ple_w_gate):
    assert a_norm_g.shape[0] == 1 and b_norm_g.shape[0] == 1 and p.shape[0] == 2
    bsz, seq, dm = x.shape
    aw = w_k.shape[-1]
    n_half = aw // HEAD_DIM
    row = lambda v: v.reshape(1, -1).astype(F32)

    bbig, cbig, a_re, a_im = _ssm_params(a_lam_re[0], a_lam_im[0], a_log_dt[0], a_b_re[0], a_b_im[0],
                                         a_c_re[0], a_c_im[0])
    y, zg = _ssm_call(x, row(a_norm_g[0]), a_w_in[0].astype(BF16), bbig, cbig, a_re, a_im, row(a_d[0]))

    lane_group = jnp.arange(aw, dtype=jnp.int32) // HEAD_DIM
    ones = (lane_group[:, None] == jnp.arange(LANES, dtype=jnp.int32)[None, :]).astype(BF16)
    expand = jnp.concatenate([ones.T, ones.T], axis=0)
    q_scale = HEAD_DIM ** -0.5 * LOG2E
    x2, qt, k, vt, zg1 = _mid_call(
        y, zg, x, p[0], a_w_glu[0].astype(BF16), a_w_out[0].astype(BF16),
        ple_w_gate[0].astype(BF16), ple_w_proj[0].astype(BF16),
        row(kv_norm_g), w_k.astype(BF16), w_v.astype(BF16), row(jnp.tile(k_norm_g, n_half)),
        row(b_norm_g[0]), b_w_in[0].astype(BF16), row(jnp.tile(b_q_norm_g[0], n_half) * q_scale),
        ones, expand)

    layer_idx = 1
    lam_init = 0.8 - 0.6 * math.exp(-0.3 * layer_idx)
    lam = (jnp.exp(jnp.sum(b_lam_q1[0].astype(F32) * b_lam_k1[0].astype(F32)))
           - jnp.exp(jnp.sum(b_lam_q2[0].astype(F32) * b_lam_k2[0].astype(F32))) + lam_init).reshape(1, 1)
    bias = _bias_call(rel_bias.astype(F32), _bias_ids(ATTN_TK, ATTN_TQ, seq))
    o = _attn_call(lam, qt, k, vt, bias, b_subln_g[0].astype(F32).reshape(V_DIM, 1), 1.0 - lam_init)
    return _final_call(o, zg1, x2, p[1], b_w_out[0].astype(BF16), ple_w_gate[1].astype(BF16),
                       ple_w_proj[1].astype(BF16))
```

```python
import functools
import math

import jax
import jax.numpy as jnp
import numpy as np
from jax import lax
from jax.experimental import pallas as pl
from jax.experimental.pallas import tpu as pltpu

F32 = jnp.float32
BF16 = jnp.bfloat16

SUBLANES = 8
LANES = 128
VMEM_LIMIT_BYTES = 56 * 1024 * 1024

EPS = 1e-6
NEG_INF = -1e30
LOG2E = 1.4426950408889634

SSM_GROUP = 16
SSM_STATE = 64
GROUPS_PER_TILE = LANES // SSM_GROUP
STATE_LANES = GROUPS_PER_TILE * SSM_STATE
HEAD_DIM = 64
V_DIM = 2 * HEAD_DIM
REL_BUCKETS = 32
REL_MAX_EXACT = REL_BUCKETS // 2
REL_MAX_DIST = 128

SSM_TIME_CHUNK = 128
ROW_TILE = 256
ATTN_TQ = 512
ATTN_TS = 256


def _rms(x, g=None):
    y = x * lax.rsqrt(jnp.mean(x * x, axis=-1, keepdims=True) + EPS)
    return y if g is None else y * g


def _dot(a, b):
    return jnp.dot(a, b, preferred_element_type=F32)


def _sigmoid(x):
    return 1.0 / (1.0 + jnp.exp(-x))


def _gelu_tanh(x):
    c = math.sqrt(2.0 / math.pi)
    return 0.5 * x * (1.0 + jnp.tanh(c * (x + 0.044715 * (x * x * x))))


def _const_spec(shape):
    nd = len(shape)
    return pl.BlockSpec(shape, lambda *_: (0,) * nd, pipeline_mode=pl.Buffered(1))


def _ssm_kernel(x_ref, g_ref, win_ref, bbig_ref, cbig_ref, are_ref, aim_ref, d_ref,
                y_ref, zg_ref,
                hr_ref, hi_ref, us_ref, xs_ref, hs_ref, ys_ref, *, pitch):
    bsz, tc, dm = x_ref.shape
    e = y_ref.shape[-1]
    n_tiles = e // LANES
    rows = bsz * tc

    @pl.when(pl.program_id(0) == 0)
    def _():
        hr_ref[...] = jnp.zeros_like(hr_ref)
        hi_ref[...] = jnp.zeros_like(hi_ref)

    x = x_ref[...].reshape(rows, dm)
    h = _rms(x, g_ref[...]).astype(BF16)
    z = _dot(h, win_ref[:, e:])
    zg_ref[...] = (z * _sigmoid(z)).astype(BF16).reshape(bsz, tc, e)
    u = _dot(h, win_ref[:, :e])
    for j in range(n_tiles):
        for b in range(bsz):
            us_ref[j, b * pitch:b * pitch + tc, :] = u[b * tc:(b + 1) * tc, j * LANES:(j + 1) * LANES]

    for j in range(n_tiles):
        u_tb = jnp.concatenate(
            [us_ref[j, pl.ds(t, bsz, stride=pitch), :] for t in range(tc)], axis=0)
        xs_ref[...] = _dot(u_tb.astype(BF16), bbig_ref[j])
        ar = jnp.broadcast_to(are_ref[j], (bsz, STATE_LANES))
        ai = jnp.broadcast_to(aim_ref[j], (bsz, STATE_LANES))

        def step(t, carry):
            sr, si = carry
            r0 = pl.multiple_of(t * bsz, SUBLANES)
            xr = xs_ref[pl.ds(r0, bsz), :STATE_LANES]
            xi = xs_ref[pl.ds(r0, bsz), STATE_LANES:]
            nr = ar * sr - ai * si + xr
            ni = ar * si + ai * sr + xi
            hs_ref[pl.ds(r0, bsz), :STATE_LANES] = nr
            hs_ref[pl.ds(r0, bsz), STATE_LANES:] = ni
            return nr, ni

        sr, si = lax.fori_loop(0, tc, step, (hr_ref[j], hi_ref[j]), unroll=8)
        hr_ref[j] = sr
        hi_ref[j] = si

        y_tb = _dot(hs_ref[...].astype(BF16), cbig_ref[j])
        y_tb = y_tb + d_ref[:, j * LANES:(j + 1) * LANES] * u_tb
        ys_ref[...] = _gelu_tanh(y_tb)
        for b in range(bsz):
            y_ref[b, :, j * LANES:(j + 1) * LANES] = ys_ref[pl.ds(b, tc, stride=bsz), :].astype(BF16)


def _ssm_call(x, norm_g, w_in, bbig, cbig, a_re, a_im, d):
    bsz, seq, dm = x.shape
    e = d.shape[-1]
    tc = SSM_TIME_CHUNK
    assert bsz == SUBLANES and seq % tc == 0 and e % LANES == 0 and tc % (2 * SUBLANES) == 0
    pitch = tc + SUBLANES
    n_tiles = e // LANES
    rows = bsz * tc
    out_shape = [jax.ShapeDtypeStruct((bsz, seq, e), BF16)] * 2
    blk = lambda i: (0, i, 0)
    return pl.pallas_call(
        functools.partial(_ssm_kernel, pitch=pitch),
        grid=(seq // tc,),
        in_specs=[pl.BlockSpec((bsz, tc, dm), blk),
                  _const_spec(norm_g.shape), _const_spec(w_in.shape), _const_spec(bbig.shape),
                  _const_spec(cbig.shape), _const_spec(a_re.shape), _const_spec(a_im.shape),
                  _const_spec(d.shape)],
        out_specs=[pl.BlockSpec((bsz, tc, e), blk)] * 2,
        out_shape=out_shape,
        scratch_shapes=[pltpu.VMEM((n_tiles, bsz, STATE_LANES), F32),
                        pltpu.VMEM((n_tiles, bsz, STATE_LANES), F32),
                        pltpu.VMEM((n_tiles, bsz * pitch, LANES), F32),
                        pltpu.VMEM((rows, 2 * STATE_LANES), F32),
                        pltpu.VMEM((rows, 2 * STATE_LANES), F32),
                        pltpu.VMEM((rows, LANES), F32)],
        compiler_params=pltpu.CompilerParams(dimension_semantics=("arbitrary",),
                                             vmem_limit_bytes=VMEM_LIMIT_BYTES),
        name="ssm_front",
    )(x, norm_g, w_in, bbig, cbig, a_re, a_im, d)


def _head_norm(t, gain_row, ones_ref, expand_ref):
    ss = _dot((t * t).astype(BF16), ones_ref[...])
    inv = lax.rsqrt(ss * (1.0 / HEAD_DIM) + EPS)
    hi = inv.astype(BF16)
    lo = (inv - hi.astype(F32)).astype(BF16)
    inv_full = _dot(jnp.concatenate([hi, lo], axis=1), expand_ref[...])
    return t * inv_full * gain_row


def _ple(x1, p, wgate_ref, wproj_ref):
    gate = _sigmoid(_dot(_rms(x1).astype(BF16), wgate_ref[...]))
    return x1 + gate * _dot(p.astype(BF16), wproj_ref[...])


def _mid_kernel(y_ref, zg_ref, x_ref, p_ref,
                wglu_ref, wout_ref, wgate_ref, wproj_ref,
                kvg_ref, wk_ref, wv_ref, kng_ref,
                bng_ref, bwin_ref, qng_ref, ones_ref, expand_ref,
                x2_ref, qt_ref, k_ref, vt_ref, zg1_ref):
    e = y_ref.shape[-1]
    n_heads = k_ref.shape[0]
    aw = n_heads * V_DIM
    gab = _dot(y_ref[...], wglu_ref[...])
    gl = gab[:, :e] * _sigmoid(gab[:, e:]) * zg_ref[...].astype(F32)
    x1 = x_ref[...] + _dot(gl.astype(BF16), wout_ref[...])
    x2 = _ple(x1, p_ref[...], wgate_ref, wproj_ref)
    x2_ref[...] = x2

    hk = _rms(x2, kvg_ref[...]).astype(BF16)
    k = _head_norm(_dot(hk, wk_ref[...]), kng_ref[...], ones_ref, expand_ref)
    v = _dot(hk, wv_ref[...])
    hq = _rms(x2, bng_ref[...]).astype(BF16)
    q = _head_norm(_dot(hq, bwin_ref[:, :aw]), qng_ref[...], ones_ref, expand_ref)
    z1 = _dot(hq, bwin_ref[:, aw:])
    zg1_ref[...] = (z1 * _sigmoid(z1)).astype(BF16)
    for hh in range(n_heads):
        sl = slice(hh * V_DIM, (hh + 1) * V_DIM)
        k_ref[hh] = k[:, sl].astype(BF16)
        qt_ref[hh] = q[:, sl].T.astype(BF16)
        vt_ref[hh] = v[:, sl].T.astype(BF16)


def _mid_call(y, zg, x, p, wglu, wout, wgate, wproj, kvg, wk, wv, kng, bng, bwin, qng, ones, expand):
    bsz, seq, dm = x.shape
    e = y.shape[-1]
    aw = wk.shape[-1]
    n_heads = aw // V_DIM
    tm = ROW_TILE
    assert seq % tm == 0
    row = lambda b, i: (b, i, 0)
    out_shape = [jax.ShapeDtypeStruct((bsz, seq, dm), F32),
                 jax.ShapeDtypeStruct((bsz, n_heads, V_DIM, seq), BF16),
                 jax.ShapeDtypeStruct((bsz, n_heads, seq, V_DIM), BF16),
                 jax.ShapeDtypeStruct((bsz, n_heads, V_DIM, seq), BF16),
                 jax.ShapeDtypeStruct((bsz, seq, aw), BF16)]
    t_spec = pl.BlockSpec((None, n_heads, V_DIM, tm), lambda b, i: (b, 0, 0, i))
    weights = (wglu, wout, wgate, wproj, kvg, wk, wv, kng, bng, bwin, qng, ones, expand)
    return pl.pallas_call(
        _mid_kernel,
        grid=(bsz, seq // tm),
        in_specs=[pl.BlockSpec((None, tm, e), row), pl.BlockSpec((None, tm, e), row),
                  pl.BlockSpec((None, tm, dm), row), pl.BlockSpec((None, tm, p.shape[-1]), row)]
                 + [_const_spec(w.shape) for w in weights],
        out_specs=[pl.BlockSpec((None, tm, dm), row), t_spec,
                   pl.BlockSpec((None, n_heads, tm, V_DIM), lambda b, i: (b, 0, i, 0)), t_spec,
                   pl.BlockSpec((None, tm, aw), row)],
        out_shape=out_shape,
        compiler_params=pltpu.CompilerParams(dimension_semantics=("parallel", "parallel"),
                                             vmem_limit_bytes=VMEM_LIMIT_BYTES),
        name="mid_rowwise",
    )(y, zg, x, p, *weights)


def _bucket_of(rel):
    n = jnp.maximum(rel, 0)
    nf = jnp.maximum(n, 1).astype(F32)
    large = REL_MAX_EXACT + (jnp.log(nf / REL_MAX_EXACT) / math.log(REL_MAX_DIST / REL_MAX_EXACT)
                             * (REL_BUCKETS - REL_MAX_EXACT)).astype(jnp.int32)
    large = jnp.minimum(large, REL_BUCKETS - 1)
    return jnp.where(n < REL_MAX_EXACT, n, large)


def _bias_kernel(table_ref, ids_ref, out_ref):
    hh = pl.program_id(0)
    ids = ids_ref[...]
    far = table_ref[REL_BUCKETS - 1, hh]
    acc = jnp.full(ids.shape, NEG_INF, F32)
    for bkt in range(REL_BUCKETS):
        acc = jnp.where(ids == bkt, (table_ref[bkt, hh] - far) * LOG2E, acc)
    out_ref[...] = acc


def _bias_call(table, ids):
    n_heads = table.shape[1]
    return pl.pallas_call(
        _bias_kernel,
        grid=(n_heads,),
        in_specs=[pl.BlockSpec(memory_space=pltpu.SMEM), _const_spec(ids.shape)],
        out_specs=pl.BlockSpec((None,) + ids.shape, lambda hh: (hh, 0, 0, 0)),
        out_shape=jax.ShapeDtypeStruct((n_heads,) + ids.shape, F32),
        name="rel_bias_tiles",
    )(table, ids)


def _attn_kernel(lam_ref, qt_ref, k_ref, vt_ref, bias_ref, sg_ref, o_ref,
                 m_ref, l_ref, acc_ref, sa_ref, sb_ref, *, out_scale):
    tq = qt_ref.shape[-1]
    ts = ATTN_TS
    i = pl.program_id(2)
    qt = qt_ref[...]
    top = lax.broadcasted_iota(jnp.int32, qt.shape, 0) < HEAD_DIM
    zero = jnp.zeros_like(qt)
    qz = (jnp.where(top, qt, zero), jnp.where(top, zero, qt))

    m_ref[...] = jnp.full(m_ref.shape, NEG_INF, F32)
    l_ref[...] = jnp.zeros_like(l_ref)
    acc_ref[...] = jnp.zeros_like(acc_ref)

    def scores(s_ref, start):
        kb = k_ref[pl.ds(pl.multiple_of(start, ts), ts), :]
        for c in range(2):
            s_ref[c] = _dot(kb, qz[c])

    def update(s_ref, start, bias):
        vb = vt_ref[:, pl.ds(pl.multiple_of(start, ts), ts)]
        for c in range(2):
            s = s_ref[c]
            if bias is not None:
                s = s + bias
            m_old = m_ref[c]
            m_new = jnp.maximum(m_old, jnp.max(s, axis=0, keepdims=True))
            alpha = jnp.exp2(m_old - m_new)
            p = jnp.exp2(s - m_new)
            l_ref[c] = alpha * l_ref[c] + jnp.sum(p, axis=0, keepdims=True)
            acc_ref[c] = alpha * acc_ref[c] + _dot(vb, p.astype(BF16))
            m_ref[c] = m_new

    def pair(start, bias_lo, bias_hi, next_start):
        scores(sb_ref, start + ts)
        update(sa_ref, start, bias_lo)
        if next_start is not None:
            scores(sa_ref, next_start)
        update(sb_ref, start + ts, bias_hi)

    scores(sa_ref, 0)

    def far_body(t, carry):
        start = t * tq
        pair(start, None, None, start + tq)
        return carry

    lax.fori_loop(0, jnp.maximum(i - 1, 0), far_body, 0)

    @pl.when(i > 0)
    def _():
        pair((i - 1) * tq, bias_ref[1, :ts], bias_ref[1, ts:], i * tq)

    pair(i * tq, bias_ref[0, :ts], bias_ref[0, ts:], None)

    o1 = acc_ref[0] * (1.0 / l_ref[0])
    o2 = acc_ref[1] * (1.0 / l_ref[1])
    o = o1 - lam_ref[...] * o2
    o = o * lax.rsqrt(jnp.mean(o * o, axis=0, keepdims=True) + EPS) * (sg_ref[...] * out_scale)
    o_ref[...] = o.T.astype(BF16)


def _attn_call(lam, qt, k, vt, bias, subln_g, out_scale):
    bsz, n_heads, _, seq = qt.shape
    tq = ATTN_TQ
    assert tq == 2 * ATTN_TS and seq % tq == 0 and bias.shape[1:] == (2, tq, tq)
    return pl.pallas_call(
        functools.partial(_attn_kernel, out_scale=out_scale),
        grid=(bsz, n_heads, seq // tq),
        in_specs=[_const_spec(lam.shape),
                  pl.BlockSpec((None, None, V_DIM, tq), lambda b, hh, i: (b, hh, 0, i)),
                  pl.BlockSpec((None, None, seq, V_DIM), lambda b, hh, i: (b, hh, 0, 0)),
                  pl.BlockSpec((None, None, V_DIM, seq), lambda b, hh, i: (b, hh, 0, 0)),
                  pl.BlockSpec((None,) + bias.shape[1:], lambda b, hh, i: (hh, 0, 0, 0)),
                  _const_spec(subln_g.shape)],
        out_specs=pl.BlockSpec((None, tq, V_DIM), lambda b, hh, i: (b, i, hh)),
        out_shape=jax.ShapeDtypeStruct((bsz, seq, n_heads * V_DIM), BF16),
        scratch_shapes=[pltpu.VMEM((2, 1, tq), F32), pltpu.VMEM((2, 1, tq), F32),
                        pltpu.VMEM((2, V_DIM, tq), F32),
                        pltpu.VMEM((2, ATTN_TS, tq), F32), pltpu.VMEM((2, ATTN_TS, tq), F32)],
        compiler_params=pltpu.CompilerParams(dimension_semantics=("parallel", "parallel", "arbitrary"),
                                             vmem_limit_bytes=VMEM_LIMIT_BYTES),
        name="diff_attention",
    )(lam, qt, k, vt, bias, subln_g)


def _final_kernel(o_ref, zg_ref, x_ref, p_ref, wout_ref, wgate_ref, wproj_ref, out_ref):
    g = (o_ref[...].astype(F32) * zg_ref[...].astype(F32)).astype(BF16)
    x1 = x_ref[...] + _dot(g, wout_ref[...])
    out_ref[...] = _ple(x1, p_ref[...], wgate_ref, wproj_ref)


def _final_call(o, zg, x, p, wout, wgate, wproj):
    bsz, seq, dm = x.shape
    tm = ROW_TILE
    row = lambda b, i: (b, i, 0)
    return pl.pallas_call(
        _final_kernel,
        grid=(bsz, seq // tm),
        in_specs=[pl.BlockSpec((None, tm, o.shape[-1]), row), pl.BlockSpec((None, tm, zg.shape[-1]), row),
                  pl.BlockSpec((None, tm, dm), row), pl.BlockSpec((None, tm, p.shape[-1]), row),
                  _const_spec(wout.shape), _const_spec(wgate.shape), _const_spec(wproj.shape)],
        out_specs=pl.BlockSpec((None, tm, dm), row),
        out_shape=jax.ShapeDtypeStruct((bsz, seq, dm), F32),
        compiler_params=pltpu.CompilerParams(dimension_semantics=("parallel", "parallel"),
                                             vmem_limit_bytes=VMEM_LIMIT_BYTES),
        name="final_rowwise",
    )(o, zg, x, p, wout, wgate, wproj)


def _ssm_params(lam_re, lam_im, log_dt, b_re, b_im, c_re, c_im):
    lr, li = lam_re.astype(F32), lam_im.astype(F32)
    dt = jnp.exp(log_dt.astype(F32))[:, None]
    mag = jnp.exp(lr * dt)
    ab_re, ab_im = mag * jnp.cos(li * dt), mag * jnp.sin(li * dt)
    den = lr * lr + li * li
    nr, ni = ab_re - 1.0, ab_im
    f_re = ((nr * lr + ni * li) / den)[..., None]
    f_im = ((ni * lr - nr * li) / den)[..., None]
    br, bi = b_re.astype(F32), b_im.astype(F32)
    bb_re, bb_im = f_re * br - f_im * bi, f_re * bi + f_im * br
    n_tiles = lam_re.shape[0] // GROUPS_PER_TILE
    eye = jnp.eye(GROUPS_PER_TILE, dtype=F32)

    def pack_b(m):
        m = m.reshape(n_tiles, GROUPS_PER_TILE, SSM_STATE, SSM_GROUP)
        return jnp.einsum('jgpc,gh->jgchp', m, eye).reshape(n_tiles, LANES, STATE_LANES)

    def pack_c(m):
        m = m.reshape(n_tiles, GROUPS_PER_TILE, SSM_GROUP, SSM_STATE)
        return jnp.einsum('jgcp,gh->jgphc', m, eye).reshape(n_tiles, STATE_LANES, LANES)

    bbig = jnp.concatenate([pack_b(bb_re), pack_b(bb_im)], axis=2).astype(BF16)
    cbig = jnp.concatenate([pack_c(c_re.astype(F32)), pack_c(-c_im.astype(F32))], axis=1).astype(BF16)
    a_re = ab_re.reshape(n_tiles, 1, STATE_LANES)
    a_im = ab_im.reshape(n_tiles, 1, STATE_LANES)
    return bbig, cbig, a_re, a_im


def _bias_ids(tk, tq, seq):
    kk = jnp.arange(tk, dtype=jnp.int32)[:, None]
    qq = jnp.arange(tq, dtype=jnp.int32)[None, :]
    rel0 = qq - kk
    rel1 = rel0 + tk
    ids0 = jnp.where(rel0 >= 0, _bucket_of(rel0), -1)
    far = np.arange(tk + 1, max(seq, tk + 2), dtype=np.float32)
    far = REL_MAX_EXACT + (np.log(far / REL_MAX_EXACT) / math.log(REL_MAX_DIST / REL_MAX_EXACT)
                           * (REL_BUCKETS - REL_MAX_EXACT)).astype(np.int32)
    assert far.min() > REL_BUCKETS
    return jnp.stack([ids0, _bucket_of(rel1)])


def kernel(x, p, a_norm_g, a_w_in, a_lam_re, a_lam_im, a_log_dt, a_b_re, a_b_im, a_c_re, a_c_im, a_d,
           a_w_glu, a_w_out, kv_norm_g, w_k, w_v, k_norm_g, b_norm_g, b_w_in, b_q_norm_g, b_lam_q1,
           b_lam_k1, b_lam_q2, b_lam_k2, b_subln_g, b_w_out, rel_bias, ple_w_proj, ple_w_gate):
    assert a_norm_g.shape[0] == 1 and b_norm_g.shape[0] == 1 and p.shape[0] == 2
    bsz, seq, dm = x.shape
    aw = w_k.shape[-1]
    n_half = aw // HEAD_DIM
    row = lambda v: v.reshape(1, -1).astype(F32)

    bbig, cbig, a_re, a_im = _ssm_params(a_lam_re[0], a_lam_im[0], a_log_dt[0], a_b_re[0], a_b_im[0],
                                         a_c_re[0], a_c_im[0])
    y, zg = _ssm_call(x, row(a_norm_g[0]), a_w_in[0].astype(BF16), bbig, cbig, a_re, a_im, row(a_d[0]))

    lane_group = jnp.arange(aw, dtype=jnp.int32) // HEAD_DIM
    ones = (lane_group[:, None] == jnp.arange(LANES, dtype=jnp.int32)[None, :]).astype(BF16)
    expand = jnp.concatenate([ones.T, ones.T], axis=0)
    q_scale = HEAD_DIM ** -0.5 * LOG2E
    x2, qt, k, vt, zg1 = _mid_call(
        y, zg, x, p[0], a_w_glu[0].astype(BF16), a_w_out[0].astype(BF16),
        ple_w_gate[0].astype(BF16), ple_w_proj[0].astype(BF16),
        row(kv_norm_g), w_k.astype(BF16), w_v.astype(BF16), row(jnp.tile(k_norm_g, n_half)),
        row(b_norm_g[0]), b_w_in[0].astype(BF16), row(jnp.tile(b_q_norm_g[0], n_half) * q_scale),
        ones, expand)

    layer_idx = 1
    lam_init = 0.8 - 0.6 * math.exp(-0.3 * layer_idx)
    lam = (jnp.exp(jnp.sum(b_lam_q1[0].astype(F32) * b_lam_k1[0].astype(F32)))
           - jnp.exp(jnp.sum(b_lam_q2[0].astype(F32) * b_lam_k2[0].astype(F32))) + lam_init).reshape(1, 1)
    bias = _bias_call(rel_bias.astype(F32), _bias_ids(ATTN_TQ, ATTN_TQ, seq))
    o = _attn_call(lam, qt, k, vt, bias, b_subln_g[0].astype(F32).reshape(V_DIM, 1), 1.0 - lam_init)
    return _final_call(o, zg1, x2, p[1], b_w_out[0].astype(BF16), ple_w_gate[1].astype(BF16),
                       ple_w_proj[1].astype(BF16))
```

```python
import functools
import math

import jax
import jax.numpy as jnp
import numpy as np
from jax import lax
from jax.experimental import pallas as pl
from jax.experimental.pallas import tpu as pltpu

F32 = jnp.float32
BF16 = jnp.bfloat16

SUBLANES = 8
LANES = 128
VMEM_LIMIT_BYTES = 56 * 1024 * 1024

EPS = 1e-6
NEG_INF = -1e30
LOG2E = 1.4426950408889634

SSM_GROUP = 16
SSM_STATE = 64
GROUPS_PER_TILE = LANES // SSM_GROUP
STATE_LANES = GROUPS_PER_TILE * SSM_STATE
HEAD_DIM = 64
V_DIM = 2 * HEAD_DIM
REL_BUCKETS = 32
REL_MAX_EXACT = REL_BUCKETS // 2
REL_MAX_DIST = 128

SSM_TIME_CHUNK = 128
ROW_TILE = 256
ATTN_TQ = 512


def _rms(x, g=None):
    y = x * lax.rsqrt(jnp.mean(x * x, axis=-1, keepdims=True) + EPS)
    return y if g is None else y * g


def _dot(a, b):
    return jnp.dot(a, b, preferred_element_type=F32)


def _sigmoid(x):
    return 1.0 / (1.0 + jnp.exp(-x))


def _gelu_tanh(x):
    c = math.sqrt(2.0 / math.pi)
    return 0.5 * x * (1.0 + jnp.tanh(c * (x + 0.044715 * (x * x * x))))


def _const_spec(shape):
    nd = len(shape)
    return pl.BlockSpec(shape, lambda *_: (0,) * nd, pipeline_mode=pl.Buffered(1))


def _ssm_kernel(x_ref, g_ref, win_ref, bbig_ref, cbig_ref, are_ref, aim_ref, d_ref,
                y_ref, zg_ref,
                hr_ref, hi_ref, us_ref, xs_ref, hs_ref, ys_ref, *, pitch):
    bsz, tc, dm = x_ref.shape
    e = y_ref.shape[-1]
    n_tiles = e // LANES
    rows = bsz * tc

    @pl.when(pl.program_id(0) == 0)
    def _():
        hr_ref[...] = jnp.zeros_like(hr_ref)
        hi_ref[...] = jnp.zeros_like(hi_ref)

    x = x_ref[...].reshape(rows, dm)
    h = _rms(x, g_ref[...]).astype(BF16)
    z = _dot(h, win_ref[:, e:])
    zg_ref[...] = (z * _sigmoid(z)).astype(BF16).reshape(bsz, tc, e)
    u = _dot(h, win_ref[:, :e])
    for j in range(n_tiles):
        for b in range(bsz):
            us_ref[j, b * pitch:b * pitch + tc, :] = u[b * tc:(b + 1) * tc, j * LANES:(j + 1) * LANES]

    for j in range(n_tiles):
        u_tb = jnp.concatenate(
            [us_ref[j, pl.ds(t, bsz, stride=pitch), :] for t in range(tc)], axis=0)
        xs_ref[...] = _dot(u_tb.astype(BF16), bbig_ref[j])
        ar = jnp.broadcast_to(are_ref[j], (bsz, STATE_LANES))
        ai = jnp.broadcast_to(aim_ref[j], (bsz, STATE_LANES))

        def step(t, carry):
            sr, si = carry
            r0 = pl.multiple_of(t * bsz, SUBLANES)
            xr = xs_ref[pl.ds(r0, bsz), :STATE_LANES]
            xi = xs_ref[pl.ds(r0, bsz), STATE_LANES:]
            nr = ar * sr - ai * si + xr
            ni = ar * si + ai * sr + xi
            hs_ref[pl.ds(r0, bsz), :STATE_LANES] = nr
            hs_ref[pl.ds(r0, bsz), STATE_LANES:] = ni
            return nr, ni

        sr, si = lax.fori_loop(0, tc, step, (hr_ref[j], hi_ref[j]), unroll=8)
        hr_ref[j] = sr
        hi_ref[j] = si

        y_tb = _dot(hs_ref[...].astype(BF16), cbig_ref[j])
        y_tb = y_tb + d_ref[:, j * LANES:(j + 1) * LANES] * u_tb
        ys_ref[...] = _gelu_tanh(y_tb)
        for b in range(bsz):
            y_ref[b, :, j * LANES:(j + 1) * LANES] = ys_ref[pl.ds(b, tc, stride=bsz), :].astype(BF16)


def _ssm_call(x, norm_g, w_in, bbig, cbig, a_re, a_im, d):
    bsz, seq, dm = x.shape
    e = d.shape[-1]
    tc = SSM_TIME_CHUNK
    assert bsz == SUBLANES and seq % tc == 0 and e % LANES == 0 and tc % (2 * SUBLANES) == 0
    pitch = tc + SUBLANES
    n_tiles = e // LANES
    rows = bsz * tc
    out_shape = [jax.ShapeDtypeStruct((bsz, seq, e), BF16)] * 2
    blk = lambda i: (0, i, 0)
    return pl.pallas_call(
        functools.partial(_ssm_kernel, pitch=pitch),
        grid=(seq // tc,),
        in_specs=[pl.BlockSpec((bsz, tc, dm), blk),
                  _const_spec(norm_g.shape), _const_spec(w_in.shape), _const_spec(bbig.shape),
                  _const_spec(cbig.shape), _const_spec(a_re.shape), _const_spec(a_im.shape),
                  _const_spec(d.shape)],
        out_specs=[pl.BlockSpec((bsz, tc, e), blk)] * 2,
        out_shape=out_shape,
        scratch_shapes=[pltpu.VMEM((n_tiles, bsz, STATE_LANES), F32),
                        pltpu.VMEM((n_tiles, bsz, STATE_LANES), F32),
                        pltpu.VMEM((n_tiles, bsz * pitch, LANES), F32),
                        pltpu.VMEM((rows, 2 * STATE_LANES), F32),
                        pltpu.VMEM((rows, 2 * STATE_LANES), F32),
                        pltpu.VMEM((rows, LANES), F32)],
        compiler_params=pltpu.CompilerParams(dimension_semantics=("arbitrary",),
                                             vmem_limit_bytes=VMEM_LIMIT_BYTES),
        name="ssm_front",
    )(x, norm_g, w_in, bbig, cbig, a_re, a_im, d)


def _head_norm(t, gain_row, ones_ref, expand_ref):
    ss = _dot((t * t).astype(BF16), ones_ref[...])
    inv = lax.rsqrt(ss * (1.0 / HEAD_DIM) + EPS)
    hi = inv.astype(BF16)
    lo = (inv - hi.astype(F32)).astype(BF16)
    inv_full = _dot(jnp.concatenate([hi, lo], axis=1), expand_ref[...])
    return t * inv_full * gain_row


def _ple(x1, p, wgate_ref, wproj_ref):
    gate = _sigmoid(_dot(_rms(x1).astype(BF16), wgate_ref[...]))
    return x1 + gate * _dot(p.astype(BF16), wproj_ref[...])


def _mid_kernel(y_ref, zg_ref, x_ref, p_ref,
                wglu_ref, wout_ref, wgate_ref, wproj_ref,
                kvg_ref, wk_ref, wv_ref, kng_ref,
                bng_ref, bwin_ref, qng_ref, ones_ref, expand_ref,
                x2_ref, qt_ref, k_ref, vt_ref, zg1_ref):
    e = y_ref.shape[-1]
    n_heads = k_ref.shape[0]
    aw = n_heads * V_DIM
    gab = _dot(y_ref[...], wglu_ref[...])
    gl = gab[:, :e] * _sigmoid(gab[:, e:]) * zg_ref[...].astype(F32)
    x1 = x_ref[...] + _dot(gl.astype(BF16), wout_ref[...])
    x2 = _ple(x1, p_ref[...], wgate_ref, wproj_ref)
    x2_ref[...] = x2

    hk = _rms(x2, kvg_ref[...]).astype(BF16)
    k = _head_norm(_dot(hk, wk_ref[...]), kng_ref[...], ones_ref, expand_ref)
    v = _dot(hk, wv_ref[...])
    hq = _rms(x2, bng_ref[...]).astype(BF16)
    q = _head_norm(_dot(hq, bwin_ref[:, :aw]), qng_ref[...], ones_ref, expand_ref)
    z1 = _dot(hq, bwin_ref[:, aw:])
    zg1_ref[...] = (z1 * _sigmoid(z1)).astype(BF16)
    for hh in range(n_heads):
        sl = slice(hh * V_DIM, (hh + 1) * V_DIM)
        k_ref[hh] = k[:, sl].astype(BF16)
        qt_ref[hh] = q[:, sl].T.astype(BF16)
        vt_ref[hh] = v[:, sl].T.astype(BF16)


def _mid_call(y, zg, x, p, wglu, wout, wgate, wproj, kvg, wk, wv, kng, bng, bwin, qng, ones, expand):
    bsz, seq, dm = x.shape
    e = y.shape[-1]
    aw = wk.shape[-1]
    n_heads = aw // V_DIM
    tm = ROW_TILE
    assert seq % tm == 0
    row = lambda b, i: (b, i, 0)
    out_shape = [jax.ShapeDtypeStruct((bsz, seq, dm), F32),
                 jax.ShapeDtypeStruct((bsz, n_heads, V_DIM, seq), BF16),
                 jax.ShapeDtypeStruct((bsz, n_heads, seq, V_DIM), BF16),
                 jax.ShapeDtypeStruct((bsz, n_heads, V_DIM, seq), BF16),
                 jax.ShapeDtypeStruct((bsz, seq, aw), BF16)]
    t_spec = pl.BlockSpec((None, n_heads, V_DIM, tm), lambda b, i: (b, 0, 0, i))
    weights = (wglu, wout, wgate, wproj, kvg, wk, wv, kng, bng, bwin, qng, ones, expand)
    return pl.pallas_call(
        _mid_kernel,
        grid=(bsz, seq // tm),
        in_specs=[pl.BlockSpec((None, tm, e), row), pl.BlockSpec((None, tm, e), row),
                  pl.BlockSpec((None, tm, dm), row), pl.BlockSpec((None, tm, p.shape[-1]), row)]
                 + [_const_spec(w.shape) for w in weights],
        out_specs=[pl.BlockSpec((None, tm, dm), row), t_spec,
                   pl.BlockSpec((None, n_heads, tm, V_DIM), lambda b, i: (b, 0, i, 0)), t_spec,
                   pl.BlockSpec((None, tm, aw), row)],
        out_shape=out_shape,
        compiler_params=pltpu.CompilerParams(dimension_semantics=("parallel", "parallel"),
                                             vmem_limit_bytes=VMEM_LIMIT_BYTES),
        name="mid_rowwise",
    )(y, zg, x, p, *weights)


def _bucket_of(rel):
    n = jnp.maximum(rel, 0)
    nf = jnp.maximum(n, 1).astype(F32)
    large = REL_MAX_EXACT + (jnp.log(nf / REL_MAX_EXACT) / math.log(REL_MAX_DIST / REL_MAX_EXACT)
                             * (REL_BUCKETS - REL_MAX_EXACT)).astype(jnp.int32)
    large = jnp.minimum(large, REL_BUCKETS - 1)
    return jnp.where(n < REL_MAX_EXACT, n, large)


def _bias_kernel(table_ref, ids_ref, out_ref):
    hh = pl.program_id(0)
    ids = ids_ref[...]
    far = table_ref[REL_BUCKETS - 1, hh]
    acc = jnp.full(ids.shape, NEG_INF, F32)
    for bkt in range(REL_BUCKETS):
        acc = jnp.where(ids == bkt, (table_ref[bkt, hh] - far) * LOG2E, acc)
    out_ref[...] = acc


def _bias_call(table, ids):
    n_heads = table.shape[1]
    return pl.pallas_call(
        _bias_kernel,
        grid=(n_heads,),
        in_specs=[pl.BlockSpec(memory_space=pltpu.SMEM), _const_spec(ids.shape)],
        out_specs=pl.BlockSpec((None,) + ids.shape, lambda hh: (hh, 0, 0, 0)),
        out_shape=jax.ShapeDtypeStruct((n_heads,) + ids.shape, F32),
        name="rel_bias_tiles",
    )(table, ids)


def _attn_kernel(lam_ref, qt_ref, k_ref, vt_ref, bias_ref, sg_ref, o_ref,
                 m_ref, l_ref, acc_ref, sa_ref, sb_ref, *, out_scale):
    tq = qt_ref.shape[-1]
    tk = tq
    i = pl.program_id(2)
    qt = qt_ref[...]
    top = lax.broadcasted_iota(jnp.int32, qt.shape, 0) < HEAD_DIM
    zero = jnp.zeros_like(qt)
    qz = (jnp.where(top, qt, zero), jnp.where(top, zero, qt))

    m_ref[...] = jnp.full(m_ref.shape, NEG_INF, F32)
    l_ref[...] = jnp.zeros_like(l_ref)
    acc_ref[...] = jnp.zeros_like(acc_ref)

    def scores(s_ref, blk):
        kb = k_ref[pl.ds(pl.multiple_of(blk * tk, tk), tk), :]
        for c in range(2):
            s_ref[c] = _dot(kb, qz[c])

    def update(s_ref, blk, bias):
        vb = vt_ref[:, pl.ds(pl.multiple_of(blk * tk, tk), tk)]
        for c in range(2):
            s = s_ref[c]
            if bias is not None:
                s = s + bias
            m_old = m_ref[c]
            m_new = jnp.maximum(m_old, jnp.max(s, axis=0, keepdims=True))
            alpha = jnp.exp2(m_old - m_new)
            p = jnp.exp2(s - m_new)
            l_ref[c] = alpha * l_ref[c] + jnp.sum(p, axis=0, keepdims=True)
            acc_ref[c] = alpha * acc_ref[c] + _dot(vb, p.astype(BF16))
            m_ref[c] = m_new

    def step(src_ref, dst_ref, blk, bias, has_next=True):
        if has_next:
            scores(dst_ref, blk + 1)
        update(src_ref, blk, bias)

    n_far = jnp.maximum(i - 1, 0)
    n_pairs = n_far // 2
    scores(sa_ref, 0)

    def far_body(t, carry):
        step(sa_ref, sb_ref, 2 * t, None)
        step(sb_ref, sa_ref, 2 * t + 1, None)
        return carry

    lax.fori_loop(0, n_pairs, far_body, 0)

    @pl.when(n_far % 2 == 1)
    def _():
        step(sa_ref, sb_ref, i - 2, None)
        step(sb_ref, sa_ref, i - 1, bias_ref[1])
        step(sa_ref, None, i, bias_ref[0], has_next=False)

    @pl.when(jnp.logical_and(n_far % 2 == 0, i > 0))
    def _():
        step(sa_ref, sb_ref, i - 1, bias_ref[1])
        step(sb_ref, None, i, bias_ref[0], has_next=False)

    @pl.when(i == 0)
    def _():
        step(sa_ref, None, 0, bias_ref[0], has_next=False)

    o1 = acc_ref[0] * (1.0 / l_ref[0])
    o2 = acc_ref[1] * (1.0 / l_ref[1])
    o = o1 - lam_ref[...] * o2
    o = o * lax.rsqrt(jnp.mean(o * o, axis=0, keepdims=True) + EPS) * (sg_ref[...] * out_scale)
    o_ref[...] = o.T.astype(BF16)


def _attn_call(lam, qt, k, vt, bias, subln_g, out_scale):
    bsz, n_heads, _, seq = qt.shape
    tq = ATTN_TQ
    assert seq % tq == 0 and bias.shape[1:] == (2, tq, tq)
    return pl.pallas_call(
        functools.partial(_attn_kernel, out_scale=out_scale),
        grid=(bsz, n_heads, seq // tq),
        in_specs=[_const_spec(lam.shape),
                  pl.BlockSpec((None, None, V_DIM, tq), lambda b, hh, i: (b, hh, 0, i)),
                  pl.BlockSpec((None, None, seq, V_DIM), lambda b, hh, i: (b, hh, 0, 0)),
                  pl.BlockSpec((None, None, V_DIM, seq), lambda b, hh, i: (b, hh, 0, 0)),
                  pl.BlockSpec((None,) + bias.shape[1:], lambda b, hh, i: (hh, 0, 0, 0)),
                  _const_spec(subln_g.shape)],
        out_specs=pl.BlockSpec((None, tq, V_DIM), lambda b, hh, i: (b, i, hh)),
        out_shape=jax.ShapeDtypeStruct((bsz, seq, n_heads * V_DIM), BF16),
        scratch_shapes=[pltpu.VMEM((2, 1, tq), F32), pltpu.VMEM((2, 1, tq), F32),
                        pltpu.VMEM((2, V_DIM, tq), F32),
                        pltpu.VMEM((2, tq, tq), F32), pltpu.VMEM((2, tq, tq), F32)],
        compiler_params=pltpu.CompilerParams(dimension_semantics=("parallel", "parallel", "arbitrary"),
                                             vmem_limit_bytes=VMEM_LIMIT_BYTES),
        name="diff_attention",
    )(lam, qt, k, vt, bias, subln_g)


def _final_kernel(o_ref, zg_ref, x_ref, p_ref, wout_ref, wgate_ref, wproj_ref, out_ref):
    g = (o_ref[...].astype(F32) * zg_ref[...].astype(F32)).astype(BF16)
    x1 = x_ref[...] + _dot(g, wout_ref[...])
    out_ref[...] = _ple(x1, p_ref[...], wgate_ref, wproj_ref)


def _final_call(o, zg, x, p, wout, wgate, wproj):
    bsz, seq, dm = x.shape
    tm = ROW_TILE
    row = lambda b, i: (b, i, 0)
    return pl.pallas_call(
        _final_kernel,
        grid=(bsz, seq // tm),
        in_specs=[pl.BlockSpec((None, tm, o.shape[-1]), row), pl.BlockSpec((None, tm, zg.shape[-1]), row),
                  pl.BlockSpec((None, tm, dm), row), pl.BlockSpec((None, tm, p.shape[-1]), row),
                  _const_spec(wout.shape), _const_spec(wgate.shape), _const_spec(wproj.shape)],
        out_specs=pl.BlockSpec((None, tm, dm), row),
        out_shape=jax.ShapeDtypeStruct((bsz, seq, dm), F32),
        compiler_params=pltpu.CompilerParams(dimension_semantics=("parallel", "parallel"),
                                             vmem_limit_bytes=VMEM_LIMIT_BYTES),
        name="final_rowwise",
    )(o, zg, x, p, wout, wgate, wproj)


def _ssm_params(lam_re, lam_im, log_dt, b_re, b_im, c_re, c_im):
    lr, li = lam_re.astype(F32), lam_im.astype(F32)
    dt = jnp.exp(log_dt.astype(F32))[:, None]
    mag = jnp.exp(lr * dt)
    ab_re, ab_im = mag * jnp.cos(li * dt), mag * jnp.sin(li * dt)
    den = lr * lr + li * li
    nr, ni = ab_re - 1.0, ab_im
    f_re = ((nr * lr + ni * li) / den)[..., None]
    f_im = ((ni * lr - nr * li) / den)[..., None]
    br, bi = b_re.astype(F32), b_im.astype(F32)
    bb_re, bb_im = f_re * br - f_im * bi, f_re * bi + f_im * br
    n_tiles = lam_re.shape[0] // GROUPS_PER_TILE
    eye = jnp.eye(GROUPS_PER_TILE, dtype=F32)

    def pack_b(m):
        m = m.reshape(n_tiles, GROUPS_PER_TILE, SSM_STATE, SSM_GROUP)
        return jnp.einsum('jgpc,gh->jgchp', m, eye).reshape(n_tiles, LANES, STATE_LANES)

    def pack_c(m):
        m = m.reshape(n_tiles, GROUPS_PER_TILE, SSM_GROUP, SSM_STATE)
        return jnp.einsum('jgcp,gh->jgphc', m, eye).reshape(n_tiles, STATE_LANES, LANES)

    bbig = jnp.concatenate([pack_b(bb_re), pack_b(bb_im)], axis=2).astype(BF16)
    cbig = jnp.concatenate([pack_c(c_re.astype(F32)), pack_c(-c_im.astype(F32))], axis=1).astype(BF16)
    a_re = ab_re.reshape(n_tiles, 1, STATE_LANES)
    a_im = ab_im.reshape(n_tiles, 1, STATE_LANES)
    return bbig, cbig, a_re, a_im


def _bias_ids(tk, tq, seq):
    kk = jnp.arange(tk, dtype=jnp.int32)[:, None]
    qq = jnp.arange(tq, dtype=jnp.int32)[None, :]
    rel0 = qq - kk
    rel1 = rel0 + tk
    ids0 = jnp.where(rel0 >= 0, _bucket_of(rel0), -1)
    far = np.arange(tk + 1, max(seq, tk + 2), dtype=np.float32)
    far = REL_MAX_EXACT + (np.log(far / REL_MAX_EXACT) / math.log(REL_MAX_DIST / REL_MAX_EXACT)
                           * (REL_BUCKETS - REL_MAX_EXACT)).astype(np.int32)
    assert far.min() > REL_BUCKETS
    return jnp.stack([ids0, _bucket_of(rel1)])


def kernel(x, p, a_norm_g, a_w_in, a_lam_re, a_lam_im, a_log_dt, a_b_re, a_b_im, a_c_re, a_c_im, a_d,
           a_w_glu, a_w_out, kv_norm_g, w_k, w_v, k_norm_g, b_norm_g, b_w_in, b_q_norm_g, b_lam_q1,
           b_lam_k1, b_lam_q2, b_lam_k2, b_subln_g, b_w_out, rel_bias, ple_w_proj, ple_w_gate):
    assert a_norm_g.shape[0] == 1 and b_norm_g.shape[0] == 1 and p.shape[0] == 2
    bsz, seq, dm = x.shape
    aw = w_k.shape[-1]
    n_half = aw // HEAD_DIM
    row = lambda v: v.reshape(1, -1).astype(F32)

    bbig, cbig, a_re, a_im = _ssm_params(a_lam_re[0], a_lam_im[0], a_log_dt[0], a_b_re[0], a_b_im[0],
                                         a_c_re[0], a_c_im[0])
    y, zg = _ssm_call(x, row(a_norm_g[0]), a_w_in[0].astype(BF16), bbig, cbig, a_re, a_im, row(a_d[0]))

    lane_group = jnp.arange(aw, dtype=jnp.int32) // HEAD_DIM
    ones = (lane_group[:, None] == jnp.arange(LANES, dtype=jnp.int32)[None, :]).astype(BF16)
    expand = jnp.concatenate([ones.T, ones.T], axis=0)
    q_scale = HEAD_DIM ** -0.5 * LOG2E
    x2, qt, k, vt, zg1 = _mid_call(
        y, zg, x, p[0], a_w_glu[0].astype(BF16), a_w_out[0].astype(BF16),
        ple_w_gate[0].astype(BF16), ple_w_proj[0].astype(BF16),
        row(kv_norm_g), w_k.astype(BF16), w_v.astype(BF16), row(jnp.tile(k_norm_g, n_half)),
        row(b_norm_g[0]), b_w_in[0].astype(BF16), row(jnp.tile(b_q_norm_g[0], n_half) * q_scale),
        ones, expand)

    layer_idx = 1
    lam_init = 0.8 - 0.6 * math.exp(-0.3 * layer_idx)
    lam = (jnp.exp(jnp.sum(b_lam_q1[0].astype(F32) * b_lam_k1[0].astype(F32)))
           - jnp.exp(jnp.sum(b_lam_q2[0].astype(F32) * b_lam_k2[0].astype(F32))) + lam_init).reshape(1, 1)
    bias = _bias_call(rel_bias.astype(F32), _bias_ids(ATTN_TQ, ATTN_TQ, seq))
    o = _attn_call(lam, qt, k, vt, bias, b_subln_g[0].astype(F32).reshape(V_DIM, 1), 1.0 - lam_init)
    return _final_call(o, zg1, x2, p[1], b_w_out[0].astype(BF16), ple_w_gate[1].astype(BF16),
                       ple_w_proj[1].astype(BF16))
```

```python
import functools
import math

import jax
import jax.numpy as jnp
import numpy as np
from jax import lax
from jax.experimental import pallas as pl
from jax.experimental.pallas import tpu as pltpu

F32 = jnp.float32
BF16 = jnp.bfloat16

SUBLANES = 8
LANES = 128
VMEM_LIMIT_BYTES = 56 * 1024 * 1024

EPS = 1e-6
NEG_INF = -1e30
LOG2E = 1.4426950408889634

SSM_GROUP = 16
SSM_STATE = 64
GROUPS_PER_TILE = LANES // SSM_GROUP
STATE_LANES = GROUPS_PER_TILE * SSM_STATE
HEAD_DIM = 64
V_DIM = 2 * HEAD_DIM
REL_BUCKETS = 32
REL_MAX_EXACT = REL_BUCKETS // 2
REL_MAX_DIST = 128

SSM_TIME_CHUNK = 128
ROW_TILE = 256
ATTN_TQ = 512


def _rms(x, g=None):
    y = x * lax.rsqrt(jnp.mean(x * x, axis=-1, keepdims=True) + EPS)
    return y if g is None else y * g


def _dot(a, b):
    return jnp.dot(a, b, preferred_element_type=F32)


def _sigmoid(x):
    return 1.0 / (1.0 + jnp.exp(-x))


def _gelu_tanh(x):
    c = math.sqrt(2.0 / math.pi)
    return 0.5 * x * (1.0 + jnp.tanh(c * (x + 0.044715 * (x * x * x))))


def _const_spec(shape):
    nd = len(shape)
    return pl.BlockSpec(shape, lambda *_: (0,) * nd, pipeline_mode=pl.Buffered(1))


def _ssm_kernel(x_ref, g_ref, win_ref, bbig_ref, cbig_ref, are_ref, aim_ref, d_ref,
                y_ref, zg_ref,
                hr_ref, hi_ref, us_ref, xs_ref, hs_ref, ys_ref, *, pitch):
    bsz, tc, dm = x_ref.shape
    e = y_ref.shape[-1]
    n_tiles = e // LANES
    rows = bsz * tc

    @pl.when(pl.program_id(0) == 0)
    def _():
        hr_ref[...] = jnp.zeros_like(hr_ref)
        hi_ref[...] = jnp.zeros_like(hi_ref)

    x = x_ref[...].reshape(rows, dm)
    h = _rms(x, g_ref[...]).astype(BF16)
    z = _dot(h, win_ref[:, e:])
    zg_ref[...] = (z * _sigmoid(z)).astype(BF16).reshape(bsz, tc, e)
    u = _dot(h, win_ref[:, :e])
    for j in range(n_tiles):
        for b in range(bsz):
            us_ref[j, b * pitch:b * pitch + tc, :] = u[b * tc:(b + 1) * tc, j * LANES:(j + 1) * LANES]

    for j in range(n_tiles):
        u_tb = jnp.concatenate(
            [us_ref[j, pl.ds(t, bsz, stride=pitch), :] for t in range(tc)], axis=0)
        xs_ref[...] = _dot(u_tb.astype(BF16), bbig_ref[j])
        ar = jnp.broadcast_to(are_ref[j], (bsz, STATE_LANES))
        ai = jnp.broadcast_to(aim_ref[j], (bsz, STATE_LANES))

        def step(t, carry):
            sr, si = carry
            r0 = pl.multiple_of(t * bsz, SUBLANES)
            xr = xs_ref[pl.ds(r0, bsz), :STATE_LANES]
            xi = xs_ref[pl.ds(r0, bsz), STATE_LANES:]
            nr = ar * sr - ai * si + xr
            ni = ar * si + ai * sr + xi
            hs_ref[pl.ds(r0, bsz), :STATE_LANES] = nr
            hs_ref[pl.ds(r0, bsz), STATE_LANES:] = ni
            return nr, ni

        sr, si = lax.fori_loop(0, tc, step, (hr_ref[j], hi_ref[j]), unroll=8)
        hr_ref[j] = sr
        hi_ref[j] = si

        y_tb = _dot(hs_ref[...].astype(BF16), cbig_ref[j])
        y_tb = y_tb + d_ref[:, j * LANES:(j + 1) * LANES] * u_tb
        ys_ref[...] = _gelu_tanh(y_tb)
        for b in range(bsz):
            y_ref[b, :, j * LANES:(j + 1) * LANES] = ys_ref[pl.ds(b, tc, stride=bsz), :].astype(BF16)


def _ssm_call(x, norm_g, w_in, bbig, cbig, a_re, a_im, d):
    bsz, seq, dm = x.shape
    e = d.shape[-1]
    tc = SSM_TIME_CHUNK
    assert bsz == SUBLANES and seq % tc == 0 and e % LANES == 0 and tc % (2 * SUBLANES) == 0
    pitch = tc + SUBLANES
    n_tiles = e // LANES
    rows = bsz * tc
    out_shape = [jax.ShapeDtypeStruct((bsz, seq, e), BF16)] * 2
    blk = lambda i: (0, i, 0)
    return pl.pallas_call(
        functools.partial(_ssm_kernel, pitch=pitch),
        grid=(seq // tc,),
        in_specs=[pl.BlockSpec((bsz, tc, dm), blk),
                  _const_spec(norm_g.shape), _const_spec(w_in.shape), _const_spec(bbig.shape),
                  _const_spec(cbig.shape), _const_spec(a_re.shape), _const_spec(a_im.shape),
                  _const_spec(d.shape)],
        out_specs=[pl.BlockSpec((bsz, tc, e), blk)] * 2,
        out_shape=out_shape,
        scratch_shapes=[pltpu.VMEM((n_tiles, bsz, STATE_LANES), F32),
                        pltpu.VMEM((n_tiles, bsz, STATE_LANES), F32),
                        pltpu.VMEM((n_tiles, bsz * pitch, LANES), F32),
                        pltpu.VMEM((rows, 2 * STATE_LANES), F32),
                        pltpu.VMEM((rows, 2 * STATE_LANES), F32),
                        pltpu.VMEM((rows, LANES), F32)],
        compiler_params=pltpu.CompilerParams(dimension_semantics=("arbitrary",),
                                             vmem_limit_bytes=VMEM_LIMIT_BYTES),
        name="ssm_front",
    )(x, norm_g, w_in, bbig, cbig, a_re, a_im, d)


def _head_norm(t, gain_row, ones_ref, expand_ref):
    ss = _dot((t * t).astype(BF16), ones_ref[...])
    inv = lax.rsqrt(ss * (1.0 / HEAD_DIM) + EPS)
    hi = inv.astype(BF16)
    lo = (inv - hi.astype(F32)).astype(BF16)
    inv_full = _dot(jnp.concatenate([hi, lo], axis=1), expand_ref[...])
    return t * inv_full * gain_row


def _ple(x1, p, wgate_ref, wproj_ref):
    gate = _sigmoid(_dot(_rms(x1).astype(BF16), wgate_ref[...]))
    return x1 + gate * _dot(p.astype(BF16), wproj_ref[...])


def _mid_kernel(y_ref, zg_ref, x_ref, p_ref,
                wglu_ref, wout_ref, wgate_ref, wproj_ref,
                kvg_ref, wk_ref, wv_ref, kng_ref,
                bng_ref, bwin_ref, qng_ref, ones_ref, expand_ref,
                x2_ref, qt_ref, k_ref, vt_ref, zg1_ref):
    e = y_ref.shape[-1]
    n_heads = k_ref.shape[0]
    aw = n_heads * V_DIM
    gab = _dot(y_ref[...], wglu_ref[...])
    gl = gab[:, :e] * _sigmoid(gab[:, e:]) * zg_ref[...].astype(F32)
    x1 = x_ref[...] + _dot(gl.astype(BF16), wout_ref[...])
    x2 = _ple(x1, p_ref[...], wgate_ref, wproj_ref)
    x2_ref[...] = x2

    hk = _rms(x2, kvg_ref[...]).astype(BF16)
    k = _head_norm(_dot(hk, wk_ref[...]), kng_ref[...], ones_ref, expand_ref)
    v = _dot(hk, wv_ref[...])
    hq = _rms(x2, bng_ref[...]).astype(BF16)
    q = _head_norm(_dot(hq, bwin_ref[:, :aw]), qng_ref[...], ones_ref, expand_ref)
    z1 = _dot(hq, bwin_ref[:, aw:])
    zg1_ref[...] = (z1 * _sigmoid(z1)).astype(BF16)
    for hh in range(n_heads):
        sl = slice(hh * V_DIM, (hh + 1) * V_DIM)
        k_ref[hh] = k[:, sl].astype(BF16)
        qt_ref[hh] = q[:, sl].T.astype(BF16)
        vt_ref[hh] = v[:, sl].T.astype(BF16)


def _mid_call(y, zg, x, p, wglu, wout, wgate, wproj, kvg, wk, wv, kng, bng, bwin, qng, ones, expand):
    bsz, seq, dm = x.shape
    e = y.shape[-1]
    aw = wk.shape[-1]
    n_heads = aw // V_DIM
    tm = ROW_TILE
    assert seq % tm == 0
    row = lambda b, i: (b, i, 0)
    out_shape = [jax.ShapeDtypeStruct((bsz, seq, dm), F32),
                 jax.ShapeDtypeStruct((bsz, n_heads, V_DIM, seq), BF16),
                 jax.ShapeDtypeStruct((bsz, n_heads, seq, V_DIM), BF16),
                 jax.ShapeDtypeStruct((bsz, n_heads, V_DIM, seq), BF16),
                 jax.ShapeDtypeStruct((bsz, seq, aw), BF16)]
    t_spec = pl.BlockSpec((None, n_heads, V_DIM, tm), lambda b, i: (b, 0, 0, i))
    weights = (wglu, wout, wgate, wproj, kvg, wk, wv, kng, bng, bwin, qng, ones, expand)
    return pl.pallas_call(
        _mid_kernel,
        grid=(bsz, seq // tm),
        in_specs=[pl.BlockSpec((None, tm, e), row), pl.BlockSpec((None, tm, e), row),
                  pl.BlockSpec((None, tm, dm), row), pl.BlockSpec((None, tm, p.shape[-1]), row)]
                 + [_const_spec(w.shape) for w in weights],
        out_specs=[pl.BlockSpec((None, tm, dm), row), t_spec,
                   pl.BlockSpec((None, n_heads, tm, V_DIM), lambda b, i: (b, 0, i, 0)), t_spec,
                   pl.BlockSpec((None, tm, aw), row)],
        out_shape=out_shape,
        compiler_params=pltpu.CompilerParams(dimension_semantics=("parallel", "parallel"),
                                             vmem_limit_bytes=VMEM_LIMIT_BYTES),
        name="mid_rowwise",
    )(y, zg, x, p, *weights)


def _bucket_of(rel):
    n = jnp.maximum(rel, 0)
    nf = jnp.maximum(n, 1).astype(F32)
    large = REL_MAX_EXACT + (jnp.log(nf / REL_MAX_EXACT) / math.log(REL_MAX_DIST / REL_MAX_EXACT)
                             * (REL_BUCKETS - REL_MAX_EXACT)).astype(jnp.int32)
    large = jnp.minimum(large, REL_BUCKETS - 1)
    return jnp.where(n < REL_MAX_EXACT, n, large)


def _bias_kernel(table_ref, ids_ref, out_ref):
    hh = pl.program_id(0)
    ids = ids_ref[...]
    far = table_ref[REL_BUCKETS - 1, hh]
    acc = jnp.full(ids.shape, NEG_INF, F32)
    for bkt in range(REL_BUCKETS):
        acc = jnp.where(ids == bkt, (table_ref[bkt, hh] - far) * LOG2E, acc)
    out_ref[...] = acc


def _bias_call(table, ids):
    n_heads = table.shape[1]
    return pl.pallas_call(
        _bias_kernel,
        grid=(n_heads,),
        in_specs=[pl.BlockSpec(memory_space=pltpu.SMEM), _const_spec(ids.shape)],
        out_specs=pl.BlockSpec((None,) + ids.shape, lambda hh: (hh, 0, 0, 0)),
        out_shape=jax.ShapeDtypeStruct((n_heads,) + ids.shape, F32),
        name="rel_bias_tiles",
    )(table, ids)


def _attn_kernel(lam_ref, qt_ref, k_ref, vt_ref, bias_ref, sg_ref, o_ref,
                 m_ref, l_ref, acc_ref, sa_ref, sb_ref, ma_ref, mb_ref, *, out_scale):
    tq = qt_ref.shape[-1]
    tk = tq
    i = pl.program_id(2)
    qt = qt_ref[...]
    top = lax.broadcasted_iota(jnp.int32, qt.shape, 0) < HEAD_DIM
    zero = jnp.zeros_like(qt)
    qz = (jnp.where(top, qt, zero), jnp.where(top, zero, qt))

    m_ref[...] = jnp.full(m_ref.shape, NEG_INF, F32)
    l_ref[...] = jnp.zeros_like(l_ref)
    acc_ref[...] = jnp.zeros_like(acc_ref)

    sub = 2 * LANES
    bufs = ((sa_ref, ma_ref), (sb_ref, mb_ref))

    def scores(buf, blk, bias):
        s_ref, bm_ref = buf
        kb = k_ref[pl.ds(pl.multiple_of(blk * tk, tk), tk), :]
        for c in range(2):
            s = _dot(kb, qz[c])
            if bias is not None:
                s = s + bias
            s_ref[c] = s
            bm_ref[c] = jnp.max(s, axis=0, keepdims=True)

    def update(buf, blk):
        s_ref, bm_ref = buf
        vb = vt_ref[:, pl.ds(pl.multiple_of(blk * tk, tk), tk)]
        for c in range(2):
            m_old = m_ref[c]
            m_new = jnp.maximum(m_old, bm_ref[c])
            alpha = jnp.exp2(m_old - m_new)
            m_ref[c] = m_new
            sums = []
            for n in range(tq // sub):
                qs = slice(n * sub, (n + 1) * sub)
                pv, ls = None, None
                for kt in range(tk // sub):
                    ks = slice(kt * sub, (kt + 1) * sub)
                    p = jnp.exp2(s_ref[c, ks, qs] - m_new[:, qs])
                    psum = jnp.sum(p, axis=0, keepdims=True)
                    d = _dot(vb[:, ks], p.astype(BF16))
                    pv, ls = (d, psum) if pv is None else (pv + d, ls + psum)
                acc_ref[c, :, qs] = alpha[:, qs] * acc_ref[c, :, qs] + pv
                sums.append(ls)
            l_ref[c] = alpha * l_ref[c] + jnp.concatenate(sums, axis=1)

    def step(src, dst, blk, next_bias):
        scores(dst, blk + 1, next_bias)
        update(src, blk)

    bufa, bufb = bufs
    n_far = jnp.maximum(i - 1, 0)
    n_pairs = jnp.maximum(n_far - 1, 0) // 2
    n_left = n_far - 2 * n_pairs

    @pl.when(i == 0)
    def _():
        scores(bufa, 0, bias_ref[0])

    @pl.when(i == 1)
    def _():
        scores(bufa, 0, bias_ref[1])

    @pl.when(i >= 2)
    def _():
        scores(bufa, 0, None)

    def far_body(t, carry):
        step(bufa, bufb, 2 * t, None)
        step(bufb, bufa, 2 * t + 1, None)
        return carry

    lax.fori_loop(0, n_pairs, far_body, 0)

    @pl.when(i == 0)
    def _():
        update(bufa, 0)

    @pl.when(i == 1)
    def _():
        step(bufa, bufb, 0, bias_ref[0])
        update(bufb, 1)

    @pl.when(n_left == 1)
    def _():
        step(bufa, bufb, i - 2, bias_ref[1])
        step(bufb, bufa, i - 1, bias_ref[0])
        update(bufa, i)

    @pl.when(n_left == 2)
    def _():
        step(bufa, bufb, i - 3, None)
        step(bufb, bufa, i - 2, bias_ref[1])
        step(bufa, bufb, i - 1, bias_ref[0])
        update(bufb, i)

    o1 = acc_ref[0] * (1.0 / l_ref[0])
    o2 = acc_ref[1] * (1.0 / l_ref[1])
    o = o1 - lam_ref[...] * o2
    o = o * lax.rsqrt(jnp.mean(o * o, axis=0, keepdims=True) + EPS) * (sg_ref[...] * out_scale)
    o_ref[...] = o.T.astype(BF16)


def _attn_call(lam, qt, k, vt, bias, subln_g, out_scale):
    bsz, n_heads, _, seq = qt.shape
    tq = ATTN_TQ
    assert seq % tq == 0 and bias.shape[1:] == (2, tq, tq)
    return pl.pallas_call(
        functools.partial(_attn_kernel, out_scale=out_scale),
        grid=(bsz, n_heads, seq // tq),
        in_specs=[_const_spec(lam.shape),
                  pl.BlockSpec((None, None, V_DIM, tq), lambda b, hh, i: (b, hh, 0, i)),
                  pl.BlockSpec((None, None, seq, V_DIM), lambda b, hh, i: (b, hh, 0, 0)),
                  pl.BlockSpec((None, None, V_DIM, seq), lambda b, hh, i: (b, hh, 0, 0)),
                  pl.BlockSpec((None,) + bias.shape[1:], lambda b, hh, i: (hh, 0, 0, 0)),
                  _const_spec(subln_g.shape)],
        out_specs=pl.BlockSpec((None, tq, V_DIM), lambda b, hh, i: (b, i, hh)),
        out_shape=jax.ShapeDtypeStruct((bsz, seq, n_heads * V_DIM), BF16),
        scratch_shapes=[pltpu.VMEM((2, 1, tq), F32), pltpu.VMEM((2, 1, tq), F32),
                        pltpu.VMEM((2, V_DIM, tq), F32),
                        pltpu.VMEM((2, tq, tq), F32), pltpu.VMEM((2, tq, tq), F32),
                        pltpu.VMEM((2, 1, tq), F32), pltpu.VMEM((2, 1, tq), F32)],
        compiler_params=pltpu.CompilerParams(dimension_semantics=("parallel", "parallel", "arbitrary"),
                                             vmem_limit_bytes=VMEM_LIMIT_BYTES),
        name="diff_attention",
    )(lam, qt, k, vt, bias, subln_g)


def _final_kernel(o_ref, zg_ref, x_ref, p_ref, wout_ref, wgate_ref, wproj_ref, out_ref):
    g = (o_ref[...].astype(F32) * zg_ref[...].astype(F32)).astype(BF16)
    x1 = x_ref[...] + _dot(g, wout_ref[...])
    out_ref[...] = _ple(x1, p_ref[...], wgate_ref, wproj_ref)


def _final_call(o, zg, x, p, wout, wgate, wproj):
    bsz, seq, dm = x.shape
    tm = ROW_TILE
    row = lambda b, i: (b, i, 0)
    return pl.pallas_call(
        _final_kernel,
        grid=(bsz, seq // tm),
        in_specs=[pl.BlockSpec((None, tm, o.shape[-1]), row), pl.BlockSpec((None, tm, zg.shape[-1]), row),
                  pl.BlockSpec((None, tm, dm), row), pl.BlockSpec((None, tm, p.shape[-1]), row),
                  _const_spec(wout.shape), _const_spec(wgate.shape), _const_spec(wproj.shape)],
        out_specs=pl.BlockSpec((None, tm, dm), row),
        out_shape=jax.ShapeDtypeStruct((bsz, seq, dm), F32),
        compiler_params=pltpu.CompilerParams(dimension_semantics=("parallel", "parallel"),
                                             vmem_limit_bytes=VMEM_LIMIT_BYTES),
        name="final_rowwise",
    )(o, zg, x, p, wout, wgate, wproj)


def _ssm_params(lam_re, lam_im, log_dt, b_re, b_im, c_re, c_im):
    lr, li = lam_re.astype(F32), lam_im.astype(F32)
    dt = jnp.exp(log_dt.astype(F32))[:, None]
    mag = jnp.exp(lr * dt)
    ab_re, ab_im = mag * jnp.cos(li * dt), mag * jnp.sin(li * dt)
    den = lr * lr + li * li
    nr, ni = ab_re - 1.0, ab_im
    f_re = ((nr * lr + ni * li) / den)[..., None]
    f_im = ((ni * lr - nr * li) / den)[..., None]
    br, bi = b_re.astype(F32), b_im.astype(F32)
    bb_re, bb_im = f_re * br - f_im * bi, f_re * bi + f_im * br
    n_tiles = lam_re.shape[0] // GROUPS_PER_TILE
    eye = jnp.eye(GROUPS_PER_TILE, dtype=F32)

    def pack_b(m):
        m = m.reshape(n_tiles, GROUPS_PER_TILE, SSM_STATE, SSM_GROUP)
        return jnp.einsum('jgpc,gh->jgchp', m, eye).reshape(n_tiles, LANES, STATE_LANES)

    def pack_c(m):
        m = m.reshape(n_tiles, GROUPS_PER_TILE, SSM_GROUP, SSM_STATE)
        return jnp.einsum('jgcp,gh->jgphc', m, eye).reshape(n_tiles, STATE_LANES, LANES)

    bbig = jnp.concatenate([pack_b(bb_re), pack_b(bb_im)], axis=2).astype(BF16)
    cbig = jnp.concatenate([pack_c(c_re.astype(F32)), pack_c(-c_im.astype(F32))], axis=1).astype(BF16)
    a_re = ab_re.reshape(n_tiles, 1, STATE_LANES)
    a_im = ab_im.reshape(n_tiles, 1, STATE_LANES)
    return bbig, cbig, a_re, a_im


def _bias_ids(tk, tq, seq):
    kk = jnp.arange(tk, dtype=jnp.int32)[:, None]
    qq = jnp.arange(tq, dtype=jnp.int32)[None, :]
    rel0 = qq - kk
    rel1 = rel0 + tk
    ids0 = jnp.where(rel0 >= 0, _bucket_of(rel0), -1)
    far = np.arange(tk + 1, max(seq, tk + 2), dtype=np.float32)
    far = REL_MAX_EXACT + (np.log(far / REL_MAX_EXACT) / math.log(REL_MAX_DIST / REL_MAX_EXACT)
                           * (REL_BUCKETS - REL_MAX_EXACT)).astype(np.int32)
    assert far.min() > REL_BUCKETS
    return jnp.stack([ids0, _bucket_of(rel1)])


def kernel(x, p, a_norm_g, a_w_in, a_lam_re, a_lam_im, a_log_dt, a_b_re, a_b_im, a_c_re, a_c_im, a_d,
           a_w_glu, a_w_out, kv_norm_g, w_k, w_v, k_norm_g, b_norm_g, b_w_in, b_q_norm_g, b_lam_q1,
           b_lam_k1, b_lam_q2, b_lam_k2, b_subln_g, b_w_out, rel_bias, ple_w_proj, ple_w_gate):
    assert a_norm_g.shape[0] == 1 and b_norm_g.shape[0] == 1 and p.shape[0] == 2
    bsz, seq, dm = x.shape
    aw = w_k.shape[-1]
    n_half = aw // HEAD_DIM
    row = lambda v: v.reshape(1, -1).astype(F32)

    bbig, cbig, a_re, a_im = _ssm_params(a_lam_re[0], a_lam_im[0], a_log_dt[0], a_b_re[0], a_b_im[0],
                                         a_c_re[0], a_c_im[0])
    y, zg = _ssm_call(x, row(a_norm_g[0]), a_w_in[0].astype(BF16), bbig, cbig, a_re, a_im, row(a_d[0]))

    lane_group = jnp.arange(aw, dtype=jnp.int32) // HEAD_DIM
    ones = (lane_group[:, None] == jnp.arange(LANES, dtype=jnp.int32)[None, :]).astype(BF16)
    expand = jnp.concatenate([ones.T, ones.T], axis=0)
    q_scale = HEAD_DIM ** -0.5 * LOG2E
    x2, qt, k, vt, zg1 = _mid_call(
        y, zg, x, p[0], a_w_glu[0].astype(BF16), a_w_out[0].astype(BF16),
        ple_w_gate[0].astype(BF16), ple_w_proj[0].astype(BF16),
        row(kv_norm_g), w_k.astype(BF16), w_v.astype(BF16), row(jnp.tile(k_norm_g, n_half)),
        row(b_norm_g[0]), b_w_in[0].astype(BF16), row(jnp.tile(b_q_norm_g[0], n_half) * q_scale),
        ones, expand)

    layer_idx = 1
    lam_init = 0.8 - 0.6 * math.exp(-0.3 * layer_idx)
    lam = (jnp.exp(jnp.sum(b_lam_q1[0].astype(F32) * b_lam_k1[0].astype(F32)))
           - jnp.exp(jnp.sum(b_lam_q2[0].astype(F32) * b_lam_k2[0].astype(F32))) + lam_init).reshape(1, 1)
    bias = _bias_call(rel_bias.astype(F32), _bias_ids(ATTN_TQ, ATTN_TQ, seq))
    o = _attn_call(lam, qt, k, vt, bias, b_subln_g[0].astype(F32).reshape(V_DIM, 1), 1.0 - lam_init)
    return _final_call(o, zg1, x2, p[1], b_w_out[0].astype(BF16), ple_w_gate[1].astype(BF16),
                       ple_w_proj[1].astype(BF16))
```

```python
import functools
import math

import jax
import jax.numpy as jnp
import numpy as np
from jax import lax
from jax.experimental import pallas as pl
from jax.experimental.pallas import tpu as pltpu

F32 = jnp.float32
BF16 = jnp.bfloat16

SUBLANES = 8
LANES = 128
VMEM_LIMIT_BYTES = 56 * 1024 * 1024

EPS = 1e-6
NEG_INF = -1e30
LOG2E = 1.4426950408889634

SSM_GROUP = 16
SSM_STATE = 64
GROUPS_PER_TILE = LANES // SSM_GROUP
STATE_LANES = GROUPS_PER_TILE * SSM_STATE
HEAD_DIM = 64
V_DIM = 2 * HEAD_DIM
REL_BUCKETS = 32
REL_MAX_EXACT = REL_BUCKETS // 2
REL_MAX_DIST = 128

SSM_TIME_CHUNK = 64
ROW_TILE = 256
FINAL_ROW_TILE = 512
ATTN_TQ = 512


def _rms(x, g=None):
    y = x * lax.rsqrt(jnp.mean(x * x, axis=-1, keepdims=True) + EPS)
    return y if g is None else y * g


def _dot(a, b):
    return jnp.dot(a, b, preferred_element_type=F32)


def _sigmoid(x):
    return 1.0 / (1.0 + jnp.exp(-x))


def _gelu_tanh(x):
    c = math.sqrt(2.0 / math.pi)
    return 0.5 * x * (1.0 + jnp.tanh(c * (x + 0.044715 * (x * x * x))))


def _const_spec(shape):
    nd = len(shape)
    return pl.BlockSpec(shape, lambda *_: (0,) * nd, pipeline_mode=pl.Buffered(1))


def _ssm_kernel(x_ref, g_ref, win_ref, bbig_ref, cbig_ref, are_ref, aim_ref, d_ref,
                y_ref, zg_ref,
                hr_ref, hi_ref, us_ref, xs_ref, hs_ref, ys_ref, *, pitch):
    bsz, tc, dm = x_ref.shape
    e = y_ref.shape[-1]
    n_tiles = e // LANES
    rows = bsz * tc

    @pl.when(pl.program_id(0) == 0)
    def _():
        hr_ref[...] = jnp.zeros_like(hr_ref)
        hi_ref[...] = jnp.zeros_like(hi_ref)

    x = x_ref[...].reshape(rows, dm)
    h = _rms(x, g_ref[...]).astype(BF16)
    z = _dot(h, win_ref[:, e:])
    zg_ref[...] = (z * _sigmoid(z)).astype(BF16).reshape(bsz, tc, e)
    u = _dot(h, win_ref[:, :e])
    for j in range(n_tiles):
        for b in range(bsz):
            us_ref[j, b * pitch:b * pitch + tc, :] = u[b * tc:(b + 1) * tc, j * LANES:(j + 1) * LANES]

    for j in range(n_tiles):
        xs, hs, ys = xs_ref.at[j % 2], hs_ref.at[j % 2], ys_ref.at[j % 2]
        u_tb = jnp.concatenate(
            [us_ref[j, pl.ds(t, bsz, stride=pitch), :] for t in range(tc)], axis=0)
        xs[...] = _dot(u_tb.astype(BF16), bbig_ref[j])
        ar = jnp.broadcast_to(are_ref[j], (bsz, STATE_LANES))
        ai = jnp.broadcast_to(aim_ref[j], (bsz, STATE_LANES))
        sr, si = hr_ref[j], hi_ref[j]
        for t in range(tc):
            r0 = t * bsz
            xr = xs[r0:r0 + bsz, :STATE_LANES]
            xi = xs[r0:r0 + bsz, STATE_LANES:]
            sr, si = ar * sr - ai * si + xr, ar * si + ai * sr + xi
            hs[r0:r0 + bsz, :STATE_LANES] = sr
            hs[r0:r0 + bsz, STATE_LANES:] = si
        hr_ref[j] = sr
        hi_ref[j] = si

        y_tb = _dot(hs[...].astype(BF16), cbig_ref[j])
        y_tb = y_tb + d_ref[:, j * LANES:(j + 1) * LANES] * u_tb
        ys[...] = _gelu_tanh(y_tb)
        for b in range(bsz):
            y_ref[b, :, j * LANES:(j + 1) * LANES] = ys[pl.ds(b, tc, stride=bsz), :].astype(BF16)


def _ssm_call(x, norm_g, w_in, bbig, cbig, a_re, a_im, d):
    bsz, seq, dm = x.shape
    e = d.shape[-1]
    tc = SSM_TIME_CHUNK
    assert bsz == SUBLANES and seq % tc == 0 and e % LANES == 0 and tc % (2 * SUBLANES) == 0
    pitch = tc + SUBLANES
    n_tiles = e // LANES
    rows = bsz * tc
    out_shape = [jax.ShapeDtypeStruct((bsz, seq, e), BF16)] * 2
    blk = lambda i: (0, i, 0)
    return pl.pallas_call(
        functools.partial(_ssm_kernel, pitch=pitch),
        grid=(seq // tc,),
        in_specs=[pl.BlockSpec((bsz, tc, dm), blk),
                  _const_spec(norm_g.shape), _const_spec(w_in.shape), _const_spec(bbig.shape),
                  _const_spec(cbig.shape), _const_spec(a_re.shape), _const_spec(a_im.shape),
                  _const_spec(d.shape)],
        out_specs=[pl.BlockSpec((bsz, tc, e), blk)] * 2,
        out_shape=out_shape,
        scratch_shapes=[pltpu.VMEM((n_tiles, bsz, STATE_LANES), F32),
                        pltpu.VMEM((n_tiles, bsz, STATE_LANES), F32),
                        pltpu.VMEM((n_tiles, bsz * pitch, LANES), F32),
                        pltpu.VMEM((2, rows, 2 * STATE_LANES), F32),
                        pltpu.VMEM((2, rows, 2 * STATE_LANES), F32),
                        pltpu.VMEM((2, rows, LANES), F32)],
        compiler_params=pltpu.CompilerParams(dimension_semantics=("arbitrary",),
                                             vmem_limit_bytes=VMEM_LIMIT_BYTES),
        name="ssm_front",
    )(x, norm_g, w_in, bbig, cbig, a_re, a_im, d)


def _head_norm(t, gain_row, ones_ref, expand_ref):
    ss = _dot((t * t).astype(BF16), ones_ref[...])
    inv = lax.rsqrt(ss * (1.0 / HEAD_DIM) + EPS)
    hi = inv.astype(BF16)
    lo = (inv - hi.astype(F32)).astype(BF16)
    inv_full = _dot(jnp.concatenate([hi, lo], axis=1), expand_ref[...])
    return t * inv_full * gain_row


def _ple(x1, p, wgate_ref, wproj_ref):
    gate = _sigmoid(_dot(_rms(x1).astype(BF16), wgate_ref[...]))
    return x1 + gate * _dot(p.astype(BF16), wproj_ref[...])


def _mid_kernel(y_ref, zg_ref, x_ref, p_ref,
                wglu_ref, wout_ref, wgate_ref, wproj_ref,
                kvg_ref, wk_ref, wv_ref, kng_ref,
                bng_ref, bwin_ref, qng_ref, ones_ref, expand_ref,
                x2_ref, qt_ref, k_ref, vt_ref, zg1_ref):
    e = y_ref.shape[-1]
    n_heads = k_ref.shape[0]
    aw = n_heads * V_DIM
    gab = _dot(y_ref[...], wglu_ref[...])
    gl = gab[:, :e] * _sigmoid(gab[:, e:]) * zg_ref[...].astype(F32)
    x1 = x_ref[...] + _dot(gl.astype(BF16), wout_ref[...])
    x2 = _ple(x1, p_ref[...], wgate_ref, wproj_ref)
    x2_ref[...] = x2

    hk = _rms(x2, kvg_ref[...]).astype(BF16)
    k = _head_norm(_dot(hk, wk_ref[...]), kng_ref[...], ones_ref, expand_ref)
    v = _dot(hk, wv_ref[...])
    hq = _rms(x2, bng_ref[...]).astype(BF16)
    q = _head_norm(_dot(hq, bwin_ref[:, :aw]), qng_ref[...], ones_ref, expand_ref)
    z1 = _dot(hq, bwin_ref[:, aw:])
    zg1_ref[...] = (z1 * _sigmoid(z1)).astype(BF16)
    for hh in range(n_heads):
        sl = slice(hh * V_DIM, (hh + 1) * V_DIM)
        k_ref[hh] = k[:, sl].astype(BF16)
        qt_ref[hh] = q[:, sl].T.astype(BF16)
        vt_ref[hh] = v[:, sl].T.astype(BF16)


def _mid_call(y, zg, x, p, p_layer, wglu, wout, wgate, wproj, kvg, wk, wv, kng, bng, bwin, qng, ones, expand):
    bsz, seq, dm = x.shape
    e = y.shape[-1]
    aw = wk.shape[-1]
    n_heads = aw // V_DIM
    tm = ROW_TILE
    assert seq % tm == 0
    row = lambda b, i: (b, i, 0)
    out_shape = [jax.ShapeDtypeStruct((bsz, seq, dm), F32),
                 jax.ShapeDtypeStruct((bsz, n_heads, V_DIM, seq), BF16),
                 jax.ShapeDtypeStruct((bsz, n_heads, seq, V_DIM), BF16),
                 jax.ShapeDtypeStruct((bsz, n_heads, V_DIM, seq), BF16),
                 jax.ShapeDtypeStruct((bsz, seq, aw), BF16)]
    t_spec = pl.BlockSpec((None, n_heads, V_DIM, tm), lambda b, i: (b, 0, 0, i))
    weights = (wglu, wout, wgate, wproj, kvg, wk, wv, kng, bng, bwin, qng, ones, expand)
    return pl.pallas_call(
        _mid_kernel,
        grid=(bsz, seq // tm),
        in_specs=[pl.BlockSpec((None, tm, e), row), pl.BlockSpec((None, tm, e), row),
                  pl.BlockSpec((None, tm, dm), row), pl.BlockSpec((None, None, tm, p.shape[-1]), lambda b, i: (p_layer, b, i, 0))]
                 + [_const_spec(w.shape) for w in weights],
        out_specs=[pl.BlockSpec((None, tm, dm), row), t_spec,
                   pl.BlockSpec((None, n_heads, tm, V_DIM), lambda b, i: (b, 0, i, 0)), t_spec,
                   pl.BlockSpec((None, tm, aw), row)],
        out_shape=out_shape,
        compiler_params=pltpu.CompilerParams(dimension_semantics=("parallel", "parallel"),
                                             vmem_limit_bytes=VMEM_LIMIT_BYTES),
        name="mid_rowwise",
    )(y, zg, x, p, *weights)


def _bucket_of(rel):
    n = jnp.maximum(rel, 0)
    nf = jnp.maximum(n, 1).astype(F32)
    large = REL_MAX_EXACT + (jnp.log(nf / REL_MAX_EXACT) / math.log(REL_MAX_DIST / REL_MAX_EXACT)
                             * (REL_BUCKETS - REL_MAX_EXACT)).astype(jnp.int32)
    large = jnp.minimum(large, REL_BUCKETS - 1)
    return jnp.where(n < REL_MAX_EXACT, n, large)


def _bias_kernel(table_ref, ids_ref, out_ref):
    hh = pl.program_id(0)
    ids = ids_ref[...]
    far = table_ref[REL_BUCKETS - 1, hh]
    acc = jnp.full(ids.shape, NEG_INF, F32)
    for bkt in range(REL_BUCKETS):
        acc = jnp.where(ids == bkt, (table_ref[bkt, hh] - far) * LOG2E, acc)
    out_ref[...] = acc


def _bias_call(table, ids):
    n_heads = table.shape[1]
    return pl.pallas_call(
        _bias_kernel,
        grid=(n_heads,),
        in_specs=[pl.BlockSpec(memory_space=pltpu.SMEM), _const_spec(ids.shape)],
        out_specs=pl.BlockSpec((None,) + ids.shape, lambda hh: (hh, 0, 0, 0)),
        out_shape=jax.ShapeDtypeStruct((n_heads,) + ids.shape, F32),
        name="rel_bias_tiles",
    )(table, ids)


def _attn_kernel(lam_ref, qt_ref, k_ref, vt_ref, bias_ref, sg_ref, o_ref,
                 m_ref, l_ref, acc_ref, sa_ref, sb_ref, ma_ref, mb_ref, *, out_scale):
    tq = qt_ref.shape[-1]
    tk = tq
    i = pl.program_id(2)
    qt = qt_ref[...]
    top = lax.broadcasted_iota(jnp.int32, qt.shape, 0) < HEAD_DIM
    zero = jnp.zeros_like(qt)
    qz = (jnp.where(top, qt, zero), jnp.where(top, zero, qt))

    m_ref[...] = jnp.full(m_ref.shape, NEG_INF, F32)
    l_ref[...] = jnp.zeros_like(l_ref)
    acc_ref[...] = jnp.zeros_like(acc_ref)

    sub = 2 * LANES
    bufs = ((sa_ref, ma_ref), (sb_ref, mb_ref))

    def scores(buf, blk, bias):
        s_ref, bm_ref = buf
        kb = k_ref[pl.ds(pl.multiple_of(blk * tk, tk), tk), :]
        for c in range(2):
            s = _dot(kb, qz[c])
            if bias is not None:
                s = s + bias
            s_ref[c] = s
            bm_ref[c] = jnp.max(s, axis=0, keepdims=True)

    def update(buf, blk):
        s_ref, bm_ref = buf
        vb = vt_ref[:, pl.ds(pl.multiple_of(blk * tk, tk), tk)]
        for c in range(2):
            m_old = m_ref[c]
            m_new = jnp.maximum(m_old, bm_ref[c])
            alpha = jnp.exp2(m_old - m_new)
            m_ref[c] = m_new
            sums = []
            for n in range(tq // sub):
                qs = slice(n * sub, (n + 1) * sub)
                pv, ls = None, None
                for kt in range(tk // sub):
                    ks = slice(kt * sub, (kt + 1) * sub)
                    p = jnp.exp2(s_ref[c, ks, qs] - m_new[:, qs])
                    psum = jnp.sum(p, axis=0, keepdims=True)
                    d = _dot(vb[:, ks], p.astype(BF16))
                    pv, ls = (d, psum) if pv is None else (pv + d, ls + psum)
                acc_ref[c, :, qs] = alpha[:, qs] * acc_ref[c, :, qs] + pv
                sums.append(ls)
            l_ref[c] = alpha * l_ref[c] + jnp.concatenate(sums, axis=1)

    def step(src, dst, blk, next_bias):
        scores(dst, blk + 1, next_bias)
        update(src, blk)

    bufa, bufb = bufs
    n_far = jnp.maximum(i - 1, 0)
    n_pairs = jnp.maximum(n_far - 1, 0) // 2
    n_left = n_far - 2 * n_pairs

    @pl.when(i == 0)
    def _():
        scores(bufa, 0, bias_ref[0])

    @pl.when(i == 1)
    def _():
        scores(bufa, 0, bias_ref[1])

    @pl.when(i >= 2)
    def _():
        scores(bufa, 0, None)

    def far_body(t, carry):
        step(bufa, bufb, 2 * t, None)
        step(bufb, bufa, 2 * t + 1, None)
        return carry

    lax.fori_loop(0, n_pairs, far_body, 0)

    @pl.when(i == 0)
    def _():
        update(bufa, 0)

    @pl.when(i == 1)
    def _():
        step(bufa, bufb, 0, bias_ref[0])
        update(bufb, 1)

    @pl.when(n_left == 1)
    def _():
        step(bufa, bufb, i - 2, bias_ref[1])
        step(bufb, bufa, i - 1, bias_ref[0])
        update(bufa, i)

    @pl.when(n_left == 2)
    def _():
        step(bufa, bufb, i - 3, None)
        step(bufb, bufa, i - 2, bias_ref[1])
        step(bufa, bufb, i - 1, bias_ref[0])
        update(bufb, i)

    o1 = acc_ref[0] * (1.0 / l_ref[0])
    o2 = acc_ref[1] * (1.0 / l_ref[1])
    o = o1 - lam_ref[...] * o2
    o = o * lax.rsqrt(jnp.mean(o * o, axis=0, keepdims=True) + EPS) * (sg_ref[...] * out_scale)
    o_ref[...] = o.T.astype(BF16)


def _attn_call(lam, qt, k, vt, bias, subln_g, out_scale):
    bsz, n_heads, _, seq = qt.shape
    tq = ATTN_TQ
    assert seq % tq == 0 and bias.shape[1:] == (2, tq, tq)
    return pl.pallas_call(
        functools.partial(_attn_kernel, out_scale=out_scale),
        grid=(bsz, n_heads, seq // tq),
        in_specs=[_const_spec(lam.shape),
                  pl.BlockSpec((None, None, V_DIM, tq), lambda b, hh, i: (b, hh, 0, i)),
                  pl.BlockSpec((None, None, seq, V_DIM), lambda b, hh, i: (b, hh, 0, 0)),
                  pl.BlockSpec((None, None, V_DIM, seq), lambda b, hh, i: (b, hh, 0, 0)),
                  pl.BlockSpec((None,) + bias.shape[1:], lambda b, hh, i: (hh, 0, 0, 0)),
                  _const_spec(subln_g.shape)],
        out_specs=pl.BlockSpec((None, tq, V_DIM), lambda b, hh, i: (b, i, hh)),
        out_shape=jax.ShapeDtypeStruct((bsz, seq, n_heads * V_DIM), BF16),
        scratch_shapes=[pltpu.VMEM((2, 1, tq), F32), pltpu.VMEM((2, 1, tq), F32),
                        pltpu.VMEM((2, V_DIM, tq), F32),
                        pltpu.VMEM((2, tq, tq), F32), pltpu.VMEM((2, tq, tq), F32),
                        pltpu.VMEM((2, 1, tq), F32), pltpu.VMEM((2, 1, tq), F32)],
        compiler_params=pltpu.CompilerParams(dimension_semantics=("parallel", "parallel", "arbitrary"),
                                             vmem_limit_bytes=VMEM_LIMIT_BYTES),
        name="diff_attention",
    )(lam, qt, k, vt, bias, subln_g)


def _final_kernel(o_ref, zg_ref, x_ref, p_ref, wout_ref, wgate_ref, wproj_ref, out_ref):
    g = (o_ref[...].astype(F32) * zg_ref[...].astype(F32)).astype(BF16)
    x1 = x_ref[...] + _dot(g, wout_ref[...])
    out_ref[...] = _ple(x1, p_ref[...], wgate_ref, wproj_ref)


def _final_call(o, zg, x, p, p_layer, wout, wgate, wproj):
    bsz, seq, dm = x.shape
    tm = FINAL_ROW_TILE
    assert seq % tm == 0
    row = lambda b, i: (b, i, 0)
    return pl.pallas_call(
        _final_kernel,
        grid=(bsz, seq // tm),
        in_specs=[pl.BlockSpec((None, tm, o.shape[-1]), row), pl.BlockSpec((None, tm, zg.shape[-1]), row),
                  pl.BlockSpec((None, tm, dm), row), pl.BlockSpec((None, None, tm, p.shape[-1]), lambda b, i: (p_layer, b, i, 0)),
                  _const_spec(wout.shape), _const_spec(wgate.shape), _const_spec(wproj.shape)],
        out_specs=pl.BlockSpec((None, tm, dm), row),
        out_shape=jax.ShapeDtypeStruct((bsz, seq, dm), F32),
        compiler_params=pltpu.CompilerParams(dimension_semantics=("parallel", "parallel"),
                                             vmem_limit_bytes=VMEM_LIMIT_BYTES),
        name="final_rowwise",
    )(o, zg, x, p, wout, wgate, wproj)


def _ssm_params(lam_re, lam_im, log_dt, b_re, b_im, c_re, c_im):
    lr, li = lam_re.astype(F32), lam_im.astype(F32)
    dt = jnp.exp(log_dt.astype(F32))[:, None]
    mag = jnp.exp(lr * dt)
    ab_re, ab_im = mag * jnp.cos(li * dt), mag * jnp.sin(li * dt)
    den = lr * lr + li * li
    nr, ni = ab_re - 1.0, ab_im
    f_re = ((nr * lr + ni * li) / den)[..., None]
    f_im = ((ni * lr - nr * li) / den)[..., None]
    br, bi = b_re.astype(F32), b_im.astype(F32)
    bb_re, bb_im = f_re * br - f_im * bi, f_re * bi + f_im * br
    n_tiles = lam_re.shape[0] // GROUPS_PER_TILE
    eye = jnp.eye(GROUPS_PER_TILE, dtype=F32)

    def pack_b(m):
        m = m.reshape(n_tiles, GROUPS_PER_TILE, SSM_STATE, SSM_GROUP)
        return jnp.einsum('jgpc,gh->jgchp', m, eye).reshape(n_tiles, LANES, STATE_LANES)

    def pack_c(m):
        m = m.reshape(n_tiles, GROUPS_PER_TILE, SSM_GROUP, SSM_STATE)
        return jnp.einsum('jgcp,gh->jgphc', m, eye).reshape(n_tiles, STATE_LANES, LANES)

    bbig = jnp.concatenate([pack_b(bb_re), pack_b(bb_im)], axis=2).astype(BF16)
    cbig = jnp.concatenate([pack_c(c_re.astype(F32)), pack_c(-c_im.astype(F32))], axis=1).astype(BF16)
    a_re = ab_re.reshape(n_tiles, 1, STATE_LANES)
    a_im = ab_im.reshape(n_tiles, 1, STATE_LANES)
    return bbig, cbig, a_re, a_im


def _bias_ids(tk, tq, seq):
    kk = jnp.arange(tk, dtype=jnp.int32)[:, None]
    qq = jnp.arange(tq, dtype=jnp.int32)[None, :]
    rel0 = qq - kk
    rel1 = rel0 + tk
    ids0 = jnp.where(rel0 >= 0, _bucket_of(rel0), -1)
    far = np.arange(tk + 1, max(seq, tk + 2), dtype=np.float32)
    far = REL_MAX_EXACT + (np.log(far / REL_MAX_EXACT) / math.log(REL_MAX_DIST / REL_MAX_EXACT)
                           * (REL_BUCKETS - REL_MAX_EXACT)).astype(np.int32)
    assert far.min() > REL_BUCKETS
    return jnp.stack([ids0, _bucket_of(rel1)])


def kernel(x, p, a_norm_g, a_w_in, a_lam_re, a_lam_im, a_log_dt, a_b_re, a_b_im, a_c_re, a_c_im, a_d,
           a_w_glu, a_w_out, kv_norm_g, w_k, w_v, k_norm_g, b_norm_g, b_w_in, b_q_norm_g, b_lam_q1,
           b_lam_k1, b_lam_q2, b_lam_k2, b_subln_g, b_w_out, rel_bias, ple_w_proj, ple_w_gate):
    assert a_norm_g.shape[0] == 1 and b_norm_g.shape[0] == 1 and p.shape[0] == 2
    bsz, seq, dm = x.shape
    aw = w_k.shape[-1]
    n_half = aw // HEAD_DIM
    row = lambda v: v.reshape(1, -1).astype(F32)

    bbig, cbig, a_re, a_im = _ssm_params(a_lam_re[0], a_lam_im[0], a_log_dt[0], a_b_re[0], a_b_im[0],
                                         a_c_re[0], a_c_im[0])
    y, zg = _ssm_call(x, row(a_norm_g[0]), a_w_in[0].astype(BF16), bbig, cbig, a_re, a_im, row(a_d[0]))

    lane_group = jnp.arange(aw, dtype=jnp.int32) // HEAD_DIM
    ones = (lane_group[:, None] == jnp.arange(LANES, dtype=jnp.int32)[None, :]).astype(BF16)
    expand = jnp.concatenate([ones.T, ones.T], axis=0)
    q_scale = HEAD_DIM ** -0.5 * LOG2E
    x2, qt, k, vt, zg1 = _mid_call(
        y, zg, x, p, 0, a_w_glu[0].astype(BF16), a_w_out[0].astype(BF16),
        ple_w_gate[0].astype(BF16), ple_w_proj[0].astype(BF16),
        row(kv_norm_g), w_k.astype(BF16), w_v.astype(BF16), row(jnp.tile(k_norm_g, n_half)),
        row(b_norm_g[0]), b_w_in[0].astype(BF16), row(jnp.tile(b_q_norm_g[0], n_half) * q_scale),
        ones, expand)

    layer_idx = 1
    lam_init = 0.8 - 0.6 * math.exp(-0.3 * layer_idx)
    lam = (jnp.exp(jnp.sum(b_lam_q1[0].astype(F32) * b_lam_k1[0].astype(F32)))
           - jnp.exp(jnp.sum(b_lam_q2[0].astype(F32) * b_lam_k2[0].astype(F32))) + lam_init).reshape(1, 1)
    bias = _bias_call(rel_bias.astype(F32), _bias_ids(ATTN_TQ, ATTN_TQ, seq))
    o = _attn_call(lam, qt, k, vt, bias, b_subln_g[0].astype(F32).reshape(V_DIM, 1), 1.0 - lam_init)
    return _final_call(o, zg1, x2, p, 1, b_w_out[0].astype(BF16), ple_w_gate[1].astype(BF16),
                       ple_w_proj[1].astype(BF16))
```

```python
import functools
import math

import jax
import jax.numpy as jnp
import numpy as np
from jax import lax
from jax.experimental import pallas as pl
from jax.experimental.pallas import tpu as pltpu

F32 = jnp.float32
BF16 = jnp.bfloat16

SUBLANES = 8
LANES = 128
VMEM_LIMIT_BYTES = 56 * 1024 * 1024

EPS = 1e-6
NEG_INF = -1e30
LOG2E = 1.4426950408889634

SSM_GROUP = 16
SSM_STATE = 64
GROUPS_PER_TILE = LANES // SSM_GROUP
STATE_LANES = GROUPS_PER_TILE * SSM_STATE
HEAD_DIM = 64
V_DIM = 2 * HEAD_DIM
REL_BUCKETS = 32
REL_MAX_EXACT = REL_BUCKETS // 2
REL_MAX_DIST = 128

SSM_TIME_CHUNK = 64
ROW_TILE = 256
FINAL_ROW_TILE = 512
ATTN_TQ = 512


def _rms(x, g=None):
    y = x * lax.rsqrt(jnp.mean(x * x, axis=-1, keepdims=True) + EPS)
    return y if g is None else y * g


def _dot(a, b):
    return jnp.dot(a, b, preferred_element_type=F32)


def _sigmoid(x):
    return 1.0 / (1.0 + jnp.exp(-x))


def _gelu_tanh(x):
    c = math.sqrt(2.0 / math.pi)
    return 0.5 * x * (1.0 + jnp.tanh(c * (x + 0.044715 * (x * x * x))))


def _const_spec(shape):
    nd = len(shape)
    return pl.BlockSpec(shape, lambda *_: (0,) * nd, pipeline_mode=pl.Buffered(1))


def _ssm_kernel(x_ref, g_ref, win_ref, bbig_ref, cbig_ref, are_ref, aim_ref, d_ref,
                y_ref, zg_ref,
                hr_ref, hi_ref, us_ref, xs_ref, hs_ref, ys_ref, *, pitch):
    bsz, tc, dm = x_ref.shape
    e = y_ref.shape[-1]
    n_tiles = e // LANES
    rows = bsz * tc

    @pl.when(pl.program_id(0) == 0)
    def _():
        hr_ref[...] = jnp.zeros_like(hr_ref)
        hi_ref[...] = jnp.zeros_like(hi_ref)

    x = x_ref[...].reshape(rows, dm)
    h = _rms(x, g_ref[...]).astype(BF16)
    z = _dot(h, win_ref[:, e:])
    zg_ref[...] = (z * _sigmoid(z)).astype(BF16).reshape(bsz, tc, e)
    u = _dot(h, win_ref[:, :e])
    for j in range(n_tiles):
        for b in range(bsz):
            us_ref[j, b * pitch:b * pitch + tc, :] = u[b * tc:(b + 1) * tc, j * LANES:(j + 1) * LANES]

    for j in range(n_tiles):
        xs, hs, ys = xs_ref.at[j % 2], hs_ref.at[j % 2], ys_ref.at[j % 2]
        u_tb = jnp.concatenate(
            [us_ref[j, pl.ds(t, bsz, stride=pitch), :] for t in range(tc)], axis=0)
        xs[...] = _dot(u_tb.astype(BF16), bbig_ref[j])
        ar = jnp.broadcast_to(are_ref[j], (bsz, STATE_LANES))
        ai = jnp.broadcast_to(aim_ref[j], (bsz, STATE_LANES))
        sr, si = hr_ref[j], hi_ref[j]
        for t in range(tc):
            r0 = t * bsz
            xr = xs[r0:r0 + bsz, :STATE_LANES]
            xi = xs[r0:r0 + bsz, STATE_LANES:]
            sr, si = ar * sr - ai * si + xr, ar * si + ai * sr + xi
            hs[r0:r0 + bsz, :STATE_LANES] = sr
            hs[r0:r0 + bsz, STATE_LANES:] = si
        hr_ref[j] = sr
        hi_ref[j] = si

        y_tb = _dot(hs[...].astype(BF16), cbig_ref[j])
        y_tb = y_tb + d_ref[:, j * LANES:(j + 1) * LANES] * u_tb
        ys[...] = _gelu_tanh(y_tb)
        for b in range(bsz):
            y_ref[b, :, j * LANES:(j + 1) * LANES] = ys[pl.ds(b, tc, stride=bsz), :].astype(BF16)


def _ssm_call(x, norm_g, w_in, bbig, cbig, a_re, a_im, d):
    bsz, seq, dm = x.shape
    e = d.shape[-1]
    tc = SSM_TIME_CHUNK
    assert bsz == SUBLANES and seq % tc == 0 and e % LANES == 0 and tc % (2 * SUBLANES) == 0
    pitch = tc + SUBLANES
    n_tiles = e // LANES
    rows = bsz * tc
    out_shape = [jax.ShapeDtypeStruct((bsz, seq, e), BF16)] * 2
    blk = lambda i: (0, i, 0)
    return pl.pallas_call(
        functools.partial(_ssm_kernel, pitch=pitch),
        grid=(seq // tc,),
        in_specs=[pl.BlockSpec((bsz, tc, dm), blk),
                  _const_spec(norm_g.shape), _const_spec(w_in.shape), _const_spec(bbig.shape),
                  _const_spec(cbig.shape), _const_spec(a_re.shape), _const_spec(a_im.shape),
                  _const_spec(d.shape)],
        out_specs=[pl.BlockSpec((bsz, tc, e), blk)] * 2,
        out_shape=out_shape,
        scratch_shapes=[pltpu.VMEM((n_tiles, bsz, STATE_LANES), F32),
                        pltpu.VMEM((n_tiles, bsz, STATE_LANES), F32),
                        pltpu.VMEM((n_tiles, bsz * pitch, LANES), F32),
                        pltpu.VMEM((2, rows, 2 * STATE_LANES), F32),
                        pltpu.VMEM((2, rows, 2 * STATE_LANES), F32),
                        pltpu.VMEM((2, rows, LANES), F32)],
        compiler_params=pltpu.CompilerParams(dimension_semantics=("arbitrary",),
                                             vmem_limit_bytes=VMEM_LIMIT_BYTES),
        name="ssm_front",
    )(x, norm_g, w_in, bbig, cbig, a_re, a_im, d)


def _head_norm(t, gain_row, ones_ref, expand_ref):
    ss = _dot((t * t).astype(BF16), ones_ref[...])
    inv = lax.rsqrt(ss * (1.0 / HEAD_DIM) + EPS)
    hi = inv.astype(BF16)
    lo = (inv - hi.astype(F32)).astype(BF16)
    inv_full = _dot(jnp.concatenate([hi, lo], axis=1), expand_ref[...])
    return t * inv_full * gain_row


def _ple(x1, p, wgate_ref, wproj_ref):
    gate = _sigmoid(_dot(_rms(x1).astype(BF16), wgate_ref[...]))
    return x1 + gate * _dot(p.astype(BF16), wproj_ref[...])


def _mid_kernel(y_ref, zg_ref, x_ref, p_ref,
                wglu_ref, wout_ref, wgate_ref, wproj_ref,
                kvg_ref, wk_ref, wv_ref, kng_ref,
                bng_ref, bwin_ref, qng_ref, ones_ref, expand_ref,
                x2_ref, qt_ref, k_ref, vt_ref, zg1_ref):
    e = y_ref.shape[-1]
    n_heads = k_ref.shape[0]
    aw = n_heads * V_DIM
    gab = _dot(y_ref[...], wglu_ref[...])
    gl = gab[:, :e] * _sigmoid(gab[:, e:]) * zg_ref[...].astype(F32)
    x1 = x_ref[...] + _dot(gl.astype(BF16), wout_ref[...])
    x2 = _ple(x1, p_ref[...], wgate_ref, wproj_ref)
    x2_ref[...] = x2

    hk = _rms(x2, kvg_ref[...]).astype(BF16)
    k = _head_norm(_dot(hk, wk_ref[...]), kng_ref[...], ones_ref, expand_ref)
    v = _dot(hk, wv_ref[...])
    hq = _rms(x2, bng_ref[...]).astype(BF16)
    q = _head_norm(_dot(hq, bwin_ref[:, :aw]), qng_ref[...], ones_ref, expand_ref)
    z1 = _dot(hq, bwin_ref[:, aw:])
    zg1_ref[...] = (z1 * _sigmoid(z1)).astype(BF16)
    for hh in range(n_heads):
        sl = slice(hh * V_DIM, (hh + 1) * V_DIM)
        k_ref[hh] = k[:, sl].astype(BF16)
        qt_ref[hh] = q[:, sl].T.astype(BF16)
        vt_ref[hh] = v[:, sl].T.astype(BF16)


def _mid_call(y, zg, x, p, p_layer, wglu, wout, wgate, wproj, kvg, wk, wv, kng, bng, bwin, qng, ones, expand):
    bsz, seq, dm = x.shape
    e = y.shape[-1]
    aw = wk.shape[-1]
    n_heads = aw // V_DIM
    tm = ROW_TILE
    assert seq % tm == 0
    row = lambda b, i: (b, i, 0)
    out_shape = [jax.ShapeDtypeStruct((bsz, seq, dm), F32),
                 jax.ShapeDtypeStruct((bsz, n_heads, V_DIM, seq), BF16),
                 jax.ShapeDtypeStruct((bsz, n_heads, seq, V_DIM), BF16),
                 jax.ShapeDtypeStruct((bsz, n_heads, V_DIM, seq), BF16),
                 jax.ShapeDtypeStruct((bsz, seq, aw), BF16)]
    t_spec = pl.BlockSpec((None, n_heads, V_DIM, tm), lambda b, i: (b, 0, 0, i))
    weights = (wglu, wout, wgate, wproj, kvg, wk, wv, kng, bng, bwin, qng, ones, expand)
    return pl.pallas_call(
        _mid_kernel,
        grid=(bsz, seq // tm),
        in_specs=[pl.BlockSpec((None, tm, e), row), pl.BlockSpec((None, tm, e), row),
                  pl.BlockSpec((None, tm, dm), row), pl.BlockSpec((None, None, tm, p.shape[-1]), lambda b, i: (p_layer, b, i, 0))]
                 + [_const_spec(w.shape) for w in weights],
        out_specs=[pl.BlockSpec((None, tm, dm), row), t_spec,
                   pl.BlockSpec((None, n_heads, tm, V_DIM), lambda b, i: (b, 0, i, 0)), t_spec,
                   pl.BlockSpec((None, tm, aw), row)],
        out_shape=out_shape,
        compiler_params=pltpu.CompilerParams(dimension_semantics=("parallel", "parallel"),
                                             vmem_limit_bytes=VMEM_LIMIT_BYTES),
        name="mid_rowwise",
    )(y, zg, x, p, *weights)


def _bucket_of(rel):
    n = jnp.maximum(rel, 0)
    nf = jnp.maximum(n, 1).astype(F32)
    large = REL_MAX_EXACT + (jnp.log(nf / REL_MAX_EXACT) / math.log(REL_MAX_DIST / REL_MAX_EXACT)
                             * (REL_BUCKETS - REL_MAX_EXACT)).astype(jnp.int32)
    large = jnp.minimum(large, REL_BUCKETS - 1)
    return jnp.where(n < REL_MAX_EXACT, n, large)


def _bias_kernel(table_ref, ids_ref, out_ref):
    hh = pl.program_id(0)
    ids = ids_ref[...]
    far = table_ref[REL_BUCKETS - 1, hh]
    acc = jnp.full(ids.shape, NEG_INF, F32)
    for bkt in range(REL_BUCKETS):
        acc = jnp.where(ids == bkt, (table_ref[bkt, hh] - far) * LOG2E, acc)
    out_ref[...] = acc


def _bias_call(table, ids):
    n_heads = table.shape[1]
    return pl.pallas_call(
        _bias_kernel,
        grid=(n_heads,),
        in_specs=[pl.BlockSpec(memory_space=pltpu.SMEM), _const_spec(ids.shape)],
        out_specs=pl.BlockSpec((None,) + ids.shape, lambda hh: (hh, 0, 0, 0)),
        out_shape=jax.ShapeDtypeStruct((n_heads,) + ids.shape, F32),
        name="rel_bias_tiles",
    )(table, ids)


def _attn_kernel(lam_ref, qt_ref, k_ref, vt_ref, bias_ref, sg_ref, o_ref,
                 m_ref, l_ref, acc_ref, sa_ref, sb_ref, ma_ref, mb_ref, *, out_scale):
    tq = qt_ref.shape[-1]
    tk = tq
    i = pl.program_id(2)
    qt = qt_ref[...]
    top = lax.broadcasted_iota(jnp.int32, qt.shape, 0) < HEAD_DIM
    zero = jnp.zeros_like(qt)
    qz = (jnp.where(top, qt, zero), jnp.where(top, zero, qt))

    m_ref[...] = jnp.full(m_ref.shape, NEG_INF, F32)
    l_ref[...] = jnp.zeros_like(l_ref)
    acc_ref[...] = jnp.zeros_like(acc_ref)

    sub = 2 * LANES
    bufs = ((sa_ref, ma_ref), (sb_ref, mb_ref))

    def scores(buf, blk, bias):
        s_ref, bm_ref = buf
        kb = k_ref[pl.ds(pl.multiple_of(blk * tk, tk), tk), :]
        for c in range(2):
            s = _dot(kb, qz[c])
            if bias is not None:
                s = s + bias
            s_ref[c] = s
            bm_ref[c] = jnp.max(s, axis=0, keepdims=True)

    def update(buf, blk):
        s_ref, bm_ref = buf
        vb = vt_ref[:, pl.ds(pl.multiple_of(blk * tk, tk), tk)]
        for c in range(2):
            m_old = m_ref[c]
            m_new = jnp.maximum(m_old, bm_ref[c])
            alpha = jnp.exp2(m_old - m_new)
            m_ref[c] = m_new
            sums = []
            for n in range(tq // sub):
                qs = slice(n * sub, (n + 1) * sub)
                pv, ls = None, None
                for kt in range(tk // sub):
                    ks = slice(kt * sub, (kt + 1) * sub)
                    p = jnp.exp2(s_ref[c, ks, qs] - m_new[:, qs])
                    psum = jnp.sum(p, axis=0, keepdims=True)
                    d = _dot(vb[:, ks], p.astype(BF16))
                    pv, ls = (d, psum) if pv is None else (pv + d, ls + psum)
                acc_ref[c, :, qs] = alpha[:, qs] * acc_ref[c, :, qs] + pv
                sums.append(ls)
            l_ref[c] = alpha * l_ref[c] + jnp.concatenate(sums, axis=1)

    def step(src, dst, blk, next_bias):
        scores(dst, blk + 1, next_bias)
        update(src, blk)

    bufa, bufb = bufs
    n_far = jnp.maximum(i - 1, 0)
    n_pairs = jnp.maximum(n_far - 1, 0) // 2
    n_left = n_far - 2 * n_pairs

    @pl.when(i == 0)
    def _():
        scores(bufa, 0, bias_ref[0])

    @pl.when(i == 1)
    def _():
        scores(bufa, 0, bias_ref[1])

    @pl.when(i >= 2)
    def _():
        scores(bufa, 0, None)

    def far_body(t, carry):
        step(bufa, bufb, 2 * t, None)
        step(bufb, bufa, 2 * t + 1, None)
        return carry

    lax.fori_loop(0, n_pairs, far_body, 0)

    @pl.when(i == 0)
    def _():
        update(bufa, 0)

    @pl.when(i == 1)
    def _():
        step(bufa, bufb, 0, bias_ref[0])
        update(bufb, 1)

    @pl.when(n_left == 1)
    def _():
        step(bufa, bufb, i - 2, bias_ref[1])
        step(bufb, bufa, i - 1, bias_ref[0])
        update(bufa, i)

    @pl.when(n_left == 2)
    def _():
        step(bufa, bufb, i - 3, None)
        step(bufb, bufa, i - 2, bias_ref[1])
        step(bufa, bufb, i - 1, bias_ref[0])
        update(bufb, i)

    o1 = acc_ref[0] * (1.0 / l_ref[0])
    o2 = acc_ref[1] * (1.0 / l_ref[1])
    o = o1 - lam_ref[...] * o2
    o = o * lax.rsqrt(jnp.mean(o * o, axis=0, keepdims=True) + EPS) * (sg_ref[...] * out_scale)
    o_ref[...] = o.T.astype(BF16)


def _attn_call(lam, qt, k, vt, bias, subln_g, out_scale):
    bsz, n_heads, _, seq = qt.shape
    tq = ATTN_TQ
    assert seq % tq == 0 and bias.shape[1:] == (2, tq, tq)
    return pl.pallas_call(
        functools.partial(_attn_kernel, out_scale=out_scale),
        grid=(bsz, n_heads, seq // tq),
        in_specs=[_const_spec(lam.shape),
                  pl.BlockSpec((None, None, V_DIM, tq), lambda b, hh, i: (b, hh, 0, i)),
                  pl.BlockSpec((None, None, seq, V_DIM), lambda b, hh, i: (b, hh, 0, 0)),
                  pl.BlockSpec((None, None, V_DIM, seq), lambda b, hh, i: (b, hh, 0, 0)),
                  pl.BlockSpec((None,) + bias.shape[1:], lambda b, hh, i: (hh, 0, 0, 0)),
                  _const_spec(subln_g.shape)],
        out_specs=pl.BlockSpec((None, tq, V_DIM), lambda b, hh, i: (b, i, hh)),
        out_shape=jax.ShapeDtypeStruct((bsz, seq, n_heads * V_DIM), BF16),
        scratch_shapes=[pltpu.VMEM((2, 1, tq), F32), pltpu.VMEM((2, 1, tq), F32),
                        pltpu.VMEM((2, V_DIM, tq), F32),
                        pltpu.VMEM((2, tq, tq), F32), pltpu.VMEM((2, tq, tq), F32),
                        pltpu.VMEM((2, 1, tq), F32), pltpu.VMEM((2, 1, tq), F32)],
        compiler_params=pltpu.CompilerParams(dimension_semantics=("parallel", "parallel", "arbitrary"),
                                             vmem_limit_bytes=VMEM_LIMIT_BYTES),
        name="diff_attention",
    )(lam, qt, k, vt, bias, subln_g)


def _attn_order(nq):
    far = [(j, i) for j in range(nq) for i in range(j + 2, nq)]
    near = [(0, 0, 0)] + [e for i in range(1, nq) for e in ((i - 1, i, 1), (i, i, 0))]
    return np.asarray(far, np.int32).T.copy(), np.asarray(near, np.int32).T.copy()


def _attn_flat_kernel(ftab_ref, ntab_ref, lam_ref, qt_ref, k_ref, vt_ref, bias_ref, sg_ref, o_ref,
                      qz_ref, m_ref, l_ref, acc_ref, sa_ref, sb_ref, ma_ref, mb_ref,
                      *, out_scale, tq, n_far, n_near):
    tk = tq
    sub = 2 * LANES
    qt = qt_ref[...]
    top = lax.broadcasted_iota(jnp.int32, qt.shape, 0) < HEAD_DIM
    zero = jnp.zeros_like(qt)
    qz_ref[0] = jnp.where(top, qt, zero)
    qz_ref[1] = jnp.where(top, zero, qt)
    m_ref[...] = jnp.full(m_ref.shape, NEG_INF, F32)
    l_ref[...] = jnp.zeros_like(l_ref)
    acc_ref[...] = jnp.zeros_like(acc_ref)
    bufa, bufb = (sa_ref, ma_ref), (sb_ref, mb_ref)

    def scores(buf, pair, bias):
        s_ref, bm_ref = buf
        j, i = pair
        kb = k_ref[pl.ds(pl.multiple_of(j * tk, tk), tk), :]
        qsl = pl.ds(pl.multiple_of(i * tq, tq), tq)
        for c in range(2):
            s = _dot(kb, qz_ref[c, :, qsl])
            if bias is not None:
                s = s + bias
            s_ref[c] = s
            bm_ref[c] = jnp.max(s, axis=0, keepdims=True)

    def update(buf, pair):
        s_ref, bm_ref = buf
        j, i = pair
        vb = vt_ref[:, pl.ds(pl.multiple_of(j * tk, tk), tk)]
        q0 = pl.multiple_of(i * tq, tq)
        qsl = pl.ds(q0, tq)
        for c in range(2):
            m_old = m_ref[c, :, qsl]
            m_new = jnp.maximum(m_old, bm_ref[c])
            alpha = jnp.exp2(m_old - m_new)
            m_ref[c, :, qsl] = m_new
            sums = []
            for n in range(tq // sub):
                qs = slice(n * sub, (n + 1) * sub)
                asl = pl.ds(pl.multiple_of(q0 + n * sub, sub), sub)
                pv, ls = None, None
                for kt in range(tk // sub):
                    ks = slice(kt * sub, (kt + 1) * sub)
                    p = jnp.exp2(s_ref[c, ks, qs] - m_new[:, qs])
                    psum = jnp.sum(p, axis=0, keepdims=True)
                    d = _dot(vb[:, ks], p.astype(BF16))
                    pv, ls = (d, psum) if pv is None else (pv + d, ls + psum)
                acc_ref[c, :, asl] = alpha[:, qs] * acc_ref[c, :, asl] + pv
                sums.append(ls)
            l_ref[c, :, qsl] = alpha * l_ref[c, :, qsl] + jnp.concatenate(sums, axis=1)

    def finish(i):
        qsl = pl.ds(pl.multiple_of(i * tq, tq), tq)
        o1 = acc_ref[0, :, qsl] * (1.0 / l_ref[0, :, qsl])
        o2 = acc_ref[1, :, qsl] * (1.0 / l_ref[1, :, qsl])
        o = o1 - lam_ref[...] * o2
        o = o * lax.rsqrt(jnp.mean(o * o, axis=0, keepdims=True) + EPS) * (sg_ref[...] * out_scale)
        o_ref[qsl, :] = o.T.astype(BF16)

    far = lambda s: (ftab_ref[0, s], ftab_ref[1, s])
    near = lambda s: (ntab_ref[0, s], ntab_ref[1, s])
    near_bias = lambda s: bias_ref[ntab_ref[2, s]]

    scores(bufa, far(0), None)

    def far_body(t, carry):
        scores(bufb, far(2 * t + 1), None)
        update(bufa, far(2 * t))
        scores(bufa, far(2 * t + 2), None)
        update(bufb, far(2 * t + 1))
        return carry

    lax.fori_loop(0, (n_far - 1) // 2, far_body, 0)
    scores(bufb, near(0), near_bias(0))
    update(bufa, far(n_far - 1))

    def near_body(t, carry):
        scores(bufa, near(2 * t + 1), near_bias(2 * t + 1))
        update(bufb, near(2 * t))
        finish(t)
        scores(bufb, near(2 * t + 2), near_bias(2 * t + 2))
        update(bufa, near(2 * t + 1))
        return carry

    lax.fori_loop(0, (n_near - 1) // 2, near_body, 0)
    update(bufb, near(n_near - 1))
    finish((n_near - 1) // 2)


def _attn_flat_call(lam, qt, k, vt, bias, subln_g, out_scale):
    bsz, n_heads, _, seq = qt.shape
    tq = ATTN_TQ
    nq = seq // tq
    assert seq % tq == 0 and nq % 4 == 0 and bias.shape[1:] == (2, tq, tq)
    ftab, ntab = _attn_order(nq)
    n_far, n_near = ftab.shape[1], ntab.shape[1]
    assert n_far % 2 == 1 and n_near == 2 * nq - 1
    smem = pl.BlockSpec(memory_space=pltpu.SMEM)
    per_head = lambda *blk: pl.BlockSpec((None, None) + blk, lambda b, hh: (b, hh, 0, 0))
    return pl.pallas_call(
        functools.partial(_attn_flat_kernel, out_scale=out_scale, tq=tq, n_far=n_far, n_near=n_near),
        grid=(bsz, n_heads),
        in_specs=[smem, smem, _const_spec(lam.shape),
                  per_head(V_DIM, seq), per_head(seq, V_DIM), per_head(V_DIM, seq),
                  pl.BlockSpec((None,) + bias.shape[1:], lambda b, hh: (hh, 0, 0, 0)),
                  _const_spec(subln_g.shape)],
        out_specs=pl.BlockSpec((None, seq, V_DIM), lambda b, hh: (b, 0, hh)),
        out_shape=jax.ShapeDtypeStruct((bsz, seq, n_heads * V_DIM), BF16),
        scratch_shapes=[pltpu.VMEM((2, V_DIM, seq), BF16),
                        pltpu.VMEM((2, 1, seq), F32), pltpu.VMEM((2, 1, seq), F32),
                        pltpu.VMEM((2, V_DIM, seq), F32),
                        pltpu.VMEM((2, tq, tq), F32), pltpu.VMEM((2, tq, tq), F32),
                        pltpu.VMEM((2, 1, tq), F32), pltpu.VMEM((2, 1, tq), F32)],
        compiler_params=pltpu.CompilerParams(dimension_semantics=("parallel", "parallel"),
                                             vmem_limit_bytes=VMEM_LIMIT_BYTES),
        name="diff_attention",
    )(jnp.asarray(ftab), jnp.asarray(ntab), lam, qt, k, vt, bias, subln_g)


def _final_kernel(o_ref, zg_ref, x_ref, p_ref, wout_ref, wgate_ref, wproj_ref, out_ref):
    g = (o_ref[...].astype(F32) * zg_ref[...].astype(F32)).astype(BF16)
    x1 = x_ref[...] + _dot(g, wout_ref[...])
    out_ref[...] = _ple(x1, p_ref[...], wgate_ref, wproj_ref)


def _final_call(o, zg, x, p, p_layer, wout, wgate, wproj):
    bsz, seq, dm = x.shape
    tm = FINAL_ROW_TILE
    assert seq % tm == 0
    row = lambda b, i: (b, i, 0)
    return pl.pallas_call(
        _final_kernel,
        grid=(bsz, seq // tm),
        in_specs=[pl.BlockSpec((None, tm, o.shape[-1]), row), pl.BlockSpec((None, tm, zg.shape[-1]), row),
                  pl.BlockSpec((None, tm, dm), row), pl.BlockSpec((None, None, tm, p.shape[-1]), lambda b, i: (p_layer, b, i, 0)),
                  _const_spec(wout.shape), _const_spec(wgate.shape), _const_spec(wproj.shape)],
        out_specs=pl.BlockSpec((None, tm, dm), row),
        out_shape=jax.ShapeDtypeStruct((bsz, seq, dm), F32),
        compiler_params=pltpu.CompilerParams(dimension_semantics=("parallel", "parallel"),
                                             vmem_limit_bytes=VMEM_LIMIT_BYTES),
        name="final_rowwise",
    )(o, zg, x, p, wout, wgate, wproj)


def _ssm_params(lam_re, lam_im, log_dt, b_re, b_im, c_re, c_im):
    lr, li = lam_re.astype(F32), lam_im.astype(F32)
    dt = jnp.exp(log_dt.astype(F32))[:, None]
    mag = jnp.exp(lr * dt)
    ab_re, ab_im = mag * jnp.cos(li * dt), mag * jnp.sin(li * dt)
    den = lr * lr + li * li
    nr, ni = ab_re - 1.0, ab_im
    f_re = ((nr * lr + ni * li) / den)[..., None]
    f_im = ((ni * lr - nr * li) / den)[..., None]
    br, bi = b_re.astype(F32), b_im.astype(F32)
    bb_re, bb_im = f_re * br - f_im * bi, f_re * bi + f_im * br
    n_tiles = lam_re.shape[0] // GROUPS_PER_TILE
    eye = jnp.eye(GROUPS_PER_TILE, dtype=F32)

    def pack_b(m):
        m = m.reshape(n_tiles, GROUPS_PER_TILE, SSM_STATE, SSM_GROUP)
        return jnp.einsum('jgpc,gh->jgchp', m, eye).reshape(n_tiles, LANES, STATE_LANES)

    def pack_c(m):
        m = m.reshape(n_tiles, GROUPS_PER_TILE, SSM_GROUP, SSM_STATE)
        return jnp.einsum('jgcp,gh->jgphc', m, eye).reshape(n_tiles, STATE_LANES, LANES)

    bbig = jnp.concatenate([pack_b(bb_re), pack_b(bb_im)], axis=2).astype(BF16)
    cbig = jnp.concatenate([pack_c(c_re.astype(F32)), pack_c(-c_im.astype(F32))], axis=1).astype(BF16)
    a_re = ab_re.reshape(n_tiles, 1, STATE_LANES)
    a_im = ab_im.reshape(n_tiles, 1, STATE_LANES)
    return bbig, cbig, a_re, a_im


def _bias_ids(tk, tq, seq):
    kk = jnp.arange(tk, dtype=jnp.int32)[:, None]
    qq = jnp.arange(tq, dtype=jnp.int32)[None, :]
    rel0 = qq - kk
    rel1 = rel0 + tk
    ids0 = jnp.where(rel0 >= 0, _bucket_of(rel0), -1)
    far = np.arange(tk + 1, max(seq, tk + 2), dtype=np.float32)
    far = REL_MAX_EXACT + (np.log(far / REL_MAX_EXACT) / math.log(REL_MAX_DIST / REL_MAX_EXACT)
                           * (REL_BUCKETS - REL_MAX_EXACT)).astype(np.int32)
    assert far.min() > REL_BUCKETS
    return jnp.stack([ids0, _bucket_of(rel1)])


def kernel(x, p, a_norm_g, a_w_in, a_lam_re, a_lam_im, a_log_dt, a_b_re, a_b_im, a_c_re, a_c_im, a_d,
           a_w_glu, a_w_out, kv_norm_g, w_k, w_v, k_norm_g, b_norm_g, b_w_in, b_q_norm_g, b_lam_q1,
           b_lam_k1, b_lam_q2, b_lam_k2, b_subln_g, b_w_out, rel_bias, ple_w_proj, ple_w_gate):
    assert a_norm_g.shape[0] == 1 and b_norm_g.shape[0] == 1 and p.shape[0] == 2
    bsz, seq, dm = x.shape
    aw = w_k.shape[-1]
    n_half = aw // HEAD_DIM
    row = lambda v: v.reshape(1, -1).astype(F32)

    bbig, cbig, a_re, a_im = _ssm_params(a_lam_re[0], a_lam_im[0], a_log_dt[0], a_b_re[0], a_b_im[0],
                                         a_c_re[0], a_c_im[0])
    y, zg = _ssm_call(x, row(a_norm_g[0]), a_w_in[0].astype(BF16), bbig, cbig, a_re, a_im, row(a_d[0]))

    lane_group = jnp.arange(aw, dtype=jnp.int32) // HEAD_DIM
    ones = (lane_group[:, None] == jnp.arange(LANES, dtype=jnp.int32)[None, :]).astype(BF16)
    expand = jnp.concatenate([ones.T, ones.T], axis=0)
    q_scale = HEAD_DIM ** -0.5 * LOG2E
    x2, qt, k, vt, zg1 = _mid_call(
        y, zg, x, p, 0, a_w_glu[0].astype(BF16), a_w_out[0].astype(BF16),
        ple_w_gate[0].astype(BF16), ple_w_proj[0].astype(BF16),
        row(kv_norm_g), w_k.astype(BF16), w_v.astype(BF16), row(jnp.tile(k_norm_g, n_half)),
        row(b_norm_g[0]), b_w_in[0].astype(BF16), row(jnp.tile(b_q_norm_g[0], n_half) * q_scale),
        ones, expand)

    layer_idx = 1
    lam_init = 0.8 - 0.6 * math.exp(-0.3 * layer_idx)
    lam = (jnp.exp(jnp.sum(b_lam_q1[0].astype(F32) * b_lam_k1[0].astype(F32)))
           - jnp.exp(jnp.sum(b_lam_q2[0].astype(F32) * b_lam_k2[0].astype(F32))) + lam_init).reshape(1, 1)
    bias = _bias_call(rel_bias.astype(F32), _bias_ids(ATTN_TQ, ATTN_TQ, seq))
    o = _attn_flat_call(lam, qt, k, vt, bias, b_subln_g[0].astype(F32).reshape(V_DIM, 1), 1.0 - lam_init)
    return _final_call(o, zg1, x2, p, 1, b_w_out[0].astype(BF16), ple_w_gate[1].astype(BF16),
                       ple_w_proj[1].astype(BF16))
```

```python
import functools
import math

import jax
import jax.numpy as jnp
import numpy as np
from jax import lax
from jax.experimental import pallas as pl
from jax.experimental.pallas import tpu as pltpu

F32 = jnp.float32
BF16 = jnp.bfloat16

SUBLANES = 8
LANES = 128
VMEM_LIMIT_BYTES = 56 * 1024 * 1024

EPS = 1e-6
NEG_INF = -1e30
LOG2E = 1.4426950408889634

SSM_GROUP = 16
SSM_STATE = 64
GROUPS_PER_TILE = LANES // SSM_GROUP
STATE_LANES = GROUPS_PER_TILE * SSM_STATE
HEAD_DIM = 64
V_DIM = 2 * HEAD_DIM
REL_BUCKETS = 32
REL_MAX_EXACT = REL_BUCKETS // 2
REL_MAX_DIST = 128

SSM_TIME_CHUNK = 64
ROW_TILE = 256
FINAL_ROW_TILE = 512
ATTN_TQ = 512


def _rms(x, g=None):
    y = x * lax.rsqrt(jnp.mean(x * x, axis=-1, keepdims=True) + EPS)
    return y if g is None else y * g


def _dot(a, b):
    return jnp.dot(a, b, preferred_element_type=F32)


def _sigmoid(x):
    return 1.0 / (1.0 + jnp.exp(-x))


def _gelu_tanh(x):
    c = math.sqrt(2.0 / math.pi)
    return 0.5 * x * (1.0 + jnp.tanh(c * (x + 0.044715 * (x * x * x))))


def _const_spec(shape):
    nd = len(shape)
    return pl.BlockSpec(shape, lambda *_: (0,) * nd, pipeline_mode=pl.Buffered(1))


def _ssm_kernel(x_ref, g_ref, win_ref, bbig_ref, cbig_ref, are_ref, aim_ref, d_ref,
                y_ref, zg_ref,
                hr_ref, hi_ref, us_ref, xs_ref, hs_ref, ys_ref, *, pitch):
    bsz, tc, dm = x_ref.shape
    e = y_ref.shape[-1]
    n_tiles = e // LANES
    rows = bsz * tc

    @pl.when(pl.program_id(0) == 0)
    def _():
        hr_ref[...] = jnp.zeros_like(hr_ref)
        hi_ref[...] = jnp.zeros_like(hi_ref)

    x = x_ref[...].reshape(rows, dm)
    h = _rms(x, g_ref[...]).astype(BF16)
    z = _dot(h, win_ref[:, e:])
    zg_ref[...] = (z * _sigmoid(z)).astype(BF16).reshape(bsz, tc, e)
    u = _dot(h, win_ref[:, :e])
    for j in range(n_tiles):
        for b in range(bsz):
            us_ref[j, b * pitch:b * pitch + tc, :] = u[b * tc:(b + 1) * tc, j * LANES:(j + 1) * LANES]

    for j in range(n_tiles):
        xs, hs, ys = xs_ref.at[j % 2], hs_ref.at[j % 2], ys_ref.at[j % 2]
        u_tb = jnp.concatenate(
            [us_ref[j, pl.ds(t, bsz, stride=pitch), :] for t in range(tc)], axis=0)
        xs[...] = _dot(u_tb.astype(BF16), bbig_ref[j])
        ar = jnp.broadcast_to(are_ref[j], (bsz, STATE_LANES))
        ai = jnp.broadcast_to(aim_ref[j], (bsz, STATE_LANES))
        sr, si = hr_ref[j], hi_ref[j]
        for t in range(tc):
            r0 = t * bsz
            xr = xs[r0:r0 + bsz, :STATE_LANES]
            xi = xs[r0:r0 + bsz, STATE_LANES:]
            sr, si = ar * sr - ai * si + xr, ar * si + ai * sr + xi
            hs[r0:r0 + bsz, :STATE_LANES] = sr
            hs[r0:r0 + bsz, STATE_LANES:] = si
        hr_ref[j] = sr
        hi_ref[j] = si

        y_tb = _dot(hs[...].astype(BF16), cbig_ref[j])
        y_tb = y_tb + d_ref[:, j * LANES:(j + 1) * LANES] * u_tb
        ys[...] = _gelu_tanh(y_tb)
        for b in range(bsz):
            y_ref[b, :, j * LANES:(j + 1) * LANES] = ys[pl.ds(b, tc, stride=bsz), :].astype(BF16)


def _ssm_call(x, norm_g, w_in, bbig, cbig, a_re, a_im, d):
    bsz, seq, dm = x.shape
    e = d.shape[-1]
    tc = SSM_TIME_CHUNK
    assert bsz == SUBLANES and seq % tc == 0 and e % LANES == 0 and tc % (2 * SUBLANES) == 0
    pitch = tc + SUBLANES
    n_tiles = e // LANES
    rows = bsz * tc
    out_shape = [jax.ShapeDtypeStruct((bsz, seq, e), BF16)] * 2
    blk = lambda i: (0, i, 0)
    return pl.pallas_call(
        functools.partial(_ssm_kernel, pitch=pitch),
        grid=(seq // tc,),
        in_specs=[pl.BlockSpec((bsz, tc, dm), blk),
                  _const_spec(norm_g.shape), _const_spec(w_in.shape), _const_spec(bbig.shape),
                  _const_spec(cbig.shape), _const_spec(a_re.shape), _const_spec(a_im.shape),
                  _const_spec(d.shape)],
        out_specs=[pl.BlockSpec((bsz, tc, e), blk)] * 2,
        out_shape=out_shape,
        scratch_shapes=[pltpu.VMEM((n_tiles, bsz, STATE_LANES), F32),
                        pltpu.VMEM((n_tiles, bsz, STATE_LANES), F32),
                        pltpu.VMEM((n_tiles, bsz * pitch, LANES), F32),
                        pltpu.VMEM((2, rows, 2 * STATE_LANES), F32),
                        pltpu.VMEM((2, rows, 2 * STATE_LANES), F32),
                        pltpu.VMEM((2, rows, LANES), F32)],
        compiler_params=pltpu.CompilerParams(dimension_semantics=("arbitrary",),
                                             vmem_limit_bytes=VMEM_LIMIT_BYTES),
        name="ssm_front",
    )(x, norm_g, w_in, bbig, cbig, a_re, a_im, d)


def _head_norm(t, gain_row, ones_ref, expand_ref):
    ss = _dot((t * t).astype(BF16), ones_ref[...])
    inv = lax.rsqrt(ss * (1.0 / HEAD_DIM) + EPS)
    hi = inv.astype(BF16)
    lo = (inv - hi.astype(F32)).astype(BF16)
    inv_full = _dot(jnp.concatenate([hi, lo], axis=1), expand_ref[...])
    return t * inv_full * gain_row


def _ple(x1, p, wgate_ref, wproj_ref):
    gate = _sigmoid(_dot(_rms(x1).astype(BF16), wgate_ref[...]))
    return x1 + gate * _dot(p.astype(BF16), wproj_ref[...])


def _mid_kernel(y_ref, zg_ref, x_ref, p_ref,
                wglu_ref, wout_ref, wgate_ref, wproj_ref,
                kvg_ref, wk_ref, wv_ref, kng_ref,
                bng_ref, bwin_ref, qng_ref, ones_ref, expand_ref,
                x2_ref, qt_ref, k_ref, vt_ref, zg1_ref):
    e = y_ref.shape[-1]
    n_heads = k_ref.shape[0]
    aw = n_heads * V_DIM
    gab = _dot(y_ref[...], wglu_ref[...])
    gl = gab[:, :e] * _sigmoid(gab[:, e:]) * zg_ref[...].astype(F32)
    x1 = x_ref[...] + _dot(gl.astype(BF16), wout_ref[...])
    x2 = _ple(x1, p_ref[...], wgate_ref, wproj_ref)
    x2_ref[...] = x2

    hk = _rms(x2, kvg_ref[...]).astype(BF16)
    k = _head_norm(_dot(hk, wk_ref[...]), kng_ref[...], ones_ref, expand_ref)
    v = _dot(hk, wv_ref[...])
    hq = _rms(x2, bng_ref[...]).astype(BF16)
    q = _head_norm(_dot(hq, bwin_ref[:, :aw]), qng_ref[...], ones_ref, expand_ref)
    z1 = _dot(hq, bwin_ref[:, aw:])
    zg1_ref[...] = (z1 * _sigmoid(z1)).astype(BF16)
    for hh in range(n_heads):
        sl = slice(hh * V_DIM, (hh + 1) * V_DIM)
        k_ref[hh] = k[:, sl].astype(BF16)
        qt_ref[hh] = q[:, sl].T.astype(BF16)
        vt_ref[hh] = v[:, sl].T.astype(BF16)


def _mid_call(y, zg, x, p, p_layer, wglu, wout, wgate, wproj, kvg, wk, wv, kng, bng, bwin, qng, ones, expand):
    bsz, seq, dm = x.shape
    e = y.shape[-1]
    aw = wk.shape[-1]
    n_heads = aw // V_DIM
    tm = ROW_TILE
    assert seq % tm == 0
    row = lambda b, i: (b, i, 0)
    out_shape = [jax.ShapeDtypeStruct((bsz, seq, dm), F32),
                 jax.ShapeDtypeStruct((bsz, n_heads, V_DIM, seq), BF16),
                 jax.ShapeDtypeStruct((bsz, n_heads, seq, V_DIM), BF16),
                 jax.ShapeDtypeStruct((bsz, n_heads, V_DIM, seq), BF16),
                 jax.ShapeDtypeStruct((bsz, seq, aw), BF16)]
    t_spec = pl.BlockSpec((None, n_heads, V_DIM, tm), lambda b, i: (b, 0, 0, i))
    weights = (wglu, wout, wgate, wproj, kvg, wk, wv, kng, bng, bwin, qng, ones, expand)
    return pl.pallas_call(
        _mid_kernel,
        grid=(bsz, seq // tm),
        in_specs=[pl.BlockSpec((None, tm, e), row), pl.BlockSpec((None, tm, e), row),
                  pl.BlockSpec((None, tm, dm), row), pl.BlockSpec((None, None, tm, p.shape[-1]), lambda b, i: (p_layer, b, i, 0))]
                 + [_const_spec(w.shape) for w in weights],
        out_specs=[pl.BlockSpec((None, tm, dm), row), t_spec,
                   pl.BlockSpec((None, n_heads, tm, V_DIM), lambda b, i: (b, 0, i, 0)), t_spec,
                   pl.BlockSpec((None, tm, aw), row)],
        out_shape=out_shape,
        compiler_params=pltpu.CompilerParams(dimension_semantics=("parallel", "parallel"),
                                             vmem_limit_bytes=VMEM_LIMIT_BYTES),
        name="mid_rowwise",
    )(y, zg, x, p, *weights)


def _bucket_of(rel):
    n = jnp.maximum(rel, 0)
    nf = jnp.maximum(n, 1).astype(F32)
    large = REL_MAX_EXACT + (jnp.log(nf / REL_MAX_EXACT) / math.log(REL_MAX_DIST / REL_MAX_EXACT)
                             * (REL_BUCKETS - REL_MAX_EXACT)).astype(jnp.int32)
    large = jnp.minimum(large, REL_BUCKETS - 1)
    return jnp.where(n < REL_MAX_EXACT, n, large)


def _bias_kernel(table_ref, ids_ref, out_ref):
    hh = pl.program_id(0)
    ids = ids_ref[...]
    far = table_ref[REL_BUCKETS - 1, hh]
    acc = jnp.full(ids.shape, NEG_INF, F32)
    for bkt in range(REL_BUCKETS):
        acc = jnp.where(ids == bkt, (table_ref[bkt, hh] - far) * LOG2E, acc)
    out_ref[...] = acc


def _bias_call(table, ids):
    n_heads = table.shape[1]
    return pl.pallas_call(
        _bias_kernel,
        grid=(n_heads,),
        in_specs=[pl.BlockSpec(memory_space=pltpu.SMEM), _const_spec(ids.shape)],
        out_specs=pl.BlockSpec((None,) + ids.shape, lambda hh: (hh, 0, 0, 0)),
        out_shape=jax.ShapeDtypeStruct((n_heads,) + ids.shape, F32),
        name="rel_bias_tiles",
    )(table, ids)


def _attn_order(nq):
    far = [(j, i) for j in range(nq) for i in range(j + 2, nq)]
    near = [(0, 0)] + [e for i in range(1, nq) for e in ((i - 1, i), (i, i))]
    return np.asarray(far, np.int32).T.copy(), np.asarray(near, np.int32).T.copy()


def _attn_flat_kernel(ftab_ref, ntab_ref, lam_ref, qt_ref, k_ref, vt_ref, bias_ref, sg_ref, o_ref,
                      qz_ref, m_ref, l_ref, acc_ref, sa_ref, sb_ref, ma_ref, mb_ref,
                      *, out_scale, tq, n_far, n_near):
    tk = tq
    sub = 2 * LANES
    qt = qt_ref[...]
    top = lax.broadcasted_iota(jnp.int32, qt.shape, 0) < HEAD_DIM
    zero = jnp.zeros_like(qt)
    qz_ref[0] = jnp.where(top, qt, zero)
    qz_ref[1] = jnp.where(top, zero, qt)
    m_ref[...] = jnp.full(m_ref.shape, NEG_INF, F32)
    l_ref[...] = jnp.zeros_like(l_ref)
    acc_ref[...] = jnp.zeros_like(acc_ref)
    bufa, bufb = (sa_ref, ma_ref), (sb_ref, mb_ref)

    half = tq // 2
    assert half == sub
    colmax = lambda v: jnp.max(v, axis=0, keepdims=True)

    def scores(buf, pair, kind):
        s_ref, bm_ref = buf
        j, i = pair
        kb = k_ref[pl.ds(pl.multiple_of(j * tk, tk), tk), :]
        qsl = pl.ds(pl.multiple_of(i * tq, tq), tq)
        for c in range(2):
            qc = qz_ref[c, :, qsl]
            if kind == "far":
                s = _dot(kb, qc)
                s_ref[c] = s
                bm_ref[c] = colmax(s)
                continue
            top = _dot(kb[:half], qc)
            if kind == "off1":
                bot = _dot(kb[half:], qc)
                bot_l = bot[:, :half] + bias_ref[1, half:, :half]
                bot_r = bot[:, half:]
                s_ref[c, half:, :half] = bot_l
                m_l = jnp.maximum(colmax(top[:, :half]), colmax(bot_l))
            else:
                top = top + bias_ref[0, :half, :]
                bot_r = _dot(kb[half:], qc[:, half:]) + bias_ref[0, half:, half:]
                m_l = colmax(top[:, :half])
            s_ref[c, :half, :] = top
            s_ref[c, half:, half:] = bot_r
            m_r = jnp.maximum(colmax(top[:, half:]), colmax(bot_r))
            bm_ref[c] = jnp.concatenate([m_l, m_r], axis=1)

    def update(buf, pair, diag=False):
        s_ref, bm_ref = buf
        j, i = pair
        vb = vt_ref[:, pl.ds(pl.multiple_of(j * tk, tk), tk)]
        q0 = pl.multiple_of(i * tq, tq)
        qsl = pl.ds(q0, tq)
        for c in range(2):
            m_old = m_ref[c, :, qsl]
            m_new = jnp.maximum(m_old, bm_ref[c])
            alpha = jnp.exp2(m_old - m_new)
            m_ref[c, :, qsl] = m_new
            sums = []
            for n in range(tq // sub):
                qs = slice(n * sub, (n + 1) * sub)
                asl = pl.ds(pl.multiple_of(q0 + n * sub, sub), sub)
                pv, ls = None, None
                for kt in range(tk // sub):
                    if diag and kt > n:
                        continue
                    ks = slice(kt * sub, (kt + 1) * sub)
                    p = jnp.exp2(s_ref[c, ks, qs] - m_new[:, qs])
                    psum = jnp.sum(p, axis=0, keepdims=True)
                    d = _dot(vb[:, ks], p.astype(BF16))
                    pv, ls = (d, psum) if pv is None else (pv + d, ls + psum)
                acc_ref[c, :, asl] = alpha[:, qs] * acc_ref[c, :, asl] + pv
                sums.append(ls)
            l_ref[c, :, qsl] = alpha * l_ref[c, :, qsl] + jnp.concatenate(sums, axis=1)

    def finish(i):
        qsl = pl.ds(pl.multiple_of(i * tq, tq), tq)
        o1 = acc_ref[0, :, qsl] * (1.0 / l_ref[0, :, qsl])
        o2 = acc_ref[1, :, qsl] * (1.0 / l_ref[1, :, qsl])
        o = o1 - lam_ref[...] * o2
        o = o * lax.rsqrt(jnp.mean(o * o, axis=0, keepdims=True) + EPS) * (sg_ref[...] * out_scale)
        o_ref[qsl, :] = o.T.astype(BF16)

    far = lambda s: (ftab_ref[0, s], ftab_ref[1, s])
    near = lambda s: (ntab_ref[0, s], ntab_ref[1, s])

    scores(bufa, far(0), "far")

    def far_body(t, carry):
        scores(bufb, far(2 * t + 1), "far")
        update(bufa, far(2 * t))
        scores(bufa, far(2 * t + 2), "far")
        update(bufb, far(2 * t + 1))
        return carry

    lax.fori_loop(0, (n_far - 1) // 2, far_body, 0)
    scores(bufb, near(0), "diag")
    update(bufa, far(n_far - 1))

    def near_body(t, carry):
        scores(bufa, near(2 * t + 1), "off1")
        update(bufb, near(2 * t), diag=True)
        finish(t)
        scores(bufb, near(2 * t + 2), "diag")
        update(bufa, near(2 * t + 1))
        return carry

    lax.fori_loop(0, (n_near - 1) // 2, near_body, 0)
    update(bufb, near(n_near - 1), diag=True)
    finish((n_near - 1) // 2)


def _attn_flat_call(lam, qt, k, vt, bias, subln_g, out_scale):
    bsz, n_heads, _, seq = qt.shape
    tq = ATTN_TQ
    nq = seq // tq
    assert seq % tq == 0 and nq % 4 == 0 and bias.shape[1:] == (2, tq, tq)
    ftab, ntab = _attn_order(nq)
    n_far, n_near = ftab.shape[1], ntab.shape[1]
    assert n_far % 2 == 1 and n_near == 2 * nq - 1
    smem = pl.BlockSpec(memory_space=pltpu.SMEM)
    per_head = lambda *blk: pl.BlockSpec((None, None) + blk, lambda b, hh: (b, hh, 0, 0))
    return pl.pallas_call(
        functools.partial(_attn_flat_kernel, out_scale=out_scale, tq=tq, n_far=n_far, n_near=n_near),
        grid=(bsz, n_heads),
        in_specs=[smem, smem, _const_spec(lam.shape),
                  per_head(V_DIM, seq), per_head(seq, V_DIM), per_head(V_DIM, seq),
                  pl.BlockSpec((None,) + bias.shape[1:], lambda b, hh: (hh, 0, 0, 0)),
                  _const_spec(subln_g.shape)],
        out_specs=pl.BlockSpec((None, seq, V_DIM), lambda b, hh: (b, 0, hh)),
        out_shape=jax.ShapeDtypeStruct((bsz, seq, n_heads * V_DIM), BF16),
        scratch_shapes=[pltpu.VMEM((2, V_DIM, seq), BF16),
                        pltpu.VMEM((2, 1, seq), F32), pltpu.VMEM((2, 1, seq), F32),
                        pltpu.VMEM((2, V_DIM, seq), F32),
                        pltpu.VMEM((2, tq, tq), F32), pltpu.VMEM((2, tq, tq), F32),
                        pltpu.VMEM((2, 1, tq), F32), pltpu.VMEM((2, 1, tq), F32)],
        compiler_params=pltpu.CompilerParams(dimension_semantics=("parallel", "parallel"),
                                             vmem_limit_bytes=VMEM_LIMIT_BYTES),
        name="diff_attention",
    )(jnp.asarray(ftab), jnp.asarray(ntab), lam, qt, k, vt, bias, subln_g)


def _final_kernel(o_ref, zg_ref, x_ref, p_ref, wout_ref, wgate_ref, wproj_ref, out_ref):
    g = (o_ref[...].astype(F32) * zg_ref[...].astype(F32)).astype(BF16)
    x1 = x_ref[...] + _dot(g, wout_ref[...])
    out_ref[...] = _ple(x1, p_ref[...], wgate_ref, wproj_ref)


def _final_call(o, zg, x, p, p_layer, wout, wgate, wproj):
    bsz, seq, dm = x.shape
    tm = FINAL_ROW_TILE
    assert seq % tm == 0
    row = lambda b, i: (b, i, 0)
    return pl.pallas_call(
        _final_kernel,
        grid=(bsz, seq // tm),
        in_specs=[pl.BlockSpec((None, tm, o.shape[-1]), row), pl.BlockSpec((None, tm, zg.shape[-1]), row),
                  pl.BlockSpec((None, tm, dm), row), pl.BlockSpec((None, None, tm, p.shape[-1]), lambda b, i: (p_layer, b, i, 0)),
                  _const_spec(wout.shape), _const_spec(wgate.shape), _const_spec(wproj.shape)],
        out_specs=pl.BlockSpec((None, tm, dm), row),
        out_shape=jax.ShapeDtypeStruct((bsz, seq, dm), F32),
        compiler_params=pltpu.CompilerParams(dimension_semantics=("parallel", "parallel"),
                                             vmem_limit_bytes=VMEM_LIMIT_BYTES),
        name="final_rowwise",
    )(o, zg, x, p, wout, wgate, wproj)


def _ssm_params(lam_re, lam_im, log_dt, b_re, b_im, c_re, c_im):
    lr, li = lam_re.astype(F32), lam_im.astype(F32)
    dt = jnp.exp(log_dt.astype(F32))[:, None]
    mag = jnp.exp(lr * dt)
    ab_re, ab_im = mag * jnp.cos(li * dt), mag * jnp.sin(li * dt)
    den = lr * lr + li * li
    nr, ni = ab_re - 1.0, ab_im
    f_re = ((nr * lr + ni * li) / den)[..., None]
    f_im = ((ni * lr - nr * li) / den)[..., None]
    br, bi = b_re.astype(F32), b_im.astype(F32)
    bb_re, bb_im = f_re * br - f_im * bi, f_re * bi + f_im * br
    n_tiles = lam_re.shape[0] // GROUPS_PER_TILE
    eye = jnp.eye(GROUPS_PER_TILE, dtype=F32)

    def pack_b(m):
        m = m.reshape(n_tiles, GROUPS_PER_TILE, SSM_STATE, SSM_GROUP)
        return jnp.einsum('jgpc,gh->jgchp', m, eye).reshape(n_tiles, LANES, STATE_LANES)

    def pack_c(m):
        m = m.reshape(n_tiles, GROUPS_PER_TILE, SSM_GROUP, SSM_STATE)
        return jnp.einsum('jgcp,gh->jgphc', m, eye).reshape(n_tiles, STATE_LANES, LANES)

    bbig = jnp.concatenate([pack_b(bb_re), pack_b(bb_im)], axis=2).astype(BF16)
    cbig = jnp.concatenate([pack_c(c_re.astype(F32)), pack_c(-c_im.astype(F32))], axis=1).astype(BF16)
    a_re = ab_re.reshape(n_tiles, 1, STATE_LANES)
    a_im = ab_im.reshape(n_tiles, 1, STATE_LANES)
    return bbig, cbig, a_re, a_im


def _bias_ids(tk, tq, seq):
    kk = jnp.arange(tk, dtype=jnp.int32)[:, None]
    qq = jnp.arange(tq, dtype=jnp.int32)[None, :]
    rel0 = qq - kk
    rel1 = rel0 + tk
    ids0 = jnp.where(rel0 >= 0, _bucket_of(rel0), -1)
    far = np.arange(tk // 2 + 1, max(seq, tk + 2), dtype=np.float32)
    far = REL_MAX_EXACT + (np.log(far / REL_MAX_EXACT) / math.log(REL_MAX_DIST / REL_MAX_EXACT)
                           * (REL_BUCKETS - REL_MAX_EXACT)).astype(np.int32)
    assert far.min() > REL_BUCKETS
    return jnp.stack([ids0, _bucket_of(rel1)])


def kernel(x, p, a_norm_g, a_w_in, a_lam_re, a_lam_im, a_log_dt, a_b_re, a_b_im, a_c_re, a_c_im, a_d,
           a_w_glu, a_w_out, kv_norm_g, w_k, w_v, k_norm_g, b_norm_g, b_w_in, b_q_norm_g, b_lam_q1,
           b_lam_k1, b_lam_q2, b_lam_k2, b_subln_g, b_w_out, rel_bias, ple_w_proj, ple_w_gate):
    assert a_norm_g.shape[0] == 1 and b_norm_g.shape[0] == 1 and p.shape[0] == 2
    bsz, seq, dm = x.shape
    aw = w_k.shape[-1]
    n_half = aw // HEAD_DIM
    row = lambda v: v.reshape(1, -1).astype(F32)

    bbig, cbig, a_re, a_im = _ssm_params(a_lam_re[0], a_lam_im[0], a_log_dt[0], a_b_re[0], a_b_im[0],
                                         a_c_re[0], a_c_im[0])
    y, zg = _ssm_call(x, row(a_norm_g[0]), a_w_in[0].astype(BF16), bbig, cbig, a_re, a_im, row(a_d[0]))

    lane_group = jnp.arange(aw, dtype=jnp.int32) // HEAD_DIM
    ones = (lane_group[:, None] == jnp.arange(LANES, dtype=jnp.int32)[None, :]).astype(BF16)
    expand = jnp.concatenate([ones.T, ones.T], axis=0)
    q_scale = HEAD_DIM ** -0.5 * LOG2E
    x2, qt, k, vt, zg1 = _mid_call(
        y, zg, x, p, 0, a_w_glu[0].astype(BF16), a_w_out[0].astype(BF16),
        ple_w_gate[0].astype(BF16), ple_w_proj[0].astype(BF16),
        row(kv_norm_g), w_k.astype(BF16), w_v.astype(BF16), row(jnp.tile(k_norm_g, n_half)),
        row(b_norm_g[0]), b_w_in[0].astype(BF16), row(jnp.tile(b_q_norm_g[0], n_half) * q_scale),
        ones, expand)

    layer_idx = 1
    lam_init = 0.8 - 0.6 * math.exp(-0.3 * layer_idx)
    lam = (jnp.exp(jnp.sum(b_lam_q1[0].astype(F32) * b_lam_k1[0].astype(F32)))
           - jnp.exp(jnp.sum(b_lam_q2[0].astype(F32) * b_lam_k2[0].astype(F32))) + lam_init).reshape(1, 1)
    bias = _bias_call(rel_bias.astype(F32), _bias_ids(ATTN_TQ, ATTN_TQ, seq))
    o = _attn_flat_call(lam, qt, k, vt, bias, b_subln_g[0].astype(F32).reshape(V_DIM, 1), 1.0 - lam_init)
    return _final_call(o, zg1, x2, p, 1, b_w_out[0].astype(BF16), ple_w_gate[1].astype(BF16),
                       ple_w_proj[1].astype(BF16))
```

```python
import functools
import math

import jax
import jax.numpy as jnp
import numpy as np
from jax import lax
from jax.experimental import pallas as pl
from jax.experimental.pallas import tpu as pltpu

F32 = jnp.float32
BF16 = jnp.bfloat16

SUBLANES = 8
LANES = 128
VMEM_LIMIT_BYTES = 56 * 1024 * 1024

EPS = 1e-6
NEG_INF = -1e30
LOG2E = 1.4426950408889634

SSM_GROUP = 16
SSM_STATE = 64
GROUPS_PER_TILE = LANES // SSM_GROUP
STATE_LANES = GROUPS_PER_TILE * SSM_STATE
HEAD_DIM = 64
V_DIM = 2 * HEAD_DIM
REL_BUCKETS = 32
REL_MAX_EXACT = REL_BUCKETS // 2
REL_MAX_DIST = 128

SSM_TIME_CHUNK = 64
ROW_TILE = 256
FINAL_ROW_TILE = 512
ATTN_TQ = 512


def _rms(x, g=None):
    y = x * lax.rsqrt(jnp.mean(x * x, axis=-1, keepdims=True) + EPS)
    return y if g is None else y * g


def _dot(a, b):
    return jnp.dot(a, b, preferred_element_type=F32)


def _sigmoid(x):
    return 1.0 / (1.0 + jnp.exp(-x))


def _gelu_tanh(x):
    c = math.sqrt(2.0 / math.pi)
    return 0.5 * x * (1.0 + jnp.tanh(c * (x + 0.044715 * (x * x * x))))


def _const_spec(shape):
    nd = len(shape)
    return pl.BlockSpec(shape, lambda *_: (0,) * nd, pipeline_mode=pl.Buffered(1))


def _ssm_kernel(x_ref, g_ref, win_ref, bbig_ref, cbig_ref, are_ref, aim_ref, d_ref,
                y_ref, zg_ref,
                hr_ref, hi_ref, us_ref, xs_ref, hs_ref, ys_ref, *, pitch):
    bsz, tc, dm = x_ref.shape
    e = y_ref.shape[-1]
    n_tiles = e // LANES
    rows = bsz * tc

    @pl.when(pl.program_id(0) == 0)
    def _():
        hr_ref[...] = jnp.zeros_like(hr_ref)
        hi_ref[...] = jnp.zeros_like(hi_ref)

    x = x_ref[...].reshape(rows, dm)
    h = _rms(x, g_ref[...]).astype(BF16)
    z = _dot(h, win_ref[:, e:])
    zg_ref[...] = (z * _sigmoid(z)).astype(BF16).reshape(bsz, tc, e)
    u = _dot(h, win_ref[:, :e])
    for j in range(n_tiles):
        for b in range(bsz):
            us_ref[j, b * pitch:b * pitch + tc, :] = u[b * tc:(b + 1) * tc, j * LANES:(j + 1) * LANES]

    for j in range(n_tiles):
        xs, hs, ys = xs_ref.at[j % 2], hs_ref.at[j % 2], ys_ref.at[j % 2]
        u_tb = jnp.concatenate(
            [us_ref[j, pl.ds(t, bsz, stride=pitch), :] for t in range(tc)], axis=0)
        xs[...] = _dot(u_tb.astype(BF16), bbig_ref[j])
        ar = jnp.broadcast_to(are_ref[j], (bsz, STATE_LANES))
        ai = jnp.broadcast_to(aim_ref[j], (bsz, STATE_LANES))
        sr, si = hr_ref[j], hi_ref[j]
        for t in range(tc):
            r0 = t * bsz
            xr = xs[r0:r0 + bsz, :STATE_LANES]
            xi = xs[r0:r0 + bsz, STATE_LANES:]
            sr, si = ar * sr - ai * si + xr, ar * si + ai * sr + xi
            hs[r0:r0 + bsz, :STATE_LANES] = sr
            hs[r0:r0 + bsz, STATE_LANES:] = si
        hr_ref[j] = sr
        hi_ref[j] = si

        y_tb = _dot(hs[...].astype(BF16), cbig_ref[j])
        y_tb = y_tb + d_ref[:, j * LANES:(j + 1) * LANES] * u_tb
        ys[...] = _gelu_tanh(y_tb)
        for b in range(bsz):
            y_ref[b, :, j * LANES:(j + 1) * LANES] = ys[pl.ds(b, tc, stride=bsz), :].astype(BF16)


def _ssm_call(x, norm_g, w_in, bbig, cbig, a_re, a_im, d):
    bsz, seq, dm = x.shape
    e = d.shape[-1]
    tc = SSM_TIME_CHUNK
    assert bsz == SUBLANES and seq % tc == 0 and e % LANES == 0 and tc % (2 * SUBLANES) == 0
    pitch = tc + SUBLANES
    n_tiles = e // LANES
    rows = bsz * tc
    out_shape = [jax.ShapeDtypeStruct((bsz, seq, e), BF16)] * 2
    blk = lambda i: (0, i, 0)
    return pl.pallas_call(
        functools.partial(_ssm_kernel, pitch=pitch),
        grid=(seq // tc,),
        in_specs=[pl.BlockSpec((bsz, tc, dm), blk),
                  _const_spec(norm_g.shape), _const_spec(w_in.shape), _const_spec(bbig.shape),
                  _const_spec(cbig.shape), _const_spec(a_re.shape), _const_spec(a_im.shape),
                  _const_spec(d.shape)],
        out_specs=[pl.BlockSpec((bsz, tc, e), blk)] * 2,
        out_shape=out_shape,
        scratch_shapes=[pltpu.VMEM((n_tiles, bsz, STATE_LANES), F32),
                        pltpu.VMEM((n_tiles, bsz, STATE_LANES), F32),
                        pltpu.VMEM((n_tiles, bsz * pitch, LANES), F32),
                        pltpu.VMEM((2, rows, 2 * STATE_LANES), F32),
                        pltpu.VMEM((2, rows, 2 * STATE_LANES), F32),
                        pltpu.VMEM((2, rows, LANES), F32)],
        compiler_params=pltpu.CompilerParams(dimension_semantics=("arbitrary",),
                                             vmem_limit_bytes=VMEM_LIMIT_BYTES),
        name="ssm_front",
    )(x, norm_g, w_in, bbig, cbig, a_re, a_im, d)


def _head_norm(t, gain_row, ones_ref, expand_ref):
    ss = _dot((t * t).astype(BF16), ones_ref[...])
    inv = lax.rsqrt(ss * (1.0 / HEAD_DIM) + EPS)
    hi = inv.astype(BF16)
    lo = (inv - hi.astype(F32)).astype(BF16)
    inv_full = _dot(jnp.concatenate([hi, lo], axis=1), expand_ref[...])
    return t * inv_full * gain_row


def _ple(x1, p, wgate_ref, wproj_ref):
    gate = _sigmoid(_dot(_rms(x1).astype(BF16), wgate_ref[...]))
    return x1 + gate * _dot(p.astype(BF16), wproj_ref[...])


def _mid_kernel(y_ref, zg_ref, x_ref, p_ref,
                wglu_ref, wout_ref, wgate_ref, wproj_ref,
                kvg_ref, wk_ref, wv_ref, kng_ref,
                bng_ref, bwin_ref, qng_ref, ones_ref, expand_ref,
                x2_ref, qt_ref, k_ref, vt_ref, zg1_ref):
    e = y_ref.shape[-1]
    n_heads = k_ref.shape[0]
    aw = n_heads * V_DIM
    gab = _dot(y_ref[...], wglu_ref[...])
    gl = gab[:, :e] * _sigmoid(gab[:, e:]) * zg_ref[...].astype(F32)
    x1 = x_ref[...] + _dot(gl.astype(BF16), wout_ref[...])
    x2 = _ple(x1, p_ref[...], wgate_ref, wproj_ref)
    x2_ref[...] = x2

    hk = _rms(x2, kvg_ref[...]).astype(BF16)
    k = _head_norm(_dot(hk, wk_ref[...]), kng_ref[...], ones_ref, expand_ref)
    v = _dot(hk, wv_ref[...])
    hq = _rms(x2, bng_ref[...]).astype(BF16)
    q = _head_norm(_dot(hq, bwin_ref[:, :aw]), qng_ref[...], ones_ref, expand_ref)
    z1 = _dot(hq, bwin_ref[:, aw:])
    zg1_ref[...] = (z1 * _sigmoid(z1)).astype(BF16)
    for hh in range(n_heads):
        sl = slice(hh * V_DIM, (hh + 1) * V_DIM)
        k_ref[hh] = k[:, sl].astype(BF16)
        qt_ref[hh] = q[:, sl].T.astype(BF16)
        vt_ref[hh] = v[:, sl].T.astype(BF16)


def _mid_call(y, zg, x, p, p_layer, wglu, wout, wgate, wproj, kvg, wk, wv, kng, bng, bwin, qng, ones, expand):
    bsz, seq, dm = x.shape
    e = y.shape[-1]
    aw = wk.shape[-1]
    n_heads = aw // V_DIM
    tm = ROW_TILE
    assert seq % tm == 0
    row = lambda b, i: (b, i, 0)
    out_shape = [jax.ShapeDtypeStruct((bsz, seq, dm), F32),
                 jax.ShapeDtypeStruct((bsz, n_heads, V_DIM, seq), BF16),
                 jax.ShapeDtypeStruct((bsz, n_heads, seq, V_DIM), BF16),
                 jax.ShapeDtypeStruct((bsz, n_heads, V_DIM, seq), BF16),
                 jax.ShapeDtypeStruct((bsz, seq, aw), BF16)]
    t_spec = pl.BlockSpec((None, n_heads, V_DIM, tm), lambda b, i: (b, 0, 0, i))
    weights = (wglu, wout, wgate, wproj, kvg, wk, wv, kng, bng, bwin, qng, ones, expand)
    return pl.pallas_call(
        _mid_kernel,
        grid=(bsz, seq // tm),
        in_specs=[pl.BlockSpec((None, tm, e), row), pl.BlockSpec((None, tm, e), row),
                  pl.BlockSpec((None, tm, dm), row), pl.BlockSpec((None, None, tm, p.shape[-1]), lambda b, i: (p_layer, b, i, 0))]
                 + [_const_spec(w.shape) for w in weights],
        out_specs=[pl.BlockSpec((None, tm, dm), row), t_spec,
                   pl.BlockSpec((None, n_heads, tm, V_DIM), lambda b, i: (b, 0, i, 0)), t_spec,
                   pl.BlockSpec((None, tm, aw), row)],
        out_shape=out_shape,
        compiler_params=pltpu.CompilerParams(dimension_semantics=("parallel", "parallel"),
                                             vmem_limit_bytes=VMEM_LIMIT_BYTES),
        name="mid_rowwise",
    )(y, zg, x, p, *weights)


def _bucket_of(rel):
    n = jnp.maximum(rel, 0)
    nf = jnp.maximum(n, 1).astype(F32)
    large = REL_MAX_EXACT + (jnp.log(nf / REL_MAX_EXACT) / math.log(REL_MAX_DIST / REL_MAX_EXACT)
                             * (REL_BUCKETS - REL_MAX_EXACT)).astype(jnp.int32)
    large = jnp.minimum(large, REL_BUCKETS - 1)
    return jnp.where(n < REL_MAX_EXACT, n, large)


def _bias_kernel(table_ref, ids_ref, out_ref):
    hh = pl.program_id(0)
    ids = ids_ref[...]
    far = table_ref[REL_BUCKETS - 1, hh]
    acc = jnp.full(ids.shape, NEG_INF, F32)
    for bkt in range(REL_BUCKETS):
        acc = jnp.where(ids == bkt, (table_ref[bkt, hh] - far) * LOG2E, acc)
    out_ref[...] = acc


def _bias_call(table, ids):
    n_heads = table.shape[1]
    return pl.pallas_call(
        _bias_kernel,
        grid=(n_heads,),
        in_specs=[pl.BlockSpec(memory_space=pltpu.SMEM), _const_spec(ids.shape)],
        out_specs=pl.BlockSpec((None,) + ids.shape, lambda hh: (hh, 0, 0, 0)),
        out_shape=jax.ShapeDtypeStruct((n_heads,) + ids.shape, F32),
        name="rel_bias_tiles",
    )(table, ids)


def _attn_order(nq):
    far = [(j, i) for j in range(nq) for i in range(j + 2, nq)]
    near = [(0, 0)] + [e for i in range(1, nq) for e in ((i - 1, i), (i, i))]
    return np.asarray(far, np.int32).T.copy(), np.asarray(near, np.int32).T.copy()


def _attn_flat_kernel(ftab_ref, ntab_ref, lam_ref, qt_ref, k_ref, vt_ref, bias_ref, sg_ref, o_ref,
                      qz_ref, m_ref, l_ref, acc_ref, sa_ref, sb_ref, ma_ref, mb_ref,
                      *, out_scale, tq, n_far, n_near):
    tk = tq
    sub = 2 * LANES
    qt = qt_ref[...]
    top = lax.broadcasted_iota(jnp.int32, qt.shape, 0) < HEAD_DIM
    zero = jnp.zeros_like(qt)
    qz_ref[0] = jnp.where(top, qt, zero)
    qz_ref[1] = jnp.where(top, zero, qt)
    bufa, bufb = (sa_ref, ma_ref), (sb_ref, mb_ref)

    half = tq // 2
    assert half == sub
    colmax = lambda v: jnp.max(v, axis=0, keepdims=True)

    def scores(buf, pair, kind):
        s_ref, bm_ref = buf
        j, i = pair
        kb = k_ref[pl.ds(pl.multiple_of(j * tk, tk), tk), :]
        qsl = pl.ds(pl.multiple_of(i * tq, tq), tq)
        for c in range(2):
            qc = qz_ref[c, :, qsl]
            if kind == "far":
                s = _dot(kb, qc)
                s_ref[c] = s
                bm_ref[c] = colmax(s)
                continue
            top = _dot(kb[:half], qc)
            if kind == "off1":
                bot = _dot(kb[half:], qc)
                bot_l = bot[:, :half] + bias_ref[1, half:, :half]
                bot_r = bot[:, half:]
                s_ref[c, half:, :half] = bot_l
                m_l = jnp.maximum(colmax(top[:, :half]), colmax(bot_l))
            else:
                top = top + bias_ref[0, :half, :]
                bot_r = _dot(kb[half:], qc[:, half:]) + bias_ref[0, half:, half:]
                m_l = colmax(top[:, :half])
            s_ref[c, :half, :] = top
            s_ref[c, half:, half:] = bot_r
            m_r = jnp.maximum(colmax(top[:, half:]), colmax(bot_r))
            bm_ref[c] = jnp.concatenate([m_l, m_r], axis=1)

    def update(buf, pair, diag=False):
        s_ref, bm_ref = buf
        j, i = pair
        vb = vt_ref[:, pl.ds(pl.multiple_of(j * tk, tk), tk)]
        q0 = pl.multiple_of(i * tq, tq)
        qsl = pl.ds(q0, tq)
        for c in range(2):
            m_old = m_ref[c, :, qsl]
            m_new = jnp.maximum(m_old, bm_ref[c])
            alpha = jnp.exp2(m_old - m_new)
            m_ref[c, :, qsl] = m_new
            sums = []
            for n in range(tq // sub):
                qs = slice(n * sub, (n + 1) * sub)
                asl = pl.ds(pl.multiple_of(q0 + n * sub, sub), sub)
                pv, ls = None, None
                for kt in range(tk // sub):
                    if diag and kt > n:
                        continue
                    ks = slice(kt * sub, (kt + 1) * sub)
                    p = jnp.exp2(s_ref[c, ks, qs] - m_new[:, qs])
                    psum = jnp.sum(p, axis=0, keepdims=True)
                    d = _dot(vb[:, ks], p.astype(BF16))
                    pv, ls = (d, psum) if pv is None else (pv + d, ls + psum)
                acc_ref[c, :, asl] = alpha[:, qs] * acc_ref[c, :, asl] + pv
                sums.append(ls)
            l_ref[c, :, qsl] = alpha * l_ref[c, :, qsl] + jnp.concatenate(sums, axis=1)

    def finish(i):
        qsl = pl.ds(pl.multiple_of(i * tq, tq), tq)
        o1 = acc_ref[0, :, qsl] * (1.0 / l_ref[0, :, qsl])
        o2 = acc_ref[1, :, qsl] * (1.0 / l_ref[1, :, qsl])
        o = o1 - lam_ref[...] * o2
        o = o * lax.rsqrt(jnp.mean(o * o, axis=0, keepdims=True) + EPS) * (sg_ref[...] * out_scale)
        o_ref[qsl, :] = o.T.astype(BF16)

    far = lambda s: (ftab_ref[0, s], ftab_ref[1, s])
    near = lambda s: (ntab_ref[0, s], ntab_ref[1, s])

    scores(bufa, far(0), "far")
    m_ref[...] = jnp.full(m_ref.shape, NEG_INF, F32)
    l_ref[...] = jnp.zeros_like(l_ref)
    acc_ref[...] = jnp.zeros_like(acc_ref)

    def far_two(t):
        scores(bufb, far(2 * t + 1), "far")
        update(bufa, far(2 * t))
        scores(bufa, far(2 * t + 2), "far")
        update(bufb, far(2 * t + 1))

    def near_two(t):
        scores(bufa, near(2 * t + 1), "off1")
        update(bufb, near(2 * t), diag=True)
        finish(t)
        scores(bufb, near(2 * t + 2), "diag")
        update(bufa, near(2 * t + 1))

    def run(two, n_twos):
        def body(t, carry):
            two(2 * t)
            two(2 * t + 1)
            return carry
        lax.fori_loop(0, n_twos // 2, body, 0)
        if n_twos % 2:
            two(n_twos - 1)

    run(far_two, (n_far - 1) // 2)
    scores(bufb, near(0), "diag")
    update(bufa, far(n_far - 1))
    run(near_two, (n_near - 1) // 2)
    update(bufb, near(n_near - 1), diag=True)
    finish((n_near - 1) // 2)


def _attn_flat_call(lam, qt, k, vt, bias, subln_g, out_scale):
    bsz, n_heads, _, seq = qt.shape
    tq = ATTN_TQ
    nq = seq // tq
    assert seq % tq == 0 and nq % 4 == 0 and bias.shape[1:] == (2, tq, tq)
    ftab, ntab = _attn_order(nq)
    n_far, n_near = ftab.shape[1], ntab.shape[1]
    assert n_far % 2 == 1 and n_near == 2 * nq - 1
    smem = pl.BlockSpec(memory_space=pltpu.SMEM)
    per_head = lambda *blk: pl.BlockSpec((None, None) + blk, lambda b, hh: (b, hh, 0, 0))
    return pl.pallas_call(
        functools.partial(_attn_flat_kernel, out_scale=out_scale, tq=tq, n_far=n_far, n_near=n_near),
        grid=(bsz, n_heads),
        in_specs=[smem, smem, _const_spec(lam.shape),
                  per_head(V_DIM, seq), per_head(seq, V_DIM), per_head(V_DIM, seq),
                  pl.BlockSpec((None,) + bias.shape[1:], lambda b, hh: (hh, 0, 0, 0)),
                  _const_spec(subln_g.shape)],
        out_specs=pl.BlockSpec((None, seq, V_DIM), lambda b, hh: (b, 0, hh)),
        out_shape=jax.ShapeDtypeStruct((bsz, seq, n_heads * V_DIM), BF16),
        scratch_shapes=[pltpu.VMEM((2, V_DIM, seq), BF16),
                        pltpu.VMEM((2, 1, seq), F32), pltpu.VMEM((2, 1, seq), F32),
                        pltpu.VMEM((2, V_DIM, seq), F32),
                        pltpu.VMEM((2, tq, tq), F32), pltpu.VMEM((2, tq, tq), F32),
                        pltpu.VMEM((2, 1, tq), F32), pltpu.VMEM((2, 1, tq), F32)],
        compiler_params=pltpu.CompilerParams(dimension_semantics=("parallel", "parallel"),
                                             vmem_limit_bytes=VMEM_LIMIT_BYTES),
        name="diff_attention",
    )(jnp.asarray(ftab), jnp.asarray(ntab), lam, qt, k, vt, bias, subln_g)


def _final_kernel(o_ref, zg_ref, x_ref, p_ref, wout_ref, wgate_ref, wproj_ref, out_ref):
    g = (o_ref[...].astype(F32) * zg_ref[...].astype(F32)).astype(BF16)
    x1 = x_ref[...] + _dot(g, wout_ref[...])
    out_ref[...] = _ple(x1, p_ref[...], wgate_ref, wproj_ref)


def _final_call(o, zg, x, p, p_layer, wout, wgate, wproj):
    bsz, seq, dm = x.shape
    tm = FINAL_ROW_TILE
    assert seq % tm == 0
    row = lambda b, i: (b, i, 0)
    return pl.pallas_call(
        _final_kernel,
        grid=(bsz, seq // tm),
        in_specs=[pl.BlockSpec((None, tm, o.shape[-1]), row), pl.BlockSpec((None, tm, zg.shape[-1]), row),
                  pl.BlockSpec((None, tm, dm), row), pl.BlockSpec((None, None, tm, p.shape[-1]), lambda b, i: (p_layer, b, i, 0)),
                  _const_spec(wout.shape), _const_spec(wgate.shape), _const_spec(wproj.shape)],
        out_specs=pl.BlockSpec((None, tm, dm), row),
        out_shape=jax.ShapeDtypeStruct((bsz, seq, dm), F32),
        compiler_params=pltpu.CompilerParams(dimension_semantics=("parallel", "parallel"),
                                             vmem_limit_bytes=VMEM_LIMIT_BYTES),
        name="final_rowwise",
    )(o, zg, x, p, wout, wgate, wproj)


def _ssm_params(lam_re, lam_im, log_dt, b_re, b_im, c_re, c_im):
    lr, li = lam_re.astype(F32), lam_im.astype(F32)
    dt = jnp.exp(log_dt.astype(F32))[:, None]
    mag = jnp.exp(lr * dt)
    ab_re, ab_im = mag * jnp.cos(li * dt), mag * jnp.sin(li * dt)
    den = lr * lr + li * li
    nr, ni = ab_re - 1.0, ab_im
    f_re = ((nr * lr + ni * li) / den)[..., None]
    f_im = ((ni * lr - nr * li) / den)[..., None]
    br, bi = b_re.astype(F32), b_im.astype(F32)
    bb_re, bb_im = f_re * br - f_im * bi, f_re * bi + f_im * br
    n_tiles = lam_re.shape[0] // GROUPS_PER_TILE
    eye = jnp.eye(GROUPS_PER_TILE, dtype=F32)

    def pack_b(m):
        m = m.reshape(n_tiles, GROUPS_PER_TILE, SSM_STATE, SSM_GROUP)
        return jnp.einsum('jgpc,gh->jgchp', m, eye).reshape(n_tiles, LANES, STATE_LANES)

    def pack_c(m):
        m = m.reshape(n_tiles, GROUPS_PER_TILE, SSM_GROUP, SSM_STATE)
        return jnp.einsum('jgcp,gh->jgphc', m, eye).reshape(n_tiles, STATE_LANES, LANES)

    bbig = jnp.concatenate([pack_b(bb_re), pack_b(bb_im)], axis=2).astype(BF16)
    cbig = jnp.concatenate([pack_c(c_re.astype(F32)), pack_c(-c_im.astype(F32))], axis=1).astype(BF16)
    a_re = ab_re.reshape(n_tiles, 1, STATE_LANES)
    a_im = ab_im.reshape(n_tiles, 1, STATE_LANES)
    return bbig, cbig, a_re, a_im


def _bias_ids(tk, tq, seq):
    kk = jnp.arange(tk, dtype=jnp.int32)[:, None]
    qq = jnp.arange(tq, dtype=jnp.int32)[None, :]
    rel0 = qq - kk
    rel1 = rel0 + tk
    ids0 = jnp.where(rel0 >= 0, _bucket_of(rel0), -1)
    far = np.arange(tk // 2 + 1, max(seq, tk + 2), dtype=np.float32)
    far = REL_MAX_EXACT + (np.log(far / REL_MAX_EXACT) / math.log(REL_MAX_DIST / REL_MAX_EXACT)
                           * (REL_BUCKETS - REL_MAX_EXACT)).astype(np.int32)
    assert far.min() > REL_BUCKETS
    return jnp.stack([ids0, _bucket_of(rel1)])


def kernel(x, p, a_norm_g, a_w_in, a_lam_re, a_lam_im, a_log_dt, a_b_re, a_b_im, a_c_re, a_c_im, a_d,
           a_w_glu, a_w_out, kv_norm_g, w_k, w_v, k_norm_g, b_norm_g, b_w_in, b_q_norm_g, b_lam_q1,
           b_lam_k1, b_lam_q2, b_lam_k2, b_subln_g, b_w_out, rel_bias, ple_w_proj, ple_w_gate):
    assert a_norm_g.shape[0] == 1 and b_norm_g.shape[0] == 1 and p.shape[0] == 2
    bsz, seq, dm = x.shape
    aw = w_k.shape[-1]
    n_half = aw // HEAD_DIM
    row = lambda v: v.reshape(1, -1).astype(F32)

    bbig, cbig, a_re, a_im = _ssm_params(a_lam_re[0], a_lam_im[0], a_log_dt[0], a_b_re[0], a_b_im[0],
                                         a_c_re[0], a_c_im[0])
    y, zg = _ssm_call(x, row(a_norm_g[0]), a_w_in[0].astype(BF16), bbig, cbig, a_re, a_im, row(a_d[0]))

    lane_group = jnp.arange(aw, dtype=jnp.int32) // HEAD_DIM
    ones = (lane_group[:, None] == jnp.arange(LANES, dtype=jnp.int32)[None, :]).astype(BF16)
    expand = jnp.concatenate([ones.T, ones.T], axis=0)
    q_scale = HEAD_DIM ** -0.5 * LOG2E
    x2, qt, k, vt, zg1 = _mid_call(
        y, zg, x, p, 0, a_w_glu[0].astype(BF16), a_w_out[0].astype(BF16),
        ple_w_gate[0].astype(BF16), ple_w_proj[0].astype(BF16),
        row(kv_norm_g), w_k.astype(BF16), w_v.astype(BF16), row(jnp.tile(k_norm_g, n_half)),
        row(b_norm_g[0]), b_w_in[0].astype(BF16), row(jnp.tile(b_q_norm_g[0], n_half) * q_scale),
        ones, expand)

    layer_idx = 1
    lam_init = 0.8 - 0.6 * math.exp(-0.3 * layer_idx)
    lam = (jnp.exp(jnp.sum(b_lam_q1[0].astype(F32) * b_lam_k1[0].astype(F32)))
           - jnp.exp(jnp.sum(b_lam_q2[0].astype(F32) * b_lam_k2[0].astype(F32))) + lam_init).reshape(1, 1)
    bias = _bias_call(rel_bias.astype(F32), _bias_ids(ATTN_TQ, ATTN_TQ, seq))
    o = _attn_flat_call(lam, qt, k, vt, bias, b_subln_g[0].astype(F32).reshape(V_DIM, 1), 1.0 - lam_init)
    return _final_call(o, zg1, x2, p, 1, b_w_out[0].astype(BF16), ple_w_gate[1].astype(BF16),
                       ple_w_proj[1].astype(BF16))
```

```python
import functools
import math

import jax
import jax.numpy as jnp
import numpy as np
from jax import lax
from jax.experimental import pallas as pl
from jax.experimental.pallas import tpu as pltpu

F32 = jnp.float32
BF16 = jnp.bfloat16

SUBLANES = 8
LANES = 128
VMEM_LIMIT_BYTES = 56 * 1024 * 1024

EPS = 1e-6
NEG_INF = -1e30
LOG2E = 1.4426950408889634

SSM_GROUP = 16
SSM_STATE = 64
GROUPS_PER_TILE = LANES // SSM_GROUP
STATE_LANES = GROUPS_PER_TILE * SSM_STATE
HEAD_DIM = 64
V_DIM = 2 * HEAD_DIM
V_AUG = V_DIM + 2 * SUBLANES
REL_BUCKETS = 32
REL_MAX_EXACT = REL_BUCKETS // 2
REL_MAX_DIST = 128

SSM_TIME_CHUNK = 64
ROW_TILE = 256
FINAL_ROW_TILE = 512
ATTN_TQ = 512


def _rms(x, g=None):
    y = x * lax.rsqrt(jnp.mean(x * x, axis=-1, keepdims=True) + EPS)
    return y if g is None else y * g


def _dot(a, b):
    return jnp.dot(a, b, preferred_element_type=F32)


def _sigmoid(x):
    return 1.0 / (1.0 + jnp.exp(-x))


def _gelu_tanh(x):
    c = math.sqrt(2.0 / math.pi)
    return 0.5 * x * (1.0 + jnp.tanh(c * (x + 0.044715 * (x * x * x))))


def _const_spec(shape):
    nd = len(shape)
    return pl.BlockSpec(shape, lambda *_: (0,) * nd, pipeline_mode=pl.Buffered(1))


def _ssm_kernel(x_ref, g_ref, win_ref, bbig_ref, cbig_ref, are_ref, aim_ref, d_ref,
                y_ref, zg_ref,
                hr_ref, hi_ref, us_ref, xs_ref, hs_ref, ys_ref, *, pitch):
    bsz, tc, dm = x_ref.shape
    e = y_ref.shape[-1]
    n_tiles = e // LANES
    rows = bsz * tc

    @pl.when(pl.program_id(0) == 0)
    def _():
        hr_ref[...] = jnp.zeros_like(hr_ref)
        hi_ref[...] = jnp.zeros_like(hi_ref)

    x = x_ref[...].reshape(rows, dm)
    h = _rms(x, g_ref[...]).astype(BF16)
    z = _dot(h, win_ref[:, e:])
    zg_ref[...] = (z * _sigmoid(z)).astype(BF16).reshape(bsz, tc, e)
    u = _dot(h, win_ref[:, :e])
    for j in range(n_tiles):
        for b in range(bsz):
            us_ref[j, b * pitch:b * pitch + tc, :] = u[b * tc:(b + 1) * tc, j * LANES:(j + 1) * LANES]

    for j in range(n_tiles):
        xs, hs, ys = xs_ref.at[j % 2], hs_ref.at[j % 2], ys_ref.at[j % 2]
        u_tb = jnp.concatenate(
            [us_ref[j, pl.ds(t, bsz, stride=pitch), :] for t in range(tc)], axis=0)
        xs[...] = _dot(u_tb.astype(BF16), bbig_ref[j])
        ar = jnp.broadcast_to(are_ref[j], (bsz, STATE_LANES))
        ai = jnp.broadcast_to(aim_ref[j], (bsz, STATE_LANES))
        sr, si = hr_ref[j], hi_ref[j]
        for t in range(tc):
            r0 = t * bsz
            xr = xs[r0:r0 + bsz, :STATE_LANES]
            xi = xs[r0:r0 + bsz, STATE_LANES:]
            sr, si = ar * sr - ai * si + xr, ar * si + ai * sr + xi
            hs[r0:r0 + bsz, :STATE_LANES] = sr
            hs[r0:r0 + bsz, STATE_LANES:] = si
        hr_ref[j] = sr
        hi_ref[j] = si

        y_tb = _dot(hs[...].astype(BF16), cbig_ref[j])
        y_tb = y_tb + d_ref[:, j * LANES:(j + 1) * LANES] * u_tb
        ys[...] = _gelu_tanh(y_tb)
        for b in range(bsz):
            y_ref[b, :, j * LANES:(j + 1) * LANES] = ys[pl.ds(b, tc, stride=bsz), :].astype(BF16)


def _ssm_call(x, norm_g, w_in, bbig, cbig, a_re, a_im, d):
    bsz, seq, dm = x.shape
    e = d.shape[-1]
    tc = SSM_TIME_CHUNK
    assert bsz == SUBLANES and seq % tc == 0 and e % LANES == 0 and tc % (2 * SUBLANES) == 0
    pitch = tc + SUBLANES
    n_tiles = e // LANES
    rows = bsz * tc
    out_shape = [jax.ShapeDtypeStruct((bsz, seq, e), BF16)] * 2
    blk = lambda i: (0, i, 0)
    return pl.pallas_call(
        functools.partial(_ssm_kernel, pitch=pitch),
        grid=(seq // tc,),
        in_specs=[pl.BlockSpec((bsz, tc, dm), blk),
                  _const_spec(norm_g.shape), _const_spec(w_in.shape), _const_spec(bbig.shape),
                  _const_spec(cbig.shape), _const_spec(a_re.shape), _const_spec(a_im.shape),
                  _const_spec(d.shape)],
        out_specs=[pl.BlockSpec((bsz, tc, e), blk)] * 2,
        out_shape=out_shape,
        scratch_shapes=[pltpu.VMEM((n_tiles, bsz, STATE_LANES), F32),
                        pltpu.VMEM((n_tiles, bsz, STATE_LANES), F32),
                        pltpu.VMEM((n_tiles, bsz * pitch, LANES), F32),
                        pltpu.VMEM((2, rows, 2 * STATE_LANES), F32),
                        pltpu.VMEM((2, rows, 2 * STATE_LANES), F32),
                        pltpu.VMEM((2, rows, LANES), F32)],
        compiler_params=pltpu.CompilerParams(dimension_semantics=("arbitrary",),
                                             vmem_limit_bytes=VMEM_LIMIT_BYTES),
        name="ssm_front",
    )(x, norm_g, w_in, bbig, cbig, a_re, a_im, d)


def _head_norm(t, gain_row, ones_ref, expand_ref):
    ss = _dot((t * t).astype(BF16), ones_ref[...])
    inv = lax.rsqrt(ss * (1.0 / HEAD_DIM) + EPS)
    hi = inv.astype(BF16)
    lo = (inv - hi.astype(F32)).astype(BF16)
    inv_full = _dot(jnp.concatenate([hi, lo], axis=1), expand_ref[...])
    return t * inv_full * gain_row


def _ple(x1, p, wgate_ref, wproj_ref):
    gate = _sigmoid(_dot(_rms(x1).astype(BF16), wgate_ref[...]))
    return x1 + gate * _dot(p.astype(BF16), wproj_ref[...])


def _mid_kernel(y_ref, zg_ref, x_ref, p_ref,
                wglu_ref, wout_ref, wgate_ref, wproj_ref,
                kvg_ref, wk_ref, wv_ref, kng_ref,
                bng_ref, bwin_ref, qng_ref, ones_ref, expand_ref,
                x2_ref, qt_ref, k_ref, vt_ref, zg1_ref):
    e = y_ref.shape[-1]
    n_heads = k_ref.shape[0]
    aw = n_heads * V_DIM
    gab = _dot(y_ref[...], wglu_ref[...])
    gl = gab[:, :e] * _sigmoid(gab[:, e:]) * zg_ref[...].astype(F32)
    x1 = x_ref[...] + _dot(gl.astype(BF16), wout_ref[...])
    x2 = _ple(x1, p_ref[...], wgate_ref, wproj_ref)
    x2_ref[...] = x2

    hk = _rms(x2, kvg_ref[...]).astype(BF16)
    k = _head_norm(_dot(hk, wk_ref[...]), kng_ref[...], ones_ref, expand_ref)
    v = _dot(hk, wv_ref[...])
    hq = _rms(x2, bng_ref[...]).astype(BF16)
    q = _head_norm(_dot(hq, bwin_ref[:, :aw]), qng_ref[...], ones_ref, expand_ref)
    z1 = _dot(hq, bwin_ref[:, aw:])
    zg1_ref[...] = (z1 * _sigmoid(z1)).astype(BF16)
    pad_rows = (lax.broadcasted_iota(jnp.int32, (V_AUG - V_DIM, v.shape[0]), 0) == 0).astype(BF16)
    for hh in range(n_heads):
        sl = slice(hh * V_DIM, (hh + 1) * V_DIM)
        k_ref[hh] = k[:, sl].astype(BF16)
        qt_ref[hh] = q[:, sl].T.astype(BF16)
        vt_ref[hh, :V_DIM, :] = v[:, sl].T.astype(BF16)
        vt_ref[hh, V_DIM:, :] = pad_rows


def _mid_call(y, zg, x, p, p_layer, wglu, wout, wgate, wproj, kvg, wk, wv, kng, bng, bwin, qng, ones, expand):
    bsz, seq, dm = x.shape
    e = y.shape[-1]
    aw = wk.shape[-1]
    n_heads = aw // V_DIM
    tm = ROW_TILE
    assert seq % tm == 0
    row = lambda b, i: (b, i, 0)
    out_shape = [jax.ShapeDtypeStruct((bsz, seq, dm), F32),
                 jax.ShapeDtypeStruct((bsz, n_heads, V_DIM, seq), BF16),
                 jax.ShapeDtypeStruct((bsz, n_heads, seq, V_DIM), BF16),
                 jax.ShapeDtypeStruct((bsz, n_heads, V_AUG, seq), BF16),
                 jax.ShapeDtypeStruct((bsz, seq, aw), BF16)]
    t_spec = lambda rows: pl.BlockSpec((None, n_heads, rows, tm), lambda b, i: (b, 0, 0, i))
    weights = (wglu, wout, wgate, wproj, kvg, wk, wv, kng, bng, bwin, qng, ones, expand)
    return pl.pallas_call(
        _mid_kernel,
        grid=(bsz, seq // tm),
        in_specs=[pl.BlockSpec((None, tm, e), row), pl.BlockSpec((None, tm, e), row),
                  pl.BlockSpec((None, tm, dm), row), pl.BlockSpec((None, None, tm, p.shape[-1]), lambda b, i: (p_layer, b, i, 0))]
                 + [_const_spec(w.shape) for w in weights],
        out_specs=[pl.BlockSpec((None, tm, dm), row), t_spec(V_DIM),
                   pl.BlockSpec((None, n_heads, tm, V_DIM), lambda b, i: (b, 0, i, 0)), t_spec(V_AUG),
                   pl.BlockSpec((None, tm, aw), row)],
        out_shape=out_shape,
        compiler_params=pltpu.CompilerParams(dimension_semantics=("parallel", "parallel"),
                                             vmem_limit_bytes=VMEM_LIMIT_BYTES),
        name="mid_rowwise",
    )(y, zg, x, p, *weights)


def _bucket_of(rel):
    n = jnp.maximum(rel, 0)
    nf = jnp.maximum(n, 1).astype(F32)
    large = REL_MAX_EXACT + (jnp.log(nf / REL_MAX_EXACT) / math.log(REL_MAX_DIST / REL_MAX_EXACT)
                             * (REL_BUCKETS - REL_MAX_EXACT)).astype(jnp.int32)
    large = jnp.minimum(large, REL_BUCKETS - 1)
    return jnp.where(n < REL_MAX_EXACT, n, large)


def _bias_kernel(table_ref, ids_ref, out_ref):
    hh = pl.program_id(0)
    ids = ids_ref[...]
    far = table_ref[REL_BUCKETS - 1, hh]
    acc = jnp.full(ids.shape, NEG_INF, F32)
    for bkt in range(REL_BUCKETS):
        acc = jnp.where(ids == bkt, (table_ref[bkt, hh] - far) * LOG2E, acc)
    out_ref[...] = acc


def _bias_call(table, ids):
    n_heads = table.shape[1]
    return pl.pallas_call(
        _bias_kernel,
        grid=(n_heads,),
        in_specs=[pl.BlockSpec(memory_space=pltpu.SMEM), _const_spec(ids.shape)],
        out_specs=pl.BlockSpec((None,) + ids.shape, lambda hh: (hh, 0, 0, 0)),
        out_shape=jax.ShapeDtypeStruct((n_heads,) + ids.shape, F32),
        name="rel_bias_tiles",
    )(table, ids)


def _attn_order(nq):
    far = [(j, i) for j in range(nq) for i in range(j + 2, nq)]
    near = [(0, 0)] + [e for i in range(1, nq) for e in ((i - 1, i), (i, i))]
    return np.asarray(far, np.int32).T.copy(), np.asarray(near, np.int32).T.copy()


def _attn_flat_kernel(ftab_ref, ntab_ref, lam_ref, qt_ref, k_ref, vt_ref, bias_ref, sg_ref, o_ref,
                      qz_ref, m_ref, acc_ref, sa_ref, sb_ref, ma_ref, mb_ref,
                      *, out_scale, tq, n_far, n_near):
    tk = tq
    sub = 2 * LANES
    qt = qt_ref[...]
    top = lax.broadcasted_iota(jnp.int32, qt.shape, 0) < HEAD_DIM
    zero = jnp.zeros_like(qt)
    qz_ref[0] = jnp.where(top, qt, zero)
    qz_ref[1] = jnp.where(top, zero, qt)
    bufa, bufb = (sa_ref, ma_ref), (sb_ref, mb_ref)

    half = tq // 2
    assert half == sub
    colmax = lambda v: jnp.max(v, axis=0, keepdims=True)

    def scores(buf, pair, kind):
        s_ref, bm_ref = buf
        j, i = pair
        kb = k_ref[pl.ds(pl.multiple_of(j * tk, tk), tk), :]
        qsl = pl.ds(pl.multiple_of(i * tq, tq), tq)
        for c in range(2):
            qc = qz_ref[c, :, qsl]
            if kind == "far":
                s = _dot(kb, qc)
                s_ref[c] = s
                bm_ref[c] = colmax(s)
                continue
            top = _dot(kb[:half], qc)
            if kind == "off1":
                bot = _dot(kb[half:], qc)
                bot_l = bot[:, :half] + bias_ref[1, half:, :half]
                bot_r = bot[:, half:]
                s_ref[c, half:, :half] = bot_l
                m_l = jnp.maximum(colmax(top[:, :half]), colmax(bot_l))
            else:
                top = top + bias_ref[0, :half, :]
                bot_r = _dot(kb[half:], qc[:, half:]) + bias_ref[0, half:, half:]
                m_l = colmax(top[:, :half])
            s_ref[c, :half, :] = top
            s_ref[c, half:, half:] = bot_r
            m_r = jnp.maximum(colmax(top[:, half:]), colmax(bot_r))
            bm_ref[c] = jnp.concatenate([m_l, m_r], axis=1)

    def update(buf, pair, diag=False):
        s_ref, bm_ref = buf
        j, i = pair
        vb = vt_ref[:, pl.ds(pl.multiple_of(j * tk, tk), tk)]
        q0 = pl.multiple_of(i * tq, tq)
        qsl = pl.ds(q0, tq)
        for c in range(2):
            m_old = m_ref[c, :, qsl]
            m_new = jnp.maximum(m_old, bm_ref[c])
            alpha = jnp.exp2(m_old - m_new)
            m_ref[c, :, qsl] = m_new
            for n in range(tq // sub):
                qs = slice(n * sub, (n + 1) * sub)
                asl = pl.ds(pl.multiple_of(q0 + n * sub, sub), sub)
                pv = None
                for kt in range(tk // sub):
                    if diag and kt > n:
                        continue
                    ks = slice(kt * sub, (kt + 1) * sub)
                    p = jnp.exp2(s_ref[c, ks, qs] - m_new[:, qs])
                    d = _dot(vb[:, ks], p.astype(BF16))
                    pv = d if pv is None else pv + d
                acc_ref[c, :, asl] = alpha[:, qs] * acc_ref[c, :, asl] + pv

    def finish(i):
        qsl = pl.ds(pl.multiple_of(i * tq, tq), tq)
        o1 = acc_ref[0, :V_DIM, qsl] * (1.0 / acc_ref[0, V_DIM:V_DIM + 1, qsl])
        o2 = acc_ref[1, :V_DIM, qsl] * (1.0 / acc_ref[1, V_DIM:V_DIM + 1, qsl])
        o = o1 - lam_ref[...] * o2
        o = o * lax.rsqrt(jnp.mean(o * o, axis=0, keepdims=True) + EPS) * (sg_ref[...] * out_scale)
        o_ref[qsl, :] = o.T.astype(BF16)

    far = lambda s: (ftab_ref[0, s], ftab_ref[1, s])
    near = lambda s: (ntab_ref[0, s], ntab_ref[1, s])

    scores(bufa, far(0), "far")
    m_ref[...] = jnp.full(m_ref.shape, NEG_INF, F32)
    acc_ref[...] = jnp.zeros_like(acc_ref)

    def far_two(t):
        scores(bufb, far(2 * t + 1), "far")
        update(bufa, far(2 * t))
        scores(bufa, far(2 * t + 2), "far")
        update(bufb, far(2 * t + 1))

    def near_two(t):
        scores(bufa, near(2 * t + 1), "off1")
        update(bufb, near(2 * t), diag=True)
        finish(t)
        scores(bufb, near(2 * t + 2), "diag")
        update(bufa, near(2 * t + 1))

    def run(two, n_twos):
        def body(t, carry):
            two(2 * t)
            two(2 * t + 1)
            return carry
        lax.fori_loop(0, n_twos // 2, body, 0)
        if n_twos % 2:
            two(n_twos - 1)

    run(far_two, (n_far - 1) // 2)
    scores(bufb, near(0), "diag")
    update(bufa, far(n_far - 1))
    run(near_two, (n_near - 1) // 2)
    update(bufb, near(n_near - 1), diag=True)
    finish((n_near - 1) // 2)


def _attn_flat_call(lam, qt, k, vt, bias, subln_g, out_scale):
    bsz, n_heads, _, seq = qt.shape
    tq = ATTN_TQ
    nq = seq // tq
    assert seq % tq == 0 and nq % 4 == 0 and bias.shape[1:] == (2, tq, tq)
    ftab, ntab = _attn_order(nq)
    n_far, n_near = ftab.shape[1], ntab.shape[1]
    assert n_far % 2 == 1 and n_near == 2 * nq - 1
    smem = pl.BlockSpec(memory_space=pltpu.SMEM)
    per_head = lambda *blk: pl.BlockSpec((None, None) + blk, lambda b, hh: (b, hh, 0, 0))
    return pl.pallas_call(
        functools.partial(_attn_flat_kernel, out_scale=out_scale, tq=tq, n_far=n_far, n_near=n_near),
        grid=(bsz, n_heads),
        in_specs=[smem, smem, _const_spec(lam.shape),
                  per_head(V_DIM, seq), per_head(seq, V_DIM), per_head(V_AUG, seq),
                  pl.BlockSpec((None,) + bias.shape[1:], lambda b, hh: (hh, 0, 0, 0)),
                  _const_spec(subln_g.shape)],
        out_specs=pl.BlockSpec((None, seq, V_DIM), lambda b, hh: (b, 0, hh)),
        out_shape=jax.ShapeDtypeStruct((bsz, seq, n_heads * V_DIM), BF16),
        scratch_shapes=[pltpu.VMEM((2, V_DIM, seq), BF16),
                        pltpu.VMEM((2, 1, seq), F32),
                        pltpu.VMEM((2, V_AUG, seq), F32),
                        pltpu.VMEM((2, tq, tq), F32), pltpu.VMEM((2, tq, tq), F32),
                        pltpu.VMEM((2, 1, tq), F32), pltpu.VMEM((2, 1, tq), F32)],
        compiler_params=pltpu.CompilerParams(dimension_semantics=("parallel", "parallel"),
                                             vmem_limit_bytes=VMEM_LIMIT_BYTES),
        name="diff_attention",
    )(jnp.asarray(ftab), jnp.asarray(ntab), lam, qt, k, vt, bias, subln_g)


def _final_kernel(o_ref, zg_ref, x_ref, p_ref, wout_ref, wgate_ref, wproj_ref, out_ref):
    g = (o_ref[...].astype(F32) * zg_ref[...].astype(F32)).astype(BF16)
    x1 = x_ref[...] + _dot(g, wout_ref[...])
    out_ref[...] = _ple(x1, p_ref[...], wgate_ref, wproj_ref)


def _final_call(o, zg, x, p, p_layer, wout, wgate, wproj):
    bsz, seq, dm = x.shape
    tm = FINAL_ROW_TILE
    assert seq % tm == 0
    row = lambda b, i: (b, i, 0)
    return pl.pallas_call(
        _final_kernel,
        grid=(bsz, seq // tm),
        in_specs=[pl.BlockSpec((None, tm, o.shape[-1]), row), pl.BlockSpec((None, tm, zg.shape[-1]), row),
                  pl.BlockSpec((None, tm, dm), row), pl.BlockSpec((None, None, tm, p.shape[-1]), lambda b, i: (p_layer, b, i, 0)),
                  _const_spec(wout.shape), _const_spec(wgate.shape), _const_spec(wproj.shape)],
        out_specs=pl.BlockSpec((None, tm, dm), row),
        out_shape=jax.ShapeDtypeStruct((bsz, seq, dm), F32),
        compiler_params=pltpu.CompilerParams(dimension_semantics=("parallel", "parallel"),
                                             vmem_limit_bytes=VMEM_LIMIT_BYTES),
        name="final_rowwise",
    )(o, zg, x, p, wout, wgate, wproj)


def _ssm_params(lam_re, lam_im, log_dt, b_re, b_im, c_re, c_im):
    lr, li = lam_re.astype(F32), lam_im.astype(F32)
    dt = jnp.exp(log_dt.astype(F32))[:, None]
    mag = jnp.exp(lr * dt)
    ab_re, ab_im = mag * jnp.cos(li * dt), mag * jnp.sin(li * dt)
    den = lr * lr + li * li
    nr, ni = ab_re - 1.0, ab_im
    f_re = ((nr * lr + ni * li) / den)[..., None]
    f_im = ((ni * lr - nr * li) / den)[..., None]
    br, bi = b_re.astype(F32), b_im.astype(F32)
    bb_re, bb_im = f_re * br - f_im * bi, f_re * bi + f_im * br
    n_tiles = lam_re.shape[0] // GROUPS_PER_TILE
    eye = jnp.eye(GROUPS_PER_TILE, dtype=F32)

    def pack_b(m):
        m = m.reshape(n_tiles, GROUPS_PER_TILE, SSM_STATE, SSM_GROUP)
        return jnp.einsum('jgpc,gh->jgchp', m, eye).reshape(n_tiles, LANES, STATE_LANES)

    def pack_c(m):
        m = m.reshape(n_tiles, GROUPS_PER_TILE, SSM_GROUP, SSM_STATE)
        return jnp.einsum('jgcp,gh->jgphc', m, eye).reshape(n_tiles, STATE_LANES, LANES)

    bbig = jnp.concatenate([pack_b(bb_re), pack_b(bb_im)], axis=2).astype(BF16)
    cbig = jnp.concatenate([pack_c(c_re.astype(F32)), pack_c(-c_im.astype(F32))], axis=1).astype(BF16)
    a_re = ab_re.reshape(n_tiles, 1, STATE_LANES)
    a_im = ab_im.reshape(n_tiles, 1, STATE_LANES)
    return bbig, cbig, a_re, a_im


def _bias_ids(tk, tq, seq):
    kk = jnp.arange(tk, dtype=jnp.int32)[:, None]
    qq = jnp.arange(tq, dtype=jnp.int32)[None, :]
    rel0 = qq - kk
    rel1 = rel0 + tk
    ids0 = jnp.where(rel0 >= 0, _bucket_of(rel0), -1)
    far = np.arange(tk // 2 + 1, max(seq, tk + 2), dtype=np.float32)
    far = REL_MAX_EXACT + (np.log(far / REL_MAX_EXACT) / math.log(REL_MAX_DIST / REL_MAX_EXACT)
                           * (REL_BUCKETS - REL_MAX_EXACT)).astype(np.int32)
    assert far.min() > REL_BUCKETS
    return jnp.stack([ids0, _bucket_of(rel1)])


def kernel(x, p, a_norm_g, a_w_in, a_lam_re, a_lam_im, a_log_dt, a_b_re, a_b_im, a_c_re, a_c_im, a_d,
           a_w_glu, a_w_out, kv_norm_g, w_k, w_v, k_norm_g, b_norm_g, b_w_in, b_q_norm_g, b_lam_q1,
           b_lam_k1, b_lam_q2, b_lam_k2, b_subln_g, b_w_out, rel_bias, ple_w_proj, ple_w_gate):
    assert a_norm_g.shape[0] == 1 and b_norm_g.shape[0] == 1 and p.shape[0] == 2
    bsz, seq, dm = x.shape
    aw = w_k.shape[-1]
    n_half = aw // HEAD_DIM
    row = lambda v: v.reshape(1, -1).astype(F32)

    bbig, cbig, a_re, a_im = _ssm_params(a_lam_re[0], a_lam_im[0], a_log_dt[0], a_b_re[0], a_b_im[0],
                                         a_c_re[0], a_c_im[0])
    y, zg = _ssm_call(x, row(a_norm_g[0]), a_w_in[0].astype(BF16), bbig, cbig, a_re, a_im, row(a_d[0]))

    lane_group = jnp.arange(aw, dtype=jnp.int32) // HEAD_DIM
    ones = (lane_group[:, None] == jnp.arange(LANES, dtype=jnp.int32)[None, :]).astype(BF16)
    expand = jnp.concatenate([ones.T, ones.T], axis=0)
    q_scale = HEAD_DIM ** -0.5 * LOG2E
    x2, qt, k, vt, zg1 = _mid_call(
        y, zg, x, p, 0, a_w_glu[0].astype(BF16), a_w_out[0].astype(BF16),
        ple_w_gate[0].astype(BF16), ple_w_proj[0].astype(BF16),
        row(kv_norm_g), w_k.astype(BF16), w_v.astype(BF16), row(jnp.tile(k_norm_g, n_half)),
        row(b_norm_g[0]), b_w_in[0].astype(BF16), row(jnp.tile(b_q_norm_g[0], n_half) * q_scale),
        ones, expand)

    layer_idx = 1
    lam_init = 0.8 - 0.6 * math.exp(-0.3 * layer_idx)
    lam = (jnp.exp(jnp.sum(b_lam_q1[0].astype(F32) * b_lam_k1[0].astype(F32)))
           - jnp.exp(jnp.sum(b_lam_q2[0].astype(F32) * b_lam_k2[0].astype(F32))) + lam_init).reshape(1, 1)
    bias = _bias_call(rel_bias.astype(F32), _bias_ids(ATTN_TQ, ATTN_TQ, seq))
    o = _attn_flat_call(lam, qt, k, vt, bias, b_subln_g[0].astype(F32).reshape(V_DIM, 1), 1.0 - lam_init)
    return _final_call(o, zg1, x2, p, 1, b_w_out[0].astype(BF16), ple_w_gate[1].astype(BF16),
                       ple_w_proj[1].astype(BF16))
```

```python
import functools
import math

import jax
import jax.numpy as jnp
import numpy as np
from jax import lax
from jax.experimental import pallas as pl
from jax.experimental.pallas import tpu as pltpu

F32 = jnp.float32
BF16 = jnp.bfloat16

SUBLANES = 8
LANES = 128
VMEM_LIMIT_BYTES = 56 * 1024 * 1024

EPS = 1e-6
NEG_INF = -1e30
LOG2E = 1.4426950408889634

SSM_GROUP = 16
SSM_STATE = 64
GROUPS_PER_TILE = LANES // SSM_GROUP
STATE_LANES = GROUPS_PER_TILE * SSM_STATE
HEAD_DIM = 64
V_DIM = 2 * HEAD_DIM
V_AUG = V_DIM + 2 * SUBLANES
REL_BUCKETS = 32
REL_MAX_EXACT = REL_BUCKETS // 2
REL_MAX_DIST = 128

SSM_TIME_CHUNK = 64
ROW_TILE = 256
FINAL_ROW_TILE = 512
ATTN_TQ = 512


def _rms(x, g=None):
    y = x * lax.rsqrt(jnp.mean(x * x, axis=-1, keepdims=True) + EPS)
    return y if g is None else y * g


def _dot(a, b):
    return jnp.dot(a, b, preferred_element_type=F32)


def _sigmoid(x):
    return 1.0 / (1.0 + jnp.exp(-x))


def _gelu_tanh(x):
    c = math.sqrt(2.0 / math.pi)
    return 0.5 * x * (1.0 + jnp.tanh(c * (x + 0.044715 * (x * x * x))))


def _const_spec(shape):
    nd = len(shape)
    return pl.BlockSpec(shape, lambda *_: (0,) * nd, pipeline_mode=pl.Buffered(1))


def _ssm_kernel(x_ref, g_ref, win_ref, bbig_ref, cbig_ref, are_ref, aim_ref, d_ref,
                y_ref, zg_ref,
                hr_ref, hi_ref, us_ref, xs_ref, hs_ref, ys_ref, *, pitch):
    bsz, tc, dm = x_ref.shape
    e = y_ref.shape[-1]
    n_tiles = e // LANES
    rows = bsz * tc

    @pl.when(pl.program_id(0) == 0)
    def _():
        hr_ref[...] = jnp.zeros_like(hr_ref)
        hi_ref[...] = jnp.zeros_like(hi_ref)

    x = x_ref[...].reshape(rows, dm)
    h = _rms(x, g_ref[...]).astype(BF16)
    z = _dot(h, win_ref[:, e:])
    zg_ref[...] = (z * _sigmoid(z)).astype(BF16).reshape(bsz, tc, e)
    u = _dot(h, win_ref[:, :e])
    for j in range(n_tiles):
        for b in range(bsz):
            us_ref[j, b * pitch:b * pitch + tc, :] = u[b * tc:(b + 1) * tc, j * LANES:(j + 1) * LANES]

    for j in range(n_tiles):
        xs, hs, ys = xs_ref.at[j % 2], hs_ref.at[j % 2], ys_ref.at[j % 2]
        u_tb = jnp.concatenate(
            [us_ref[j, pl.ds(t, bsz, stride=pitch), :] for t in range(tc)], axis=0)
        xs[...] = _dot(u_tb.astype(BF16), bbig_ref[j])
        ar = jnp.broadcast_to(are_ref[j], (bsz, STATE_LANES))
        ai = jnp.broadcast_to(aim_ref[j], (bsz, STATE_LANES))
        sr, si = hr_ref[j], hi_ref[j]
        for t in range(tc):
            r0 = t * bsz
            xr = xs[r0:r0 + bsz, :STATE_LANES]
            xi = xs[r0:r0 + bsz, STATE_LANES:]
            sr, si = ar * sr - ai * si + xr, ar * si + ai * sr + xi
            hs[r0:r0 + bsz, :STATE_LANES] = sr
            hs[r0:r0 + bsz, STATE_LANES:] = si
        hr_ref[j] = sr
        hi_ref[j] = si

        y_tb = _dot(hs[...].astype(BF16), cbig_ref[j])
        y_tb = y_tb + d_ref[:, j * LANES:(j + 1) * LANES] * u_tb
        ys[...] = _gelu_tanh(y_tb)
        for b in range(bsz):
            y_ref[b, :, j * LANES:(j + 1) * LANES] = ys[pl.ds(b, tc, stride=bsz), :].astype(BF16)


def _ssm_call(x, norm_g, w_in, bbig, cbig, a_re, a_im, d):
    bsz, seq, dm = x.shape
    e = d.shape[-1]
    tc = SSM_TIME_CHUNK
    assert bsz == SUBLANES and seq % tc == 0 and e % LANES == 0 and tc % (2 * SUBLANES) == 0
    pitch = tc + SUBLANES
    n_tiles = e // LANES
    rows = bsz * tc
    out_shape = [jax.ShapeDtypeStruct((bsz, seq, e), BF16)] * 2
    blk = lambda i: (0, i, 0)
    return pl.pallas_call(
        functools.partial(_ssm_kernel, pitch=pitch),
        grid=(seq // tc,),
        in_specs=[pl.BlockSpec((bsz, tc, dm), blk),
                  _const_spec(norm_g.shape), _const_spec(w_in.shape), _const_spec(bbig.shape),
                  _const_spec(cbig.shape), _const_spec(a_re.shape), _const_spec(a_im.shape),
                  _const_spec(d.shape)],
        out_specs=[pl.BlockSpec((bsz, tc, e), blk)] * 2,
        out_shape=out_shape,
        scratch_shapes=[pltpu.VMEM((n_tiles, bsz, STATE_LANES), F32),
                        pltpu.VMEM((n_tiles, bsz, STATE_LANES), F32),
                        pltpu.VMEM((n_tiles, bsz * pitch, LANES), F32),
                        pltpu.VMEM((2, rows, 2 * STATE_LANES), F32),
                        pltpu.VMEM((2, rows, 2 * STATE_LANES), F32),
                        pltpu.VMEM((2, rows, LANES), F32)],
        compiler_params=pltpu.CompilerParams(dimension_semantics=("arbitrary",),
                                             vmem_limit_bytes=VMEM_LIMIT_BYTES),
        name="ssm_front",
    )(x, norm_g, w_in, bbig, cbig, a_re, a_im, d)


def _head_norm(t, gain_row, ones_ref, expand_ref):
    ss = _dot((t * t).astype(BF16), ones_ref[...])
    inv = lax.rsqrt(ss * (1.0 / HEAD_DIM) + EPS)
    hi = inv.astype(BF16)
    lo = (inv - hi.astype(F32)).astype(BF16)
    inv_full = _dot(jnp.concatenate([hi, lo], axis=1), expand_ref[...])
    return t * inv_full * gain_row


def _ple(x1, p, wgate_ref, wproj_ref):
    gate = _sigmoid(_dot(_rms(x1).astype(BF16), wgate_ref[...]))
    return x1 + gate * _dot(p.astype(BF16), wproj_ref[...])


def _mid_kernel(y_ref, zg_ref, x_ref, p_ref,
                wglu_ref, wout_ref, wgate_ref, wproj_ref,
                kvg_ref, wk_ref, wv_ref, kng_ref,
                bng_ref, bwin_ref, qng_ref, ones_ref, expand_ref,
                x2_ref, qt_ref, k_ref, vt_ref, zg1_ref):
    e = y_ref.shape[-1]
    n_heads = k_ref.shape[0]
    aw = n_heads * V_DIM
    gab = _dot(y_ref[...], wglu_ref[...])
    gl = gab[:, :e] * _sigmoid(gab[:, e:]) * zg_ref[...].astype(F32)
    x1 = x_ref[...] + _dot(gl.astype(BF16), wout_ref[...])
    x2 = _ple(x1, p_ref[...], wgate_ref, wproj_ref)
    x2_ref[...] = x2

    hk = _rms(x2, kvg_ref[...]).astype(BF16)
    k = _head_norm(_dot(hk, wk_ref[...]), kng_ref[...], ones_ref, expand_ref)
    v = _dot(hk, wv_ref[...])
    hq = _rms(x2, bng_ref[...]).astype(BF16)
    q = _head_norm(_dot(hq, bwin_ref[:, :aw]), qng_ref[...], ones_ref, expand_ref)
    z1 = _dot(hq, bwin_ref[:, aw:])
    zg1_ref[...] = (z1 * _sigmoid(z1)).astype(BF16)
    pad_rows = (lax.broadcasted_iota(jnp.int32, (V_AUG - V_DIM, v.shape[0]), 0) == 0).astype(BF16)
    for hh in range(n_heads):
        sl = slice(hh * V_DIM, (hh + 1) * V_DIM)
        k_ref[hh] = k[:, sl].astype(BF16)
        qt_ref[hh] = q[:, sl].T.astype(BF16)
        vt_ref[hh, :V_DIM, :] = v[:, sl].T.astype(BF16)
        vt_ref[hh, V_DIM:, :] = pad_rows


def _mid_call(y, zg, x, p, p_layer, wglu, wout, wgate, wproj, kvg, wk, wv, kng, bng, bwin, qng, ones, expand):
    bsz, seq, dm = x.shape
    e = y.shape[-1]
    aw = wk.shape[-1]
    n_heads = aw // V_DIM
    tm = ROW_TILE
    assert seq % tm == 0
    row = lambda b, i: (b, i, 0)
    out_shape = [jax.ShapeDtypeStruct((bsz, seq, dm), F32),
                 jax.ShapeDtypeStruct((bsz, n_heads, V_DIM, seq), BF16),
                 jax.ShapeDtypeStruct((bsz, n_heads, seq, V_DIM), BF16),
                 jax.ShapeDtypeStruct((bsz, n_heads, V_AUG, seq), BF16),
                 jax.ShapeDtypeStruct((bsz, seq, aw), BF16)]
    t_spec = lambda rows: pl.BlockSpec((None, n_heads, rows, tm), lambda b, i: (b, 0, 0, i))
    weights = (wglu, wout, wgate, wproj, kvg, wk, wv, kng, bng, bwin, qng, ones, expand)
    return pl.pallas_call(
        _mid_kernel,
        grid=(bsz, seq // tm),
        in_specs=[pl.BlockSpec((None, tm, e), row), pl.BlockSpec((None, tm, e), row),
                  pl.BlockSpec((None, tm, dm), row), pl.BlockSpec((None, None, tm, p.shape[-1]), lambda b, i: (p_layer, b, i, 0))]
                 + [_const_spec(w.shape) for w in weights],
        out_specs=[pl.BlockSpec((None, tm, dm), row), t_spec(V_DIM),
                   pl.BlockSpec((None, n_heads, tm, V_DIM), lambda b, i: (b, 0, i, 0)), t_spec(V_AUG),
                   pl.BlockSpec((None, tm, aw), row)],
        out_shape=out_shape,
        compiler_params=pltpu.CompilerParams(dimension_semantics=("parallel", "parallel"),
                                             vmem_limit_bytes=VMEM_LIMIT_BYTES),
        name="mid_rowwise",
    )(y, zg, x, p, *weights)


def _bucket_of(rel):
    n = jnp.maximum(rel, 0)
    nf = jnp.maximum(n, 1).astype(F32)
    large = REL_MAX_EXACT + jnp.trunc(jnp.log(nf / REL_MAX_EXACT) / math.log(REL_MAX_DIST / REL_MAX_EXACT)
                                      * (REL_BUCKETS - REL_MAX_EXACT)).astype(jnp.int32)
    large = jnp.minimum(large, REL_BUCKETS - 1)
    return jnp.where(n < REL_MAX_EXACT, n, large)


def _bias_kernel(table_ref, ids_ref, out_ref):
    hh = pl.program_id(0)
    ids = ids_ref[...]
    far = table_ref[REL_BUCKETS - 1, hh]
    acc = jnp.full(ids.shape, NEG_INF, F32)
    for bkt in range(REL_BUCKETS):
        acc = jnp.where(ids == bkt, (table_ref[bkt, hh] - far) * LOG2E, acc)
    out_ref[...] = acc


def _bias_call(table, ids):
    n_heads = table.shape[1]
    return pl.pallas_call(
        _bias_kernel,
        grid=(n_heads,),
        in_specs=[pl.BlockSpec(memory_space=pltpu.SMEM), _const_spec(ids.shape)],
        out_specs=pl.BlockSpec((None,) + ids.shape, lambda hh: (hh, 0, 0, 0)),
        out_shape=jax.ShapeDtypeStruct((n_heads,) + ids.shape, F32),
        name="rel_bias_tiles",
    )(table, ids)


def _attn_order(nq):
    far = [(j, i) for j in range(nq) for i in range(j + 2, nq)]
    near = [(0, 0)] + [e for i in range(1, nq) for e in ((i - 1, i), (i, i))]
    return np.asarray(far, np.int32).T.copy(), np.asarray(near, np.int32).T.copy()


def _attn_flat_kernel(ftab_ref, ntab_ref, lam_ref, qt_ref, k_ref, vt_ref, bias_ref, sg_ref, o_ref,
                      qz_ref, m_ref, acc_ref, sa_ref, sb_ref, ma_ref, mb_ref,
                      *, out_scale, tq, n_far, n_near):
    tk = tq
    sub = 2 * LANES
    qt = qt_ref[...]
    top = lax.broadcasted_iota(jnp.int32, qt.shape, 0) < HEAD_DIM
    zero = jnp.zeros_like(qt)
    qz_ref[0] = jnp.where(top, qt, zero)
    qz_ref[1] = jnp.where(top, zero, qt)
    bufa, bufb = (sa_ref, ma_ref), (sb_ref, mb_ref)

    half = tq // 2
    assert half == sub
    colmax = lambda v: jnp.max(v, axis=0, keepdims=True)

    def scores(buf, pair, kind):
        s_ref, bm_ref = buf
        j, i = pair
        kb = k_ref[pl.ds(pl.multiple_of(j * tk, tk), tk), :]
        qsl = pl.ds(pl.multiple_of(i * tq, tq), tq)
        for c in range(2):
            qc = qz_ref[c, :, qsl]
            if kind == "far":
                s = _dot(kb, qc)
                s_ref[c] = s
                bm_ref[c] = colmax(s)
                continue
            top = _dot(kb[:half], qc)
            if kind == "off1":
                bot = _dot(kb[half:], qc)
                bot_l = bot[:, :half] + bias_ref[1, half:, :half]
                bot_r = bot[:, half:]
                s_ref[c, half:, :half] = bot_l
                m_l = jnp.maximum(colmax(top[:, :half]), colmax(bot_l))
            else:
                top = top + bias_ref[0, :half, :]
                bot_r = _dot(kb[half:], qc[:, half:]) + bias_ref[0, half:, half:]
                m_l = colmax(top[:, :half])
            s_ref[c, :half, :] = top
            s_ref[c, half:, half:] = bot_r
            m_r = jnp.maximum(colmax(top[:, half:]), colmax(bot_r))
            bm_ref[c] = jnp.concatenate([m_l, m_r], axis=1)

    def update(buf, pair, diag=False, first=False):
        s_ref, bm_ref = buf
        j, i = pair
        vb = vt_ref[:, pl.ds(pl.multiple_of(j * tk, tk), tk)]
        q0 = pl.multiple_of(i * tq, tq)
        qsl = pl.ds(q0, tq)
        for c in range(2):
            if first:
                m_new = bm_ref[c]
            else:
                m_old = m_ref[c, :, qsl]
                m_new = jnp.maximum(m_old, bm_ref[c])
                alpha = jnp.exp2(m_old - m_new)
            m_ref[c, :, qsl] = m_new
            for n in range(tq // sub):
                qs = slice(n * sub, (n + 1) * sub)
                asl = pl.ds(pl.multiple_of(q0 + n * sub, sub), sub)
                pv = None
                for kt in range(tk // sub):
                    if diag and kt > n:
                        continue
                    ks = slice(kt * sub, (kt + 1) * sub)
                    p = jnp.exp2(s_ref[c, ks, qs] - m_new[:, qs])
                    d = _dot(vb[:, ks], p.astype(BF16))
                    pv = d if pv is None else pv + d
                acc_ref[c, :, asl] = pv if first else alpha[:, qs] * acc_ref[c, :, asl] + pv

    def finish(i):
        qsl = pl.ds(pl.multiple_of(i * tq, tq), tq)
        o1 = acc_ref[0, :V_DIM, qsl] * (1.0 / acc_ref[0, V_DIM:V_DIM + 1, qsl])
        o2 = acc_ref[1, :V_DIM, qsl] * (1.0 / acc_ref[1, V_DIM:V_DIM + 1, qsl])
        o = o1 - lam_ref[...] * o2
        o = o * lax.rsqrt(jnp.mean(o * o, axis=0, keepdims=True) + EPS) * (sg_ref[...] * out_scale)
        o_ref[qsl, :] = o.T.astype(BF16)

    far = lambda s: (ftab_ref[0, s], ftab_ref[1, s])
    near = lambda s: (ntab_ref[0, s], ntab_ref[1, s])

    scores(bufa, far(0), "far")

    def far_two(t, first=False):
        scores(bufb, far(2 * t + 1), "far")
        update(bufa, far(2 * t), first=first)
        scores(bufa, far(2 * t + 2), "far")
        update(bufb, far(2 * t + 1), first=first)

    def near_two(t, first=False):
        scores(bufa, near(2 * t + 1), "off1")
        update(bufb, near(2 * t), diag=True, first=first)
        finish(t)
        scores(bufb, near(2 * t + 2), "diag")
        update(bufa, near(2 * t + 1), first=first)

    def run(two, start, stop):
        def body(t, carry):
            two(start + 2 * t)
            two(start + 2 * t + 1)
            return carry
        lax.fori_loop(0, (stop - start) // 2, body, 0)
        if (stop - start) % 2:
            two(stop - 1)

    nq = qt_ref.shape[-1] // tq
    n_first = (nq - 2) // 2
    for t in range(n_first):
        far_two(t, first=True)
    run(far_two, n_first, (n_far - 1) // 2)
    scores(bufb, near(0), "diag")
    update(bufa, far(n_far - 1))
    near_two(0, first=True)
    run(near_two, 1, (n_near - 1) // 2)
    update(bufb, near(n_near - 1), diag=True)
    finish((n_near - 1) // 2)


def _attn_flat_call(lam, qt, k, vt, bias, subln_g, out_scale):
    bsz, n_heads, _, seq = qt.shape
    tq = ATTN_TQ
    nq = seq // tq
    assert seq % tq == 0 and nq % 4 == 0 and bias.shape[1:] == (2, tq, tq)
    ftab, ntab = _attn_order(nq)
    n_far, n_near = ftab.shape[1], ntab.shape[1]
    assert n_far % 2 == 1 and n_near == 2 * nq - 1
    smem = pl.BlockSpec(memory_space=pltpu.SMEM)
    per_head = lambda *blk: pl.BlockSpec((None, None) + blk, lambda b, hh: (b, hh, 0, 0))
    return pl.pallas_call(
        functools.partial(_attn_flat_kernel, out_scale=out_scale, tq=tq, n_far=n_far, n_near=n_near),
        grid=(bsz, n_heads),
        in_specs=[smem, smem, _const_spec(lam.shape),
                  per_head(V_DIM, seq), per_head(seq, V_DIM), per_head(V_AUG, seq),
                  pl.BlockSpec((None,) + bias.shape[1:], lambda b, hh: (hh, 0, 0, 0)),
                  _const_spec(subln_g.shape)],
        out_specs=pl.BlockSpec((None, seq, V_DIM), lambda b, hh: (b, 0, hh)),
        out_shape=jax.ShapeDtypeStruct((bsz, seq, n_heads * V_DIM), BF16),
        scratch_shapes=[pltpu.VMEM((2, V_DIM, seq), BF16),
                        pltpu.VMEM((2, 1, seq), F32),
                        pltpu.VMEM((2, V_AUG, seq), F32),
                        pltpu.VMEM((2, tq, tq), F32), pltpu.VMEM((2, tq, tq), F32),
                        pltpu.VMEM((2, 1, tq), F32), pltpu.VMEM((2, 1, tq), F32)],
        compiler_params=pltpu.CompilerParams(dimension_semantics=("parallel", "parallel"),
                                             vmem_limit_bytes=VMEM_LIMIT_BYTES),
        name="diff_attention",
    )(jnp.asarray(ftab), jnp.asarray(ntab), lam, qt, k, vt, bias, subln_g)


def _final_kernel(o_ref, zg_ref, x_ref, p_ref, wout_ref, wgate_ref, wproj_ref, out_ref):
    g = (o_ref[...].astype(F32) * zg_ref[...].astype(F32)).astype(BF16)
    x1 = x_ref[...] + _dot(g, wout_ref[...])
    out_ref[...] = _ple(x1, p_ref[...], wgate_ref, wproj_ref)


def _final_call(o, zg, x, p, p_layer, wout, wgate, wproj):
    bsz, seq, dm = x.shape
    tm = FINAL_ROW_TILE
    assert seq % tm == 0
    row = lambda b, i: (b, i, 0)
    return pl.pallas_call(
        _final_kernel,
        grid=(bsz, seq // tm),
        in_specs=[pl.BlockSpec((None, tm, o.shape[-1]), row), pl.BlockSpec((None, tm, zg.shape[-1]), row),
                  pl.BlockSpec((None, tm, dm), row), pl.BlockSpec((None, None, tm, p.shape[-1]), lambda b, i: (p_layer, b, i, 0)),
                  _const_spec(wout.shape), _const_spec(wgate.shape), _const_spec(wproj.shape)],
        out_specs=pl.BlockSpec((None, tm, dm), row),
        out_shape=jax.ShapeDtypeStruct((bsz, seq, dm), F32),
        compiler_params=pltpu.CompilerParams(dimension_semantics=("parallel", "parallel"),
                                             vmem_limit_bytes=VMEM_LIMIT_BYTES),
        name="final_rowwise",
    )(o, zg, x, p, wout, wgate, wproj)


def _ssm_params(lam_re, lam_im, log_dt, b_re, b_im, c_re, c_im):
    lr, li = lam_re.astype(F32), lam_im.astype(F32)
    dt = jnp.exp(log_dt.astype(F32))[:, None]
    mag = jnp.exp(lr * dt)
    ab_re, ab_im = mag * jnp.cos(li * dt), mag * jnp.sin(li * dt)
    den = lr * lr + li * li
    nr, ni = ab_re - 1.0, ab_im
    f_re = ((nr * lr + ni * li) / den)[..., None]
    f_im = ((ni * lr - nr * li) / den)[..., None]
    br, bi = b_re.astype(F32), b_im.astype(F32)
    bb_re, bb_im = f_re * br - f_im * bi, f_re * bi + f_im * br
    n_tiles = lam_re.shape[0] // GROUPS_PER_TILE
    eye = jnp.eye(GROUPS_PER_TILE, dtype=F32)

    def pack_b(m):
        m = m.reshape(n_tiles, GROUPS_PER_TILE, SSM_STATE, SSM_GROUP)
        return jnp.einsum('jgpc,gh->jgchp', m, eye).reshape(n_tiles, LANES, STATE_LANES)

    def pack_c(m):
        m = m.reshape(n_tiles, GROUPS_PER_TILE, SSM_GROUP, SSM_STATE)
        return jnp.einsum('jgcp,gh->jgphc', m, eye).reshape(n_tiles, STATE_LANES, LANES)

    bbig = jnp.concatenate([pack_b(bb_re), pack_b(bb_im)], axis=2).astype(BF16)
    cbig = jnp.concatenate([pack_c(c_re.astype(F32)), pack_c(-c_im.astype(F32))], axis=1).astype(BF16)
    a_re = ab_re.reshape(n_tiles, 1, STATE_LANES)
    a_im = ab_im.reshape(n_tiles, 1, STATE_LANES)
    return bbig, cbig, a_re, a_im


def _bias_ids(tk, tq, seq):
    kk = jnp.arange(tk, dtype=jnp.int32)[:, None]
    qq = jnp.arange(tq, dtype=jnp.int32)[None, :]
    rel0 = qq - kk
    rel1 = rel0 + tk
    ids0 = jnp.where(rel0 >= 0, _bucket_of(rel0), -1)
    far = np.arange(tk // 2 + 1, max(seq, tk + 2), dtype=np.float32)
    far = REL_MAX_EXACT + (np.log(far / REL_MAX_EXACT) / math.log(REL_MAX_DIST / REL_MAX_EXACT)
                           * (REL_BUCKETS - REL_MAX_EXACT)).astype(np.int32)
    assert far.min() > REL_BUCKETS
    return jnp.stack([ids0, _bucket_of(rel1)])


def kernel(x, p, a_norm_g, a_w_in, a_lam_re, a_lam_im, a_log_dt, a_b_re, a_b_im, a_c_re, a_c_im, a_d,
           a_w_glu, a_w_out, kv_norm_g, w_k, w_v, k_norm_g, b_norm_g, b_w_in, b_q_norm_g, b_lam_q1,
           b_lam_k1, b_lam_q2, b_lam_k2, b_subln_g, b_w_out, rel_bias, ple_w_proj, ple_w_gate):
    assert a_norm_g.shape[0] == 1 and b_norm_g.shape[0] == 1 and p.shape[0] == 2
    bsz, seq, dm = x.shape
    aw = w_k.shape[-1]
    n_half = aw // HEAD_DIM
    row = lambda v: v.reshape(1, -1).astype(F32)

    bbig, cbig, a_re, a_im = _ssm_params(a_lam_re[0], a_lam_im[0], a_log_dt[0], a_b_re[0], a_b_im[0],
                                         a_c_re[0], a_c_im[0])
    y, zg = _ssm_call(x, row(a_norm_g[0]), a_w_in[0].astype(BF16), bbig, cbig, a_re, a_im, row(a_d[0]))

    lane_group = jnp.arange(aw, dtype=jnp.int32) // HEAD_DIM
    ones = (lane_group[:, None] == jnp.arange(LANES, dtype=jnp.int32)[None, :]).astype(BF16)
    expand = jnp.concatenate([ones.T, ones.T], axis=0)
    q_scale = HEAD_DIM ** -0.5 * LOG2E
    x2, qt, k, vt, zg1 = _mid_call(
        y, zg, x, p, 0, a_w_glu[0].astype(BF16), a_w_out[0].astype(BF16),
        ple_w_gate[0].astype(BF16), ple_w_proj[0].astype(BF16),
        row(kv_norm_g), w_k.astype(BF16), w_v.astype(BF16), row(jnp.tile(k_norm_g, n_half)),
        row(b_norm_g[0]), b_w_in[0].astype(BF16), row(jnp.tile(b_q_norm_g[0], n_half) * q_scale),
        ones, expand)

    layer_idx = 1
    lam_init = 0.8 - 0.6 * math.exp(-0.3 * layer_idx)
    lam = (jnp.exp(jnp.sum(b_lam_q1[0].astype(F32) * b_lam_k1[0].astype(F32)))
           - jnp.exp(jnp.sum(b_lam_q2[0].astype(F32) * b_lam_k2[0].astype(F32))) + lam_init).reshape(1, 1)
    bias = _bias_call(rel_bias.astype(F32), _bias_ids(ATTN_TQ, ATTN_TQ, seq))
    o = _attn_flat_call(lam, qt, k, vt, bias, b_subln_g[0].astype(F32).reshape(V_DIM, 1), 1.0 - lam_init)
    return _final_call(o, zg1, x2, p, 1, b_w_out[0].astype(BF16), ple_w_gate[1].astype(BF16),
                       ple_w_proj[1].astype(BF16))
```

```python
import functools
import math

import jax
import jax.numpy as jnp
import numpy as np
from jax import lax
from jax.experimental import pallas as pl
from jax.experimental.pallas import tpu as pltpu

F32 = jnp.float32
BF16 = jnp.bfloat16

SUBLANES = 8
LANES = 128
VMEM_LIMIT_BYTES = 56 * 1024 * 1024

EPS = 1e-6
NEG_INF = -1e30
LOG2E = 1.4426950408889634

SSM_GROUP = 16
SSM_STATE = 64
GROUPS_PER_TILE = LANES // SSM_GROUP
STATE_LANES = GROUPS_PER_TILE * SSM_STATE
HEAD_DIM = 64
V_DIM = 2 * HEAD_DIM
V_AUG = V_DIM + 2 * SUBLANES
REL_BUCKETS = 32
REL_MAX_EXACT = REL_BUCKETS // 2
REL_MAX_DIST = 128

SSM_TIME_CHUNK = 64
ROW_TILE = 256
FINAL_ROW_TILE = 512
ATTN_TQ = 512


def _rms(x, g=None):
    y = x * lax.rsqrt(jnp.mean(x * x, axis=-1, keepdims=True) + EPS)
    return y if g is None else y * g


def _dot(a, b):
    return jnp.dot(a, b, preferred_element_type=F32)


def _sigmoid(x):
    return 1.0 / (1.0 + jnp.exp(-x))


def _gelu_tanh(x):
    c = math.sqrt(2.0 / math.pi)
    return 0.5 * x * (1.0 + jnp.tanh(c * (x + 0.044715 * (x * x * x))))


def _const_spec(shape):
    nd = len(shape)
    return pl.BlockSpec(shape, lambda *_: (0,) * nd, pipeline_mode=pl.Buffered(1))


def _ssm_kernel(x_ref, g_ref, win_ref, bbig_ref, cbig_ref, are_ref, aim_ref, d_ref,
                y_ref, zg_ref,
                hr_ref, hi_ref, us_ref, xs_ref, hs_ref, ys_ref, *, pitch):
    bsz, tc, dm = x_ref.shape
    e = y_ref.shape[-1]
    n_tiles = e // LANES
    rows = bsz * tc

    @pl.when(pl.program_id(0) == 0)
    def _():
        hr_ref[...] = jnp.zeros_like(hr_ref)
        hi_ref[...] = jnp.zeros_like(hi_ref)

    x = x_ref[...].reshape(rows, dm)
    h = _rms(x, g_ref[...]).astype(BF16)
    z = _dot(h, win_ref[:, e:])
    zg_ref[...] = (z * _sigmoid(z)).astype(BF16).reshape(bsz, tc, e)
    u = _dot(h, win_ref[:, :e])
    for j in range(n_tiles):
        for b in range(bsz):
            us_ref[j, b * pitch:b * pitch + tc, :] = u[b * tc:(b + 1) * tc, j * LANES:(j + 1) * LANES]

    for j in range(n_tiles):
        xs, hs, ys = xs_ref.at[j % 2], hs_ref.at[j % 2], ys_ref.at[j % 2]
        u_tb = jnp.concatenate(
            [us_ref[j, pl.ds(t, bsz, stride=pitch), :] for t in range(tc)], axis=0)
        xs[...] = _dot(u_tb.astype(BF16), bbig_ref[j])
        ar = jnp.broadcast_to(are_ref[j], (bsz, STATE_LANES))
        ai = jnp.broadcast_to(aim_ref[j], (bsz, STATE_LANES))
        sr, si = hr_ref[j], hi_ref[j]
        for t in range(tc):
            r0 = t * bsz
            xr = xs[r0:r0 + bsz, :STATE_LANES]
            xi = xs[r0:r0 + bsz, STATE_LANES:]
            sr, si = ar * sr - ai * si + xr, ar * si + ai * sr + xi
            hs[r0:r0 + bsz, :STATE_LANES] = sr
            hs[r0:r0 + bsz, STATE_LANES:] = si
        hr_ref[j] = sr
        hi_ref[j] = si

        y_tb = _dot(hs[...].astype(BF16), cbig_ref[j])
        y_tb = y_tb + d_ref[:, j * LANES:(j + 1) * LANES] * u_tb
        ys[...] = _gelu_tanh(y_tb)
        for b in range(bsz):
            y_ref[b, :, j * LANES:(j + 1) * LANES] = ys[pl.ds(b, tc, stride=bsz), :].astype(BF16)


def _ssm_call(x, norm_g, w_in, bbig, cbig, a_re, a_im, d):
    bsz, seq, dm = x.shape
    e = d.shape[-1]
    tc = SSM_TIME_CHUNK
    assert bsz == SUBLANES and seq % tc == 0 and e % LANES == 0 and tc % (2 * SUBLANES) == 0
    pitch = tc + SUBLANES
    n_tiles = e // LANES
    rows = bsz * tc
    out_shape = [jax.ShapeDtypeStruct((bsz, seq, e), BF16)] * 2
    blk = lambda i: (0, i, 0)
    return pl.pallas_call(
        functools.partial(_ssm_kernel, pitch=pitch),
        grid=(seq // tc,),
        in_specs=[pl.BlockSpec((bsz, tc, dm), blk),
                  _const_spec(norm_g.shape), _const_spec(w_in.shape), _const_spec(bbig.shape),
                  _const_spec(cbig.shape), _const_spec(a_re.shape), _const_spec(a_im.shape),
                  _const_spec(d.shape)],
        out_specs=[pl.BlockSpec((bsz, tc, e), blk)] * 2,
        out_shape=out_shape,
        scratch_shapes=[pltpu.VMEM((n_tiles, bsz, STATE_LANES), F32),
                        pltpu.VMEM((n_tiles, bsz, STATE_LANES), F32),
                        pltpu.VMEM((n_tiles, bsz * pitch, LANES), F32),
                        pltpu.VMEM((2, rows, 2 * STATE_LANES), F32),
                        pltpu.VMEM((2, rows, 2 * STATE_LANES), F32),
                        pltpu.VMEM((2, rows, LANES), F32)],
        compiler_params=pltpu.CompilerParams(dimension_semantics=("arbitrary",),
                                             vmem_limit_bytes=VMEM_LIMIT_BYTES),
        name="ssm_front",
    )(x, norm_g, w_in, bbig, cbig, a_re, a_im, d)


def _head_norm(t, gain_row, ones_ref, expand_ref):
    ss = _dot((t * t).astype(BF16), ones_ref[...])
    inv = lax.rsqrt(ss * (1.0 / HEAD_DIM) + EPS)
    hi = inv.astype(BF16)
    lo = (inv - hi.astype(F32)).astype(BF16)
    inv_full = _dot(jnp.concatenate([hi, lo], axis=1), expand_ref[...])
    return t * inv_full * gain_row


def _ple(x1, p, wgate_ref, wproj_ref):
    gate = _sigmoid(_dot(_rms(x1).astype(BF16), wgate_ref[...]))
    return x1 + gate * _dot(p.astype(BF16), wproj_ref[...])


def _mid_kernel(y_ref, zg_ref, x_ref, p_ref,
                wglu_ref, wout_ref, wgate_ref, wproj_ref,
                kvg_ref, wk_ref, wv_ref, kng_ref,
                bng_ref, bwin_ref, qng_ref, ones_ref, expand_ref,
                x2_ref, qz_ref, k_ref, vt_ref, zg1_ref):
    e = y_ref.shape[-1]
    n_heads = k_ref.shape[0]
    aw = n_heads * V_DIM
    gab = _dot(y_ref[...], wglu_ref[...])
    gl = gab[:, :e] * _sigmoid(gab[:, e:]) * zg_ref[...].astype(F32)
    x1 = x_ref[...] + _dot(gl.astype(BF16), wout_ref[...])
    x2 = _ple(x1, p_ref[...], wgate_ref, wproj_ref)
    x2_ref[...] = x2

    hk = _rms(x2, kvg_ref[...]).astype(BF16)
    k = _head_norm(_dot(hk, wk_ref[...]), kng_ref[...], ones_ref, expand_ref)
    v = _dot(hk, wv_ref[...])
    hq = _rms(x2, bng_ref[...]).astype(BF16)
    q = _dot(hq, bwin_ref[:, :aw])
    z1 = _dot(hq, bwin_ref[:, aw:])
    zg1_ref[...] = (z1 * _sigmoid(z1)).astype(BF16)
    pad_rows = (lax.broadcasted_iota(jnp.int32, (V_AUG - V_DIM, v.shape[0]), 0) == 0).astype(BF16)
    for hh in range(n_heads):
        sl = slice(hh * V_DIM, (hh + 1) * V_DIM)
        k_ref[hh] = k[:, sl].astype(BF16)
        qt = q[:, sl].T
        for c in range(2):
            part = qt[c * HEAD_DIM:(c + 1) * HEAD_DIM]
            part = part * lax.rsqrt(jnp.mean(part * part, axis=0, keepdims=True) + EPS) * qng_ref[...]
            qz_ref[hh, c, c * HEAD_DIM:(c + 1) * HEAD_DIM, :] = part.astype(BF16)
            qz_ref[hh, c, (1 - c) * HEAD_DIM:(2 - c) * HEAD_DIM, :] = jnp.zeros_like(part, dtype=BF16)
        vt_ref[hh, :V_DIM, :] = v[:, sl].T.astype(BF16)
        vt_ref[hh, V_DIM:, :] = pad_rows


def _mid_call(y, zg, x, p, p_layer, wglu, wout, wgate, wproj, kvg, wk, wv, kng, bng, bwin, qng, ones, expand):
    bsz, seq, dm = x.shape
    e = y.shape[-1]
    aw = wk.shape[-1]
    n_heads = aw // V_DIM
    tm = ROW_TILE
    assert seq % tm == 0
    row = lambda b, i: (b, i, 0)
    out_shape = [jax.ShapeDtypeStruct((bsz, seq, dm), F32),
                 jax.ShapeDtypeStruct((bsz, n_heads, 2, V_DIM, seq), BF16),
                 jax.ShapeDtypeStruct((bsz, n_heads, seq, V_DIM), BF16),
                 jax.ShapeDtypeStruct((bsz, n_heads, V_AUG, seq), BF16),
                 jax.ShapeDtypeStruct((bsz, seq, aw), BF16)]
    t_spec = lambda rows: pl.BlockSpec((None, n_heads, rows, tm), lambda b, i: (b, 0, 0, i))
    weights = (wglu, wout, wgate, wproj, kvg, wk, wv, kng, bng, bwin, qng, ones, expand)
    return pl.pallas_call(
        _mid_kernel,
        grid=(bsz, seq // tm),
        in_specs=[pl.BlockSpec((None, tm, e), row), pl.BlockSpec((None, tm, e), row),
                  pl.BlockSpec((None, tm, dm), row), pl.BlockSpec((None, None, tm, p.shape[-1]), lambda b, i: (p_layer, b, i, 0))]
                 + [_const_spec(w.shape) for w in weights],
        out_specs=[pl.BlockSpec((None, tm, dm), row),
                   pl.BlockSpec((None, n_heads, 2, V_DIM, tm), lambda b, i: (b, 0, 0, 0, i)),
                   pl.BlockSpec((None, n_heads, tm, V_DIM), lambda b, i: (b, 0, i, 0)), t_spec(V_AUG),
                   pl.BlockSpec((None, tm, aw), row)],
        out_shape=out_shape,
        compiler_params=pltpu.CompilerParams(dimension_semantics=("parallel", "parallel"),
                                             vmem_limit_bytes=VMEM_LIMIT_BYTES),
        name="mid_rowwise",
    )(y, zg, x, p, *weights)


def _bucket_of(rel):
    n = jnp.maximum(rel, 0)
    nf = jnp.maximum(n, 1).astype(F32)
    large = REL_MAX_EXACT + jnp.trunc(jnp.log(nf / REL_MAX_EXACT) / math.log(REL_MAX_DIST / REL_MAX_EXACT)
                                      * (REL_BUCKETS - REL_MAX_EXACT)).astype(jnp.int32)
    large = jnp.minimum(large, REL_BUCKETS - 1)
    return jnp.where(n < REL_MAX_EXACT, n, large)


def _bias_kernel(table_ref, ids_ref, out_ref):
    hh = pl.program_id(0)
    ids = ids_ref[...]
    far = table_ref[REL_BUCKETS - 1, hh]
    acc = jnp.full(ids.shape, NEG_INF, F32)
    for bkt in range(REL_BUCKETS):
        acc = jnp.where(ids == bkt, (table_ref[bkt, hh] - far) * LOG2E, acc)
    out_ref[...] = acc


def _bias_call(table, ids):
    n_heads = table.shape[1]
    return pl.pallas_call(
        _bias_kernel,
        grid=(n_heads,),
        in_specs=[pl.BlockSpec(memory_space=pltpu.SMEM), _const_spec(ids.shape)],
        out_specs=pl.BlockSpec((None,) + ids.shape, lambda hh: (hh, 0, 0, 0)),
        out_shape=jax.ShapeDtypeStruct((n_heads,) + ids.shape, F32),
        name="rel_bias_tiles",
    )(table, ids)


def _attn_order(nq):
    far = [(j, i) for j in range(nq) for i in range(j + 2, nq)]
    near = [(0, 0)] + [e for i in range(1, nq) for e in ((i - 1, i), (i, i))]
    return np.asarray(far, np.int32).T.copy(), np.asarray(near, np.int32).T.copy()


def _attn_flat_kernel(ftab_ref, ntab_ref, lam_ref, qz_ref, k_ref, vt_ref, bias_ref, sg_ref, o_ref,
                      m_ref, acc_ref, sa_ref, sb_ref, ma_ref, mb_ref,
                      *, out_scale, tq, n_far, n_near):
    tk = tq
    sub = 2 * LANES
    bufa, bufb = (sa_ref, ma_ref), (sb_ref, mb_ref)

    half = tq // 2
    assert half == sub
    colmax = lambda v: jnp.max(v, axis=0, keepdims=True)

    def scores(buf, pair, kind):
        s_ref, bm_ref = buf
        j, i = pair
        kb = k_ref[pl.ds(pl.multiple_of(j * tk, tk), tk), :]
        qsl = pl.ds(pl.multiple_of(i * tq, tq), tq)
        for c in range(2):
            qc = qz_ref[c, :, qsl]
            if kind == "far":
                s = _dot(kb, qc)
                s_ref[c] = s
                bm_ref[c] = colmax(s)
                continue
            top = _dot(kb[:half], qc)
            if kind == "off1":
                bot = _dot(kb[half:], qc)
                bot_l = bot[:, :half] + bias_ref[1, half:, :half]
                bot_r = bot[:, half:]
                s_ref[c, half:, :half] = bot_l
                m_l = jnp.maximum(colmax(top[:, :half]), colmax(bot_l))
            else:
                top = top + bias_ref[0, :half, :]
                bot_r = _dot(kb[half:], qc[:, half:]) + bias_ref[0, half:, half:]
                m_l = colmax(top[:, :half])
            s_ref[c, :half, :] = top
            s_ref[c, half:, half:] = bot_r
            m_r = jnp.maximum(colmax(top[:, half:]), colmax(bot_r))
            bm_ref[c] = jnp.concatenate([m_l, m_r], axis=1)

    def update(buf, pair, diag=False, first=False):
        s_ref, bm_ref = buf
        j, i = pair
        vb = vt_ref[:, pl.ds(pl.multiple_of(j * tk, tk), tk)]
        q0 = pl.multiple_of(i * tq, tq)
        qsl = pl.ds(q0, tq)
        for c in range(2):
            if first:
                m_new = bm_ref[c]
            else:
                m_old = m_ref[c, :, qsl]
                m_new = jnp.maximum(m_old, bm_ref[c])
                alpha = jnp.exp2(m_old - m_new)
            m_ref[c, :, qsl] = m_new
            for n in range(tq // sub):
                qs = slice(n * sub, (n + 1) * sub)
                asl = pl.ds(pl.multiple_of(q0 + n * sub, sub), sub)
                pv = None
                for kt in range(tk // sub):
                    if diag and kt > n:
                        continue
                    ks = slice(kt * sub, (kt + 1) * sub)
                    p = jnp.exp2(s_ref[c, ks, qs] - m_new[:, qs])
                    d = _dot(vb[:, ks], p.astype(BF16))
                    pv = d if pv is None else pv + d
                acc_ref[c, :, asl] = pv if first else alpha[:, qs] * acc_ref[c, :, asl] + pv

    def finish(i):
        qsl = pl.ds(pl.multiple_of(i * tq, tq), tq)
        o1 = acc_ref[0, :V_DIM, qsl] * (1.0 / acc_ref[0, V_DIM:V_DIM + 1, qsl])
        o2 = acc_ref[1, :V_DIM, qsl] * (1.0 / acc_ref[1, V_DIM:V_DIM + 1, qsl])
        o = o1 - lam_ref[...] * o2
        o = o * lax.rsqrt(jnp.mean(o * o, axis=0, keepdims=True) + EPS) * (sg_ref[...] * out_scale)
        o_ref[qsl, :] = o.T.astype(BF16)

    far = lambda s: (ftab_ref[0, s], ftab_ref[1, s])
    near = lambda s: (ntab_ref[0, s], ntab_ref[1, s])

    scores(bufa, far(0), "far")

    def far_two(t, first=False):
        scores(bufb, far(2 * t + 1), "far")
        update(bufa, far(2 * t), first=first)
        scores(bufa, far(2 * t + 2), "far")
        update(bufb, far(2 * t + 1), first=first)

    def near_two(t, first=False):
        scores(bufa, near(2 * t + 1), "off1")
        update(bufb, near(2 * t), diag=True, first=first)
        finish(t)
        scores(bufb, near(2 * t + 2), "diag")
        update(bufa, near(2 * t + 1), first=first)

    def run(two, start, stop):
        def body(t, carry):
            two(start + 2 * t)
            two(start + 2 * t + 1)
            return carry
        lax.fori_loop(0, (stop - start) // 2, body, 0)
        if (stop - start) % 2:
            two(stop - 1)

    nq = qz_ref.shape[-1] // tq
    n_first = (nq - 2) // 2
    for t in range(n_first):
        far_two(t, first=True)
    run(far_two, n_first, (n_far - 1) // 2)
    scores(bufb, near(0), "diag")
    update(bufa, far(n_far - 1))
    near_two(0, first=True)
    run(near_two, 1, (n_near - 1) // 2)
    update(bufb, near(n_near - 1), diag=True)
    finish((n_near - 1) // 2)


def _attn_flat_call(lam, qz, k, vt, bias, subln_g, out_scale):
    bsz, n_heads, seq, _ = k.shape
    tq = ATTN_TQ
    nq = seq // tq
    assert seq % tq == 0 and nq % 4 == 0 and bias.shape[1:] == (2, tq, tq)
    ftab, ntab = _attn_order(nq)
    n_far, n_near = ftab.shape[1], ntab.shape[1]
    assert n_far % 2 == 1 and n_near == 2 * nq - 1
    smem = pl.BlockSpec(memory_space=pltpu.SMEM)
    per_head = lambda *blk: pl.BlockSpec((None, None) + blk, lambda b, hh: (b, hh, 0, 0))
    return pl.pallas_call(
        functools.partial(_attn_flat_kernel, out_scale=out_scale, tq=tq, n_far=n_far, n_near=n_near),
        grid=(bsz, n_heads),
        in_specs=[smem, smem, _const_spec(lam.shape),
                  pl.BlockSpec((None, None, 2, V_DIM, seq), lambda b, hh: (b, hh, 0, 0, 0)),
                  per_head(seq, V_DIM), per_head(V_AUG, seq),
                  pl.BlockSpec((None,) + bias.shape[1:], lambda b, hh: (hh, 0, 0, 0)),
                  _const_spec(subln_g.shape)],
        out_specs=pl.BlockSpec((None, seq, V_DIM), lambda b, hh: (b, 0, hh)),
        out_shape=jax.ShapeDtypeStruct((bsz, seq, n_heads * V_DIM), BF16),
        scratch_shapes=[pltpu.VMEM((2, 1, seq), F32),
                        pltpu.VMEM((2, V_AUG, seq), F32),
                        pltpu.VMEM((2, tq, tq), F32), pltpu.VMEM((2, tq, tq), F32),
                        pltpu.VMEM((2, 1, tq), F32), pltpu.VMEM((2, 1, tq), F32)],
        compiler_params=pltpu.CompilerParams(dimension_semantics=("parallel", "parallel"),
                                             vmem_limit_bytes=VMEM_LIMIT_BYTES),
        name="diff_attention",
    )(jnp.asarray(ftab), jnp.asarray(ntab), lam, qz, k, vt, bias, subln_g)


def _final_kernel(o_ref, zg_ref, x_ref, p_ref, wout_ref, wgate_ref, wproj_ref, out_ref):
    g = (o_ref[...].astype(F32) * zg_ref[...].astype(F32)).astype(BF16)
    x1 = x_ref[...] + _dot(g, wout_ref[...])
    out_ref[...] = _ple(x1, p_ref[...], wgate_ref, wproj_ref)


def _final_call(o, zg, x, p, p_layer, wout, wgate, wproj):
    bsz, seq, dm = x.shape
    tm = FINAL_ROW_TILE
    assert seq % tm == 0
    row = lambda b, i: (b, i, 0)
    return pl.pallas_call(
        _final_kernel,
        grid=(bsz, seq // tm),
        in_specs=[pl.BlockSpec((None, tm, o.shape[-1]), row), pl.BlockSpec((None, tm, zg.shape[-1]), row),
                  pl.BlockSpec((None, tm, dm), row), pl.BlockSpec((None, None, tm, p.shape[-1]), lambda b, i: (p_layer, b, i, 0)),
                  _const_spec(wout.shape), _const_spec(wgate.shape), _const_spec(wproj.shape)],
        out_specs=pl.BlockSpec((None, tm, dm), row),
        out_shape=jax.ShapeDtypeStruct((bsz, seq, dm), F32),
        compiler_params=pltpu.CompilerParams(dimension_semantics=("parallel", "parallel"),
                                             vmem_limit_bytes=VMEM_LIMIT_BYTES),
        name="final_rowwise",
    )(o, zg, x, p, wout, wgate, wproj)


def _ssm_params(lam_re, lam_im, log_dt, b_re, b_im, c_re, c_im):
    lr, li = lam_re.astype(F32), lam_im.astype(F32)
    dt = jnp.exp(log_dt.astype(F32))[:, None]
    mag = jnp.exp(lr * dt)
    ab_re, ab_im = mag * jnp.cos(li * dt), mag * jnp.sin(li * dt)
    den = lr * lr + li * li
    nr, ni = ab_re - 1.0, ab_im
    f_re = ((nr * lr + ni * li) / den)[..., None]
    f_im = ((ni * lr - nr * li) / den)[..., None]
    br, bi = b_re.astype(F32), b_im.astype(F32)
    bb_re, bb_im = f_re * br - f_im * bi, f_re * bi + f_im * br
    n_tiles = lam_re.shape[0] // GROUPS_PER_TILE
    eye = jnp.eye(GROUPS_PER_TILE, dtype=F32)

    def pack_b(m):
        m = m.reshape(n_tiles, GROUPS_PER_TILE, SSM_STATE, SSM_GROUP)
        return jnp.einsum('jgpc,gh->jgchp', m, eye).reshape(n_tiles, LANES, STATE_LANES)

    def pack_c(m):
        m = m.reshape(n_tiles, GROUPS_PER_TILE, SSM_GROUP, SSM_STATE)
        return jnp.einsum('jgcp,gh->jgphc', m, eye).reshape(n_tiles, STATE_LANES, LANES)

    bbig = jnp.concatenate([pack_b(bb_re), pack_b(bb_im)], axis=2).astype(BF16)
    cbig = jnp.concatenate([pack_c(c_re.astype(F32)), pack_c(-c_im.astype(F32))], axis=1).astype(BF16)
    a_re = ab_re.reshape(n_tiles, 1, STATE_LANES)
    a_im = ab_im.reshape(n_tiles, 1, STATE_LANES)
    return bbig, cbig, a_re, a_im


def _bias_ids(tk, tq, seq):
    kk = jnp.arange(tk, dtype=jnp.int32)[:, None]
    qq = jnp.arange(tq, dtype=jnp.int32)[None, :]
    rel0 = qq - kk
    rel1 = rel0 + tk
    ids0 = jnp.where(rel0 >= 0, _bucket_of(rel0), -1)
    far = np.arange(tk // 2 + 1, max(seq, tk + 2), dtype=np.float32)
    far = REL_MAX_EXACT + (np.log(far / REL_MAX_EXACT) / math.log(REL_MAX_DIST / REL_MAX_EXACT)
                           * (REL_BUCKETS - REL_MAX_EXACT)).astype(np.int32)
    assert far.min() > REL_BUCKETS
    return jnp.stack([ids0, _bucket_of(rel1)])


def kernel(x, p, a_norm_g, a_w_in, a_lam_re, a_lam_im, a_log_dt, a_b_re, a_b_im, a_c_re, a_c_im, a_d,
           a_w_glu, a_w_out, kv_norm_g, w_k, w_v, k_norm_g, b_norm_g, b_w_in, b_q_norm_g, b_lam_q1,
           b_lam_k1, b_lam_q2, b_lam_k2, b_subln_g, b_w_out, rel_bias, ple_w_proj, ple_w_gate):
    assert a_norm_g.shape[0] == 1 and b_norm_g.shape[0] == 1 and p.shape[0] == 2
    bsz, seq, dm = x.shape
    aw = w_k.shape[-1]
    n_half = aw // HEAD_DIM
    row = lambda v: v.reshape(1, -1).astype(F32)

    bbig, cbig, a_re, a_im = _ssm_params(a_lam_re[0], a_lam_im[0], a_log_dt[0], a_b_re[0], a_b_im[0],
                                         a_c_re[0], a_c_im[0])
    y, zg = _ssm_call(x, row(a_norm_g[0]), a_w_in[0].astype(BF16), bbig, cbig, a_re, a_im, row(a_d[0]))

    lane_group = jnp.arange(aw, dtype=jnp.int32) // HEAD_DIM
    ones = (lane_group[:, None] == jnp.arange(LANES, dtype=jnp.int32)[None, :]).astype(BF16)
    expand = jnp.concatenate([ones.T, ones.T], axis=0)
    q_scale = HEAD_DIM ** -0.5 * LOG2E
    x2, qz, k, vt, zg1 = _mid_call(
        y, zg, x, p, 0, a_w_glu[0].astype(BF16), a_w_out[0].astype(BF16),
        ple_w_gate[0].astype(BF16), ple_w_proj[0].astype(BF16),
        row(kv_norm_g), w_k.astype(BF16), w_v.astype(BF16), row(jnp.tile(k_norm_g, n_half)),
        row(b_norm_g[0]), b_w_in[0].astype(BF16), (b_q_norm_g[0].astype(F32) * q_scale).reshape(HEAD_DIM, 1),
        ones, expand)

    layer_idx = 1
    lam_init = 0.8 - 0.6 * math.exp(-0.3 * layer_idx)
    lam = (jnp.exp(jnp.sum(b_lam_q1[0].astype(F32) * b_lam_k1[0].astype(F32)))
           - jnp.exp(jnp.sum(b_lam_q2[0].astype(F32) * b_lam_k2[0].astype(F32))) + lam_init).reshape(1, 1)
    bias = _bias_call(rel_bias.astype(F32), _bias_ids(ATTN_TQ, ATTN_TQ, seq))
    o = _attn_flat_call(lam, qz, k, vt, bias, b_subln_g[0].astype(F32).reshape(V_DIM, 1), 1.0 - lam_init)
    return _final_call(o, zg1, x2, p, 1, b_w_out[0].astype(BF16), ple_w_gate[1].astype(BF16),
                       ple_w_proj[1].astype(BF16))
```

```python
import functools
import math

import jax
import jax.numpy as jnp
import numpy as np
from jax import lax
from jax.experimental import pallas as pl
from jax.experimental.pallas import tpu as pltpu

F32 = jnp.float32
BF16 = jnp.bfloat16

SUBLANES = 8
LANES = 128
VMEM_LIMIT_BYTES = 56 * 1024 * 1024

EPS = 1e-6
NEG_INF = -1e30
LOG2E = 1.4426950408889634

SSM_GROUP = 16
SSM_STATE = 64
GROUPS_PER_TILE = LANES // SSM_GROUP
STATE_LANES = GROUPS_PER_TILE * SSM_STATE
HEAD_DIM = 64
V_DIM = 2 * HEAD_DIM
V_AUG = V_DIM + 2 * SUBLANES
REL_BUCKETS = 32
REL_MAX_EXACT = REL_BUCKETS // 2
REL_MAX_DIST = 128

SSM_TIME_CHUNK = 64
ROW_TILE = 256
FINAL_ROW_TILE = 512
ATTN_TQ = 512


def _rms(x, g=None):
    y = x * lax.rsqrt(jnp.mean(x * x, axis=-1, keepdims=True) + EPS)
    return y if g is None else y * g


def _dot(a, b):
    return jnp.dot(a, b, preferred_element_type=F32)


def _sigmoid(x):
    return 1.0 / (1.0 + jnp.exp(-x))


def _gelu_tanh(x):
    c = math.sqrt(2.0 / math.pi)
    return 0.5 * x * (1.0 + jnp.tanh(c * (x + 0.044715 * (x * x * x))))


def _const_spec(shape):
    nd = len(shape)
    return pl.BlockSpec(shape, lambda *_: (0,) * nd, pipeline_mode=pl.Buffered(1))


def _ssm_kernel(x_ref, g_ref, win_ref, bbig_ref, cbig_ref, are_ref, aim_ref, d_ref,
                y_ref, zg_ref,
                hr_ref, hi_ref, us_ref, xs_ref, hs_ref, ys_ref, *, pitch):
    bsz, tc, dm = x_ref.shape
    e = y_ref.shape[-1]
    n_tiles = e // LANES
    rows = bsz * tc

    @pl.when(pl.program_id(0) == 0)
    def _():
        hr_ref[...] = jnp.zeros_like(hr_ref)
        hi_ref[...] = jnp.zeros_like(hi_ref)

    x = x_ref[...].reshape(rows, dm)
    h = _rms(x, g_ref[...]).astype(BF16)
    z = _dot(h, win_ref[:, e:])
    zg_ref[...] = (z * _sigmoid(z)).astype(BF16).reshape(bsz, tc, e)
    u = _dot(h, win_ref[:, :e])
    for j in range(n_tiles):
        for b in range(bsz):
            us_ref[j, b * pitch:b * pitch + tc, :] = u[b * tc:(b + 1) * tc, j * LANES:(j + 1) * LANES]

    for j in range(n_tiles):
        xs, hs, ys = xs_ref.at[j % 2], hs_ref.at[j % 2], ys_ref.at[j % 2]
        u_tb = jnp.concatenate(
            [us_ref[j, pl.ds(t, bsz, stride=pitch), :] for t in range(tc)], axis=0)
        xs[...] = _dot(u_tb.astype(BF16), bbig_ref[j])
        ar = jnp.broadcast_to(are_ref[j], (bsz, STATE_LANES))
        ai = jnp.broadcast_to(aim_ref[j], (bsz, STATE_LANES))
        sr, si = hr_ref[j], hi_ref[j]
        for t in range(tc):
            r0 = t * bsz
            xr = xs[r0:r0 + bsz, :STATE_LANES]
            xi = xs[r0:r0 + bsz, STATE_LANES:]
            sr, si = ar * sr - ai * si + xr, ar * si + ai * sr + xi
            hs[r0:r0 + bsz, :STATE_LANES] = sr
            hs[r0:r0 + bsz, STATE_LANES:] = si
        hr_ref[j] = sr
        hi_ref[j] = si

        y_tb = _dot(hs[...].astype(BF16), cbig_ref[j])
        y_tb = y_tb + d_ref[:, j * LANES:(j + 1) * LANES] * u_tb
        ys[...] = _gelu_tanh(y_tb)
        for b in range(bsz):
            y_ref[b, :, j * LANES:(j + 1) * LANES] = ys[pl.ds(b, tc, stride=bsz), :].astype(BF16)


def _ssm_call(x, norm_g, w_in, bbig, cbig, a_re, a_im, d):
    bsz, seq, dm = x.shape
    e = d.shape[-1]
    tc = SSM_TIME_CHUNK
    assert bsz == SUBLANES and seq % tc == 0 and e % LANES == 0 and tc % (2 * SUBLANES) == 0
    pitch = tc + SUBLANES
    n_tiles = e // LANES
    rows = bsz * tc
    out_shape = [jax.ShapeDtypeStruct((bsz, seq, e), BF16)] * 2
    blk = lambda i: (0, i, 0)
    return pl.pallas_call(
        functools.partial(_ssm_kernel, pitch=pitch),
        grid=(seq // tc,),
        in_specs=[pl.BlockSpec((bsz, tc, dm), blk),
                  _const_spec(norm_g.shape), _const_spec(w_in.shape), _const_spec(bbig.shape),
                  _const_spec(cbig.shape), _const_spec(a_re.shape), _const_spec(a_im.shape),
                  _const_spec(d.shape)],
        out_specs=[pl.BlockSpec((bsz, tc, e), blk)] * 2,
        out_shape=out_shape,
        scratch_shapes=[pltpu.VMEM((n_tiles, bsz, STATE_LANES), F32),
                        pltpu.VMEM((n_tiles, bsz, STATE_LANES), F32),
                        pltpu.VMEM((n_tiles, bsz * pitch, LANES), F32),
                        pltpu.VMEM((2, rows, 2 * STATE_LANES), F32),
                        pltpu.VMEM((2, rows, 2 * STATE_LANES), F32),
                        pltpu.VMEM((2, rows, LANES), F32)],
        compiler_params=pltpu.CompilerParams(dimension_semantics=("arbitrary",),
                                             vmem_limit_bytes=VMEM_LIMIT_BYTES),
        name="ssm_front",
    )(x, norm_g, w_in, bbig, cbig, a_re, a_im, d)


def _half_head_norm_t(part, gain_col):
    return part * lax.rsqrt(jnp.mean(part * part, axis=0, keepdims=True) + EPS) * gain_col


def _ple(x1, p, wgate_ref, wproj_ref):
    gate = _sigmoid(_dot(_rms(x1).astype(BF16), wgate_ref[...]))
    return x1 + gate * _dot(p.astype(BF16), wproj_ref[...])


def _mid_kernel(y_ref, zg_ref, x_ref, p_ref,
                wglu_ref, wout_ref, wgate_ref, wproj_ref,
                kvg_ref, wk_ref, wv_ref, kng_ref,
                bng_ref, bwin_ref, qng_ref,
                x2_ref, qz_ref, k_ref, vt_ref, zg1_ref):
    e = y_ref.shape[-1]
    n_heads = k_ref.shape[0]
    aw = n_heads * V_DIM
    gab = _dot(y_ref[...], wglu_ref[...])
    gl = gab[:, :e] * _sigmoid(gab[:, e:]) * zg_ref[...].astype(F32)
    x1 = x_ref[...] + _dot(gl.astype(BF16), wout_ref[...])
    x2 = _ple(x1, p_ref[...], wgate_ref, wproj_ref)
    x2_ref[...] = x2

    hk = _rms(x2, kvg_ref[...]).astype(BF16)
    k = _dot(hk, wk_ref[...])
    v = _dot(hk, wv_ref[...])
    hq = _rms(x2, bng_ref[...]).astype(BF16)
    q = _dot(hq, bwin_ref[:, :aw])
    z1 = _dot(hq, bwin_ref[:, aw:])
    zg1_ref[...] = (z1 * _sigmoid(z1)).astype(BF16)
    pad_rows = (lax.broadcasted_iota(jnp.int32, (V_AUG - V_DIM, v.shape[0]), 0) == 0).astype(BF16)
    for hh in range(n_heads):
        sl = slice(hh * V_DIM, (hh + 1) * V_DIM)
        kt = k[:, sl].T
        kn = jnp.concatenate([_half_head_norm_t(kt[c * HEAD_DIM:(c + 1) * HEAD_DIM], kng_ref[...])
                              for c in range(2)], axis=0)
        k_ref[hh] = kn.T.astype(BF16)
        qt = q[:, sl].T
        for c in range(2):
            part = _half_head_norm_t(qt[c * HEAD_DIM:(c + 1) * HEAD_DIM], qng_ref[...])
            qz_ref[hh, c, c * HEAD_DIM:(c + 1) * HEAD_DIM, :] = part.astype(BF16)
            qz_ref[hh, c, (1 - c) * HEAD_DIM:(2 - c) * HEAD_DIM, :] = jnp.zeros_like(part, dtype=BF16)
        vt_ref[hh, :V_DIM, :] = v[:, sl].T.astype(BF16)
        vt_ref[hh, V_DIM:, :] = pad_rows


def _mid_call(y, zg, x, p, p_layer, wglu, wout, wgate, wproj, kvg, wk, wv, kng, bng, bwin, qng):
    bsz, seq, dm = x.shape
    e = y.shape[-1]
    aw = wk.shape[-1]
    n_heads = aw // V_DIM
    tm = ROW_TILE
    assert seq % tm == 0
    row = lambda b, i: (b, i, 0)
    out_shape = [jax.ShapeDtypeStruct((bsz, seq, dm), F32),
                 jax.ShapeDtypeStruct((bsz, n_heads, 2, V_DIM, seq), BF16),
                 jax.ShapeDtypeStruct((bsz, n_heads, seq, V_DIM), BF16),
                 jax.ShapeDtypeStruct((bsz, n_heads, V_AUG, seq), BF16),
                 jax.ShapeDtypeStruct((bsz, seq, aw), BF16)]
    t_spec = lambda rows: pl.BlockSpec((None, n_heads, rows, tm), lambda b, i: (b, 0, 0, i))
    weights = (wglu, wout, wgate, wproj, kvg, wk, wv, kng, bng, bwin, qng)
    return pl.pallas_call(
        _mid_kernel,
        grid=(bsz, seq // tm),
        in_specs=[pl.BlockSpec((None, tm, e), row), pl.BlockSpec((None, tm, e), row),
                  pl.BlockSpec((None, tm, dm), row), pl.BlockSpec((None, None, tm, p.shape[-1]), lambda b, i: (p_layer, b, i, 0))]
                 + [_const_spec(w.shape) for w in weights],
        out_specs=[pl.BlockSpec((None, tm, dm), row),
                   pl.BlockSpec((None, n_heads, 2, V_DIM, tm), lambda b, i: (b, 0, 0, 0, i)),
                   pl.BlockSpec((None, n_heads, tm, V_DIM), lambda b, i: (b, 0, i, 0)), t_spec(V_AUG),
                   pl.BlockSpec((None, tm, aw), row)],
        out_shape=out_shape,
        compiler_params=pltpu.CompilerParams(dimension_semantics=("parallel", "parallel"),
                                             vmem_limit_bytes=VMEM_LIMIT_BYTES),
        name="mid_rowwise",
    )(y, zg, x, p, *weights)


def _bucket_of(rel):
    n = jnp.maximum(rel, 0)
    nf = jnp.maximum(n, 1).astype(F32)
    large = REL_MAX_EXACT + jnp.trunc(jnp.log(nf / REL_MAX_EXACT) / math.log(REL_MAX_DIST / REL_MAX_EXACT)
                                      * (REL_BUCKETS - REL_MAX_EXACT)).astype(jnp.int32)
    large = jnp.minimum(large, REL_BUCKETS - 1)
    return jnp.where(n < REL_MAX_EXACT, n, large)


def _bias_kernel(table_ref, ids_ref, out_ref):
    hh = pl.program_id(0)
    ids = ids_ref[...]
    far = table_ref[REL_BUCKETS - 1, hh]
    acc = jnp.full(ids.shape, NEG_INF, F32)
    for bkt in range(REL_BUCKETS):
        acc = jnp.where(ids == bkt, (table_ref[bkt, hh] - far) * LOG2E, acc)
    out_ref[...] = acc


def _bias_call(table, ids):
    n_heads = table.shape[1]
    return pl.pallas_call(
        _bias_kernel,
        grid=(n_heads,),
        in_specs=[pl.BlockSpec(memory_space=pltpu.SMEM), _const_spec(ids.shape)],
        out_specs=pl.BlockSpec((None,) + ids.shape, lambda hh: (hh, 0, 0, 0)),
        out_shape=jax.ShapeDtypeStruct((n_heads,) + ids.shape, F32),
        name="rel_bias_tiles",
    )(table, ids)


def _attn_order(nq):
    far = [(j, i) for j in range(nq) for i in range(j + 2, nq)]
    near = [(0, 0)] + [e for i in range(1, nq) for e in ((i - 1, i), (i, i))]
    return np.asarray(far, np.int32).T.copy(), np.asarray(near, np.int32).T.copy()


def _attn_flat_kernel(ftab_ref, ntab_ref, lam_ref, qz_ref, k_ref, vt_ref, bias_ref, sg_ref, o_ref,
                      m_ref, acc_ref, sa_ref, sb_ref, ma_ref, mb_ref,
                      *, out_scale, tq, n_far, n_near):
    tk = tq
    sub = 2 * LANES
    bufa, bufb = (sa_ref, ma_ref), (sb_ref, mb_ref)

    half = tq // 2
    assert half == sub
    colmax = lambda v: jnp.max(v, axis=0, keepdims=True)

    def scores(buf, pair, kind):
        s_ref, bm_ref = buf
        j, i = pair
        kb = k_ref[pl.ds(pl.multiple_of(j * tk, tk), tk), :]
        qsl = pl.ds(pl.multiple_of(i * tq, tq), tq)
        for c in range(2):
            qc = qz_ref[c, :, qsl]
            if kind == "far":
                s = _dot(kb, qc)
                s_ref[c] = s
                bm_ref[c] = colmax(s)
                continue
            top = _dot(kb[:half], qc)
            if kind == "off1":
                bot = _dot(kb[half:], qc)
                bot_l = bot[:, :half] + bias_ref[1, half:, :half]
                bot_r = bot[:, half:]
                s_ref[c, half:, :half] = bot_l
                m_l = jnp.maximum(colmax(top[:, :half]), colmax(bot_l))
            else:
                top = top + bias_ref[0, :half, :]
                bot_r = _dot(kb[half:], qc[:, half:]) + bias_ref[0, half:, half:]
                m_l = colmax(top[:, :half])
            s_ref[c, :half, :] = top
            s_ref[c, half:, half:] = bot_r
            m_r = jnp.maximum(colmax(top[:, half:]), colmax(bot_r))
            bm_ref[c] = jnp.concatenate([m_l, m_r], axis=1)

    def update(buf, pair, diag=False, first=False):
        s_ref, bm_ref = buf
        j, i = pair
        vb = vt_ref[:, pl.ds(pl.multiple_of(j * tk, tk), tk)]
        q0 = pl.multiple_of(i * tq, tq)
        qsl = pl.ds(q0, tq)
        for c in range(2):
            if first:
                m_new = bm_ref[c]
            else:
                m_old = m_ref[c, :, qsl]
                m_new = jnp.maximum(m_old, bm_ref[c])
                alpha = jnp.exp2(m_old - m_new)
            m_ref[c, :, qsl] = m_new
            for n in range(tq // sub):
                qs = slice(n * sub, (n + 1) * sub)
                asl = pl.ds(pl.multiple_of(q0 + n * sub, sub), sub)
                pv = None
                for kt in range(tk // sub):
                    if diag and kt > n:
                        continue
                    ks = slice(kt * sub, (kt + 1) * sub)
                    p = jnp.exp2(s_ref[c, ks, qs] - m_new[:, qs])
                    d = _dot(vb[:, ks], p.astype(BF16))
                    pv = d if pv is None else pv + d
                acc_ref[c, :, asl] = pv if first else alpha[:, qs] * acc_ref[c, :, asl] + pv

    def finish(i):
        qsl = pl.ds(pl.multiple_of(i * tq, tq), tq)
        o1 = acc_ref[0, :V_DIM, qsl] * (1.0 / acc_ref[0, V_DIM:V_DIM + 1, qsl])
        o2 = acc_ref[1, :V_DIM, qsl] * (1.0 / acc_ref[1, V_DIM:V_DIM + 1, qsl])
        o = o1 - lam_ref[...] * o2
        o = o * lax.rsqrt(jnp.mean(o * o, axis=0, keepdims=True) + EPS) * (sg_ref[...] * out_scale)
        o_ref[qsl, :] = o.T.astype(BF16)

    far = lambda s: (ftab_ref[0, s], ftab_ref[1, s])
    near = lambda s: (ntab_ref[0, s], ntab_ref[1, s])

    scores(bufa, far(0), "far")

    def far_two(t, first=False):
        scores(bufb, far(2 * t + 1), "far")
        update(bufa, far(2 * t), first=first)
        scores(bufa, far(2 * t + 2), "far")
        update(bufb, far(2 * t + 1), first=first)

    def near_two(t, first=False):
        scores(bufa, near(2 * t + 1), "off1")
        update(bufb, near(2 * t), diag=True, first=first)
        finish(t)
        scores(bufb, near(2 * t + 2), "diag")
        update(bufa, near(2 * t + 1), first=first)

    def run(two, start, stop):
        def body(t, carry):
            two(start + 2 * t)
            two(start + 2 * t + 1)
            return carry
        lax.fori_loop(0, (stop - start) // 2, body, 0)
        if (stop - start) % 2:
            two(stop - 1)

    nq = qz_ref.shape[-1] // tq
    n_first = (nq - 2) // 2
    for t in range(n_first):
        far_two(t, first=True)
    run(far_two, n_first, (n_far - 1) // 2)
    scores(bufb, near(0), "diag")
    update(bufa, far(n_far - 1))
    near_two(0, first=True)
    run(near_two, 1, (n_near - 1) // 2)
    update(bufb, near(n_near - 1), diag=True)
    finish((n_near - 1) // 2)


def _attn_flat_call(lam, qz, k, vt, bias, subln_g, out_scale):
    bsz, n_heads, seq, _ = k.shape
    tq = ATTN_TQ
    nq = seq // tq
    assert seq % tq == 0 and nq % 4 == 0 and bias.shape[1:] == (2, tq, tq)
    ftab, ntab = _attn_order(nq)
    n_far, n_near = ftab.shape[1], ntab.shape[1]
    assert n_far % 2 == 1 and n_near == 2 * nq - 1
    smem = pl.BlockSpec(memory_space=pltpu.SMEM)
    per_head = lambda *blk: pl.BlockSpec((None, None) + blk, lambda b, hh: (b, hh, 0, 0))
    return pl.pallas_call(
        functools.partial(_attn_flat_kernel, out_scale=out_scale, tq=tq, n_far=n_far, n_near=n_near),
        grid=(bsz, n_heads),
        in_specs=[smem, smem, _const_spec(lam.shape),
                  pl.BlockSpec((None, None, 2, V_DIM, seq), lambda b, hh: (b, hh, 0, 0, 0)),
                  per_head(seq, V_DIM), per_head(V_AUG, seq),
                  pl.BlockSpec((None,) + bias.shape[1:], lambda b, hh: (hh, 0, 0, 0)),
                  _const_spec(subln_g.shape)],
        out_specs=pl.BlockSpec((None, seq, V_DIM), lambda b, hh: (b, 0, hh)),
        out_shape=jax.ShapeDtypeStruct((bsz, seq, n_heads * V_DIM), BF16),
        scratch_shapes=[pltpu.VMEM((2, 1, seq), F32),
                        pltpu.VMEM((2, V_AUG, seq), F32),
                        pltpu.VMEM((2, tq, tq), F32), pltpu.VMEM((2, tq, tq), F32),
                        pltpu.VMEM((2, 1, tq), F32), pltpu.VMEM((2, 1, tq), F32)],
        compiler_params=pltpu.CompilerParams(dimension_semantics=("parallel", "parallel"),
                                             vmem_limit_bytes=VMEM_LIMIT_BYTES),
        name="diff_attention",
    )(jnp.asarray(ftab), jnp.asarray(ntab), lam, qz, k, vt, bias, subln_g)


def _final_kernel(o_ref, zg_ref, x_ref, p_ref, wout_ref, wgate_ref, wproj_ref, out_ref):
    g = (o_ref[...].astype(F32) * zg_ref[...].astype(F32)).astype(BF16)
    x1 = x_ref[...] + _dot(g, wout_ref[...])
    out_ref[...] = _ple(x1, p_ref[...], wgate_ref, wproj_ref)


def _final_call(o, zg, x, p, p_layer, wout, wgate, wproj):
    bsz, seq, dm = x.shape
    tm = FINAL_ROW_TILE
    assert seq % tm == 0
    row = lambda b, i: (b, i, 0)
    return pl.pallas_call(
        _final_kernel,
        grid=(bsz, seq // tm),
        in_specs=[pl.BlockSpec((None, tm, o.shape[-1]), row), pl.BlockSpec((None, tm, zg.shape[-1]), row),
                  pl.BlockSpec((None, tm, dm), row), pl.BlockSpec((None, None, tm, p.shape[-1]), lambda b, i: (p_layer, b, i, 0)),
                  _const_spec(wout.shape), _const_spec(wgate.shape), _const_spec(wproj.shape)],
        out_specs=pl.BlockSpec((None, tm, dm), row),
        out_shape=jax.ShapeDtypeStruct((bsz, seq, dm), F32),
        compiler_params=pltpu.CompilerParams(dimension_semantics=("parallel", "parallel"),
                                             vmem_limit_bytes=VMEM_LIMIT_BYTES),
        name="final_rowwise",
    )(o, zg, x, p, wout, wgate, wproj)


def _ssm_params(lam_re, lam_im, log_dt, b_re, b_im, c_re, c_im):
    lr, li = lam_re.astype(F32), lam_im.astype(F32)
    dt = jnp.exp(log_dt.astype(F32))[:, None]
    mag = jnp.exp(lr * dt)
    ab_re, ab_im = mag * jnp.cos(li * dt), mag * jnp.sin(li * dt)
    den = lr * lr + li * li
    nr, ni = ab_re - 1.0, ab_im
    f_re = ((nr * lr + ni * li) / den)[..., None]
    f_im = ((ni * lr - nr * li) / den)[..., None]
    br, bi = b_re.astype(F32), b_im.astype(F32)
    bb_re, bb_im = f_re * br - f_im * bi, f_re * bi + f_im * br
    n_tiles = lam_re.shape[0] // GROUPS_PER_TILE
    eye = jnp.eye(GROUPS_PER_TILE, dtype=F32)

    def pack_b(m):
        m = m.reshape(n_tiles, GROUPS_PER_TILE, SSM_STATE, SSM_GROUP)
        return jnp.einsum('jgpc,gh->jgchp', m, eye).reshape(n_tiles, LANES, STATE_LANES)

    def pack_c(m):
        m = m.reshape(n_tiles, GROUPS_PER_TILE, SSM_GROUP, SSM_STATE)
        return jnp.einsum('jgcp,gh->jgphc', m, eye).reshape(n_tiles, STATE_LANES, LANES)

    bbig = jnp.concatenate([pack_b(bb_re), pack_b(bb_im)], axis=2).astype(BF16)
    cbig = jnp.concatenate([pack_c(c_re.astype(F32)), pack_c(-c_im.astype(F32))], axis=1).astype(BF16)
    a_re = ab_re.reshape(n_tiles, 1, STATE_LANES)
    a_im = ab_im.reshape(n_tiles, 1, STATE_LANES)
    return bbig, cbig, a_re, a_im


def _bias_ids(tk, tq, seq):
    kk = jnp.arange(tk, dtype=jnp.int32)[:, None]
    qq = jnp.arange(tq, dtype=jnp.int32)[None, :]
    rel0 = qq - kk
    rel1 = rel0 + tk
    ids0 = jnp.where(rel0 >= 0, _bucket_of(rel0), -1)
    far = np.arange(tk // 2 + 1, max(seq, tk + 2), dtype=np.float32)
    far = REL_MAX_EXACT + (np.log(far / REL_MAX_EXACT) / math.log(REL_MAX_DIST / REL_MAX_EXACT)
                           * (REL_BUCKETS - REL_MAX_EXACT)).astype(np.int32)
    assert far.min() > REL_BUCKETS
    return jnp.stack([ids0, _bucket_of(rel1)])


def kernel(x, p, a_norm_g, a_w_in, a_lam_re, a_lam_im, a_log_dt, a_b_re, a_b_im, a_c_re, a_c_im, a_d,
           a_w_glu, a_w_out, kv_norm_g, w_k, w_v, k_norm_g, b_norm_g, b_w_in, b_q_norm_g, b_lam_q1,
           b_lam_k1, b_lam_q2, b_lam_k2, b_subln_g, b_w_out, rel_bias, ple_w_proj, ple_w_gate):
    assert a_norm_g.shape[0] == 1 and b_norm_g.shape[0] == 1 and p.shape[0] == 2
    seq = x.shape[1]
    row = lambda v: v.reshape(1, -1).astype(F32)

    bbig, cbig, a_re, a_im = _ssm_params(a_lam_re[0], a_lam_im[0], a_log_dt[0], a_b_re[0], a_b_im[0],
                                         a_c_re[0], a_c_im[0])
    y, zg = _ssm_call(x, row(a_norm_g[0]), a_w_in[0].astype(BF16), bbig, cbig, a_re, a_im, row(a_d[0]))

    q_scale = HEAD_DIM ** -0.5 * LOG2E
    x2, qz, k, vt, zg1 = _mid_call(
        y, zg, x, p, 0, a_w_glu[0].astype(BF16), a_w_out[0].astype(BF16),
        ple_w_gate[0].astype(BF16), ple_w_proj[0].astype(BF16),
        row(kv_norm_g), w_k.astype(BF16), w_v.astype(BF16), k_norm_g.astype(F32).reshape(HEAD_DIM, 1),
        row(b_norm_g[0]), b_w_in[0].astype(BF16), (b_q_norm_g[0].astype(F32) * q_scale).reshape(HEAD_DIM, 1))

    layer_idx = 1
    lam_init = 0.8 - 0.6 * math.exp(-0.3 * layer_idx)
    lam = (jnp.exp(jnp.sum(b_lam_q1[0].astype(F32) * b_lam_k1[0].astype(F32)))
           - jnp.exp(jnp.sum(b_lam_q2[0].astype(F32) * b_lam_k2[0].astype(F32))) + lam_init).reshape(1, 1)
    bias = _bias_call(rel_bias.astype(F32), _bias_ids(ATTN_TQ, ATTN_TQ, seq))
    o = _attn_flat_call(lam, qz, k, vt, bias, b_subln_g[0].astype(F32).reshape(V_DIM, 1), 1.0 - lam_init)
    return _final_call(o, zg1, x2, p, 1, b_w_out[0].astype(BF16), ple_w_gate[1].astype(BF16),
                       ple_w_proj[1].astype(BF16))
```

```python
import functools
import math

import jax
import jax.numpy as jnp
import numpy as np
from jax import lax
from jax.experimental import pallas as pl
from jax.experimental.pallas import tpu as pltpu

F32 = jnp.float32
BF16 = jnp.bfloat16

SUBLANES = 8
LANES = 128
VMEM_LIMIT_BYTES = 56 * 1024 * 1024

EPS = 1e-6
NEG_INF = -1e30
LOG2E = 1.4426950408889634

SSM_GROUP = 16
SSM_STATE = 64
GROUPS_PER_TILE = LANES // SSM_GROUP
STATE_LANES = GROUPS_PER_TILE * SSM_STATE
HEAD_DIM = 64
V_DIM = 2 * HEAD_DIM
V_AUG = V_DIM + 2 * SUBLANES
REL_BUCKETS = 32
REL_MAX_EXACT = REL_BUCKETS // 2
REL_MAX_DIST = 128

SSM_TIME_CHUNK = 64
ROW_TILE = 256
FINAL_ROW_TILE = 512
ATTN_TQ = 512


def _rms(x, g=None):
    y = x * lax.rsqrt(jnp.mean(x * x, axis=-1, keepdims=True) + EPS)
    return y if g is None else y * g


def _dot(a, b):
    return jnp.dot(a, b, preferred_element_type=F32)


def _sigmoid(x):
    return 1.0 / (1.0 + jnp.exp(-x))


def _gelu_tanh(x):
    c = math.sqrt(2.0 / math.pi)
    return 0.5 * x * (1.0 + jnp.tanh(c * (x + 0.044715 * (x * x * x))))


def _const_spec(shape):
    nd = len(shape)
    return pl.BlockSpec(shape, lambda *_: (0,) * nd, pipeline_mode=pl.Buffered(1))


def _ssm_kernel(x_ref, g_ref, win_ref, bbig_ref, cbig_ref, are_ref, aim_ref, d_ref,
                y_ref, zg_ref,
                hr_ref, hi_ref, us_ref, xs_ref, hs_ref, ys_ref, *, pitch):
    bsz, tc, dm = x_ref.shape
    e = y_ref.shape[-1]
    n_tiles = e // LANES
    rows = bsz * tc

    @pl.when(pl.program_id(0) == 0)
    def _():
        hr_ref[...] = jnp.zeros_like(hr_ref)
        hi_ref[...] = jnp.zeros_like(hi_ref)

    x = x_ref[...].reshape(rows, dm)
    h = _rms(x, g_ref[...]).astype(BF16)
    z = _dot(h, win_ref[:, e:])
    zg_ref[...] = (z * _sigmoid(z)).astype(BF16).reshape(bsz, tc, e)
    u = _dot(h, win_ref[:, :e])
    for j in range(n_tiles):
        for b in range(bsz):
            us_ref[j, b * pitch:b * pitch + tc, :] = u[b * tc:(b + 1) * tc, j * LANES:(j + 1) * LANES]

    for j in range(n_tiles):
        xs, hs, ys = xs_ref.at[j % 2], hs_ref.at[j % 2], ys_ref.at[j % 2]
        u_tb = jnp.concatenate(
            [us_ref[j, pl.ds(t, bsz, stride=pitch), :] for t in range(tc)], axis=0)
        xs[...] = _dot(u_tb.astype(BF16), bbig_ref[j])
        ar = jnp.broadcast_to(are_ref[j], (bsz, STATE_LANES))
        ai = jnp.broadcast_to(aim_ref[j], (bsz, STATE_LANES))
        sr, si = hr_ref[j], hi_ref[j]
        for t in range(tc):
            r0 = t * bsz
            xr = xs[r0:r0 + bsz, :STATE_LANES]
            xi = xs[r0:r0 + bsz, STATE_LANES:]
            sr, si = ar * sr - ai * si + xr, ar * si + ai * sr + xi
            hs[r0:r0 + bsz, :STATE_LANES] = sr
            hs[r0:r0 + bsz, STATE_LANES:] = si
        hr_ref[j] = sr
        hi_ref[j] = si

        y_tb = _dot(hs[...].astype(BF16), cbig_ref[j])
        y_tb = y_tb + d_ref[:, j * LANES:(j + 1) * LANES] * u_tb
        ys[...] = _gelu_tanh(y_tb)
        for b in range(bsz):
            y_ref[b, :, j * LANES:(j + 1) * LANES] = ys[pl.ds(b, tc, stride=bsz), :].astype(BF16)


def _ssm_call(x, norm_g, w_in, bbig, cbig, a_re, a_im, d):
    bsz, seq, dm = x.shape
    e = d.shape[-1]
    tc = SSM_TIME_CHUNK
    assert bsz == SUBLANES and seq % tc == 0 and e % LANES == 0 and tc % (2 * SUBLANES) == 0
    pitch = tc + SUBLANES
    n_tiles = e // LANES
    rows = bsz * tc
    out_shape = [jax.ShapeDtypeStruct((bsz, seq, e), BF16)] * 2
    blk = lambda i: (0, i, 0)
    return pl.pallas_call(
        functools.partial(_ssm_kernel, pitch=pitch),
        grid=(seq // tc,),
        in_specs=[pl.BlockSpec((bsz, tc, dm), blk),
                  _const_spec(norm_g.shape), _const_spec(w_in.shape), _const_spec(bbig.shape),
                  _const_spec(cbig.shape), _const_spec(a_re.shape), _const_spec(a_im.shape),
                  _const_spec(d.shape)],
        out_specs=[pl.BlockSpec((bsz, tc, e), blk)] * 2,
        out_shape=out_shape,
        scratch_shapes=[pltpu.VMEM((n_tiles, bsz, STATE_LANES), F32),
                        pltpu.VMEM((n_tiles, bsz, STATE_LANES), F32),
                        pltpu.VMEM((n_tiles, bsz * pitch, LANES), F32),
                        pltpu.VMEM((2, rows, 2 * STATE_LANES), F32),
                        pltpu.VMEM((2, rows, 2 * STATE_LANES), F32),
                        pltpu.VMEM((2, rows, LANES), F32)],
        compiler_params=pltpu.CompilerParams(dimension_semantics=("arbitrary",),
                                             vmem_limit_bytes=VMEM_LIMIT_BYTES),
        name="ssm_front",
    )(x, norm_g, w_in, bbig, cbig, a_re, a_im, d)


def _half_head_norm_t(part, gain_col):
    return part * lax.rsqrt(jnp.mean(part * part, axis=0, keepdims=True) + EPS) * gain_col


def _ple(x1, p, wgate_ref, wproj_ref):
    gate = _sigmoid(_dot(_rms(x1).astype(BF16), wgate_ref[...]))
    return x1 + gate * _dot(p.astype(BF16), wproj_ref[...])


def _mid_kernel(y_ref, zg_ref, x_ref, p_ref,
                wglu_ref, wout_ref, wgate_ref, wproj_ref,
                kvg_ref, wk_ref, wv_ref, kng_ref,
                bng_ref, bwin_ref, qng_ref,
                x2_ref, qz_ref, k_ref, vt_ref, zg1_ref):
    e = y_ref.shape[-1]
    n_heads = k_ref.shape[0]
    aw = n_heads * V_DIM
    gab = _dot(y_ref[...], wglu_ref[...])
    gl = gab[:, :e] * _sigmoid(gab[:, e:]) * zg_ref[...].astype(F32)
    x1 = x_ref[...] + _dot(gl.astype(BF16), wout_ref[...])
    x2 = _ple(x1, p_ref[...], wgate_ref, wproj_ref)
    x2_ref[...] = x2

    hk = _rms(x2, kvg_ref[...]).astype(BF16)
    k = _dot(hk, wk_ref[...])
    v = _dot(hk, wv_ref[...])
    hq = _rms(x2, bng_ref[...]).astype(BF16)
    q = _dot(hq, bwin_ref[:, :aw])
    z1 = _dot(hq, bwin_ref[:, aw:])
    zg1_ref[...] = (z1 * _sigmoid(z1)).astype(BF16)
    pad_rows = (lax.broadcasted_iota(jnp.int32, (V_AUG - V_DIM, v.shape[0]), 0) == 0).astype(BF16)
    for hh in range(n_heads):
        sl = slice(hh * V_DIM, (hh + 1) * V_DIM)
        kt = k[:, sl].T
        kn = jnp.concatenate([_half_head_norm_t(kt[c * HEAD_DIM:(c + 1) * HEAD_DIM], kng_ref[...])
                              for c in range(2)], axis=0)
        k_ref[hh] = kn.T.astype(BF16)
        qt = q[:, sl].T
        for c in range(2):
            part = _half_head_norm_t(qt[c * HEAD_DIM:(c + 1) * HEAD_DIM], qng_ref[...])
            qz_ref[hh, c, c * HEAD_DIM:(c + 1) * HEAD_DIM, :] = part.astype(BF16)
            qz_ref[hh, c, (1 - c) * HEAD_DIM:(2 - c) * HEAD_DIM, :] = jnp.zeros_like(part, dtype=BF16)
        vt_ref[hh, :V_DIM, :] = v[:, sl].T.astype(BF16)
        vt_ref[hh, V_DIM:, :] = pad_rows


def _mid_call(y, zg, x, p, p_layer, wglu, wout, wgate, wproj, kvg, wk, wv, kng, bng, bwin, qng):
    bsz, seq, dm = x.shape
    e = y.shape[-1]
    aw = wk.shape[-1]
    n_heads = aw // V_DIM
    tm = ROW_TILE
    assert seq % tm == 0
    row = lambda b, i: (b, i, 0)
    out_shape = [jax.ShapeDtypeStruct((bsz, seq, dm), F32),
                 jax.ShapeDtypeStruct((bsz, n_heads, 2, V_DIM, seq), BF16),
                 jax.ShapeDtypeStruct((bsz, n_heads, seq, V_DIM), BF16),
                 jax.ShapeDtypeStruct((bsz, n_heads, V_AUG, seq), BF16),
                 jax.ShapeDtypeStruct((bsz, seq, aw), BF16)]
    t_spec = lambda rows: pl.BlockSpec((None, n_heads, rows, tm), lambda b, i: (b, 0, 0, i))
    weights = (wglu, wout, wgate, wproj, kvg, wk, wv, kng, bng, bwin, qng)
    return pl.pallas_call(
        _mid_kernel,
        grid=(bsz, seq // tm),
        in_specs=[pl.BlockSpec((None, tm, e), row), pl.BlockSpec((None, tm, e), row),
                  pl.BlockSpec((None, tm, dm), row), pl.BlockSpec((None, None, tm, p.shape[-1]), lambda b, i: (p_layer, b, i, 0))]
                 + [_const_spec(w.shape) for w in weights],
        out_specs=[pl.BlockSpec((None, tm, dm), row),
                   pl.BlockSpec((None, n_heads, 2, V_DIM, tm), lambda b, i: (b, 0, 0, 0, i)),
                   pl.BlockSpec((None, n_heads, tm, V_DIM), lambda b, i: (b, 0, i, 0)), t_spec(V_AUG),
                   pl.BlockSpec((None, tm, aw), row)],
        out_shape=out_shape,
        compiler_params=pltpu.CompilerParams(dimension_semantics=("parallel", "parallel"),
                                             vmem_limit_bytes=VMEM_LIMIT_BYTES),
        name="mid_rowwise",
    )(y, zg, x, p, *weights)


def _bucket_of(rel):
    n = jnp.maximum(rel, 0)
    nf = jnp.maximum(n, 1).astype(F32)
    large = REL_MAX_EXACT + jnp.trunc(jnp.log(nf / REL_MAX_EXACT) / math.log(REL_MAX_DIST / REL_MAX_EXACT)
                                      * (REL_BUCKETS - REL_MAX_EXACT)).astype(jnp.int32)
    large = jnp.minimum(large, REL_BUCKETS - 1)
    return jnp.where(n < REL_MAX_EXACT, n, large)


def _bias_kernel(table_ref, ids_ref, out_ref):
    hh = pl.program_id(0)
    n_tiles, tile, _ = ids_ref.shape
    far = table_ref[REL_BUCKETS - 1, hh]
    for t in range(n_tiles):
        for a in range(tile // LANES):
            for b in range(tile // LANES):
                centre = t * tile + LANES * (b - a)
                blk = (t, slice(a * LANES, (a + 1) * LANES), slice(b * LANES, (b + 1) * LANES))
                if centre + (LANES - 1) < 0:
                    out_ref[blk] = jnp.full((LANES, LANES), NEG_INF, F32)
                elif centre - (LANES - 1) > tile // 2:
                    out_ref[blk] = jnp.zeros((LANES, LANES), F32)
                else:
                    ids = ids_ref[blk]
                    acc = jnp.full(ids.shape, NEG_INF, F32)
                    for bkt in range(REL_BUCKETS):
                        acc = jnp.where(ids == bkt, (table_ref[bkt, hh] - far) * LOG2E, acc)
                    out_ref[blk] = acc


def _bias_call(table, ids):
    n_heads = table.shape[1]
    return pl.pallas_call(
        _bias_kernel,
        grid=(n_heads,),
        in_specs=[pl.BlockSpec(memory_space=pltpu.SMEM), _const_spec(ids.shape)],
        out_specs=pl.BlockSpec((None,) + ids.shape, lambda hh: (hh, 0, 0, 0)),
        out_shape=jax.ShapeDtypeStruct((n_heads,) + ids.shape, F32),
        name="rel_bias_tiles",
    )(table, ids)


def _attn_order(nq):
    far = [(j, i) for j in range(nq) for i in range(j + 2, nq)]
    near = [(0, 0)] + [e for i in range(1, nq) for e in ((i - 1, i), (i, i))]
    return np.asarray(far, np.int32).T.copy(), np.asarray(near, np.int32).T.copy()


def _attn_flat_kernel(ftab_ref, ntab_ref, lam_ref, qz_ref, k_ref, vt_ref, bias_ref, sg_ref, o_ref,
                      m_ref, acc_ref, sa_ref, sb_ref, ma_ref, mb_ref,
                      *, out_scale, tq, n_far, n_near):
    tk = tq
    sub = 2 * LANES
    bufa, bufb = (sa_ref, ma_ref), (sb_ref, mb_ref)

    half = tq // 2
    assert half == sub
    colmax = lambda v: jnp.max(v, axis=0, keepdims=True)

    def scores(buf, pair, kind):
        s_ref, bm_ref = buf
        j, i = pair
        kb = k_ref[pl.ds(pl.multiple_of(j * tk, tk), tk), :]
        qsl = pl.ds(pl.multiple_of(i * tq, tq), tq)
        for c in range(2):
            qc = qz_ref[c, :, qsl]
            if kind == "far":
                s = _dot(kb, qc)
                s_ref[c] = s
                bm_ref[c] = colmax(s)
                continue
            top = _dot(kb[:half], qc)
            if kind == "off1":
                bot = _dot(kb[half:], qc)
                bot_l = bot[:, :half] + bias_ref[1, half:, :half]
                bot_r = bot[:, half:]
                s_ref[c, half:, :half] = bot_l
                m_l = jnp.maximum(colmax(top[:, :half]), colmax(bot_l))
            else:
                top = top + bias_ref[0, :half, :]
                bot_r = _dot(kb[half:], qc[:, half:]) + bias_ref[0, half:, half:]
                m_l = colmax(top[:, :half])
            s_ref[c, :half, :] = top
            s_ref[c, half:, half:] = bot_r
            m_r = jnp.maximum(colmax(top[:, half:]), colmax(bot_r))
            bm_ref[c] = jnp.concatenate([m_l, m_r], axis=1)

    def update(buf, pair, diag=False, first=False):
        s_ref, bm_ref = buf
        j, i = pair
        vb = vt_ref[:, pl.ds(pl.multiple_of(j * tk, tk), tk)]
        q0 = pl.multiple_of(i * tq, tq)
        qsl = pl.ds(q0, tq)
        for c in range(2):
            if first:
                m_new = bm_ref[c]
            else:
                m_old = m_ref[c, :, qsl]
                m_new = jnp.maximum(m_old, bm_ref[c])
                alpha = jnp.exp2(m_old - m_new)
            m_ref[c, :, qsl] = m_new
            for n in range(tq // sub):
                qs = slice(n * sub, (n + 1) * sub)
                asl = pl.ds(pl.multiple_of(q0 + n * sub, sub), sub)
                pv = None
                for kt in range(tk // sub):
                    if diag and kt > n:
                        continue
                    ks = slice(kt * sub, (kt + 1) * sub)
                    p = jnp.exp2(s_ref[c, ks, qs] - m_new[:, qs])
                    d = _dot(vb[:, ks], p.astype(BF16))
                    pv = d if pv is None else pv + d
                acc_ref[c, :, asl] = pv if first else alpha[:, qs] * acc_ref[c, :, asl] + pv

    def finish(i):
        qsl = pl.ds(pl.multiple_of(i * tq, tq), tq)
        o1 = acc_ref[0, :V_DIM, qsl] * (1.0 / acc_ref[0, V_DIM:V_DIM + 1, qsl])
        o2 = acc_ref[1, :V_DIM, qsl] * (1.0 / acc_ref[1, V_DIM:V_DIM + 1, qsl])
        o = o1 - lam_ref[...] * o2
        o = o * lax.rsqrt(jnp.mean(o * o, axis=0, keepdims=True) + EPS) * (sg_ref[...] * out_scale)
        o_ref[qsl, :] = o.T.astype(BF16)

    far = lambda s: (ftab_ref[0, s], ftab_ref[1, s])
    near = lambda s: (ntab_ref[0, s], ntab_ref[1, s])

    scores(bufa, far(0), "far")

    def far_two(t, first=False):
        scores(bufb, far(2 * t + 1), "far")
        update(bufa, far(2 * t), first=first)
        scores(bufa, far(2 * t + 2), "far")
        update(bufb, far(2 * t + 1), first=first)

    def near_two(t, first=False):
        scores(bufa, near(2 * t + 1), "off1")
        update(bufb, near(2 * t), diag=True, first=first)
        finish(t)
        scores(bufb, near(2 * t + 2), "diag")
        update(bufa, near(2 * t + 1), first=first)

    def run(two, start, stop):
        def body(t, carry):
            two(start + 2 * t)
            two(start + 2 * t + 1)
            return carry
        lax.fori_loop(0, (stop - start) // 2, body, 0)
        if (stop - start) % 2:
            two(stop - 1)

    nq = qz_ref.shape[-1] // tq
    n_first = (nq - 2) // 2
    for t in range(n_first):
        far_two(t, first=True)
    run(far_two, n_first, (n_far - 1) // 2)
    scores(bufb, near(0), "diag")
    update(bufa, far(n_far - 1))
    near_two(0, first=True)
    run(near_two, 1, (n_near - 1) // 2)
    update(bufb, near(n_near - 1), diag=True)
    finish((n_near - 1) // 2)


def _attn_flat_call(lam, qz, k, vt, bias, subln_g, out_scale):
    bsz, n_heads, seq, _ = k.shape
    tq = ATTN_TQ
    nq = seq // tq
    assert seq % tq == 0 and nq % 4 == 0 and bias.shape[1:] == (2, tq, tq)
    ftab, ntab = _attn_order(nq)
    n_far, n_near = ftab.shape[1], ntab.shape[1]
    assert n_far % 2 == 1 and n_near == 2 * nq - 1
    smem = pl.BlockSpec(memory_space=pltpu.SMEM)
    per_head = lambda *blk: pl.BlockSpec((None, None) + blk, lambda b, hh: (b, hh, 0, 0))
    return pl.pallas_call(
        functools.partial(_attn_flat_kernel, out_scale=out_scale, tq=tq, n_far=n_far, n_near=n_near),
        grid=(bsz, n_heads),
        in_specs=[smem, smem, _const_spec(lam.shape),
                  pl.BlockSpec((None, None, 2, V_DIM, seq), lambda b, hh: (b, hh, 0, 0, 0)),
                  per_head(seq, V_DIM), per_head(V_AUG, seq),
                  pl.BlockSpec((None,) + bias.shape[1:], lambda b, hh: (hh, 0, 0, 0)),
                  _const_spec(subln_g.shape)],
        out_specs=pl.BlockSpec((None, seq, V_DIM), lambda b, hh: (b, 0, hh)),
        out_shape=jax.ShapeDtypeStruct((bsz, seq, n_heads * V_DIM), BF16),
        scratch_shapes=[pltpu.VMEM((2, 1, seq), F32),
                        pltpu.VMEM((2, V_AUG, seq), F32),
                        pltpu.VMEM((2, tq, tq), F32), pltpu.VMEM((2, tq, tq), F32),
                        pltpu.VMEM((2, 1, tq), F32), pltpu.VMEM((2, 1, tq), F32)],
        compiler_params=pltpu.CompilerParams(dimension_semantics=("parallel", "parallel"),
                                             vmem_limit_bytes=VMEM_LIMIT_BYTES),
        name="diff_attention",
    )(jnp.asarray(ftab), jnp.asarray(ntab), lam, qz, k, vt, bias, subln_g)


def _final_kernel(o_ref, zg_ref, x_ref, p_ref, wout_ref, wgate_ref, wproj_ref, out_ref):
    g = (o_ref[...].astype(F32) * zg_ref[...].astype(F32)).astype(BF16)
    x1 = x_ref[...] + _dot(g, wout_ref[...])
    out_ref[...] = _ple(x1, p_ref[...], wgate_ref, wproj_ref)


def _final_call(o, zg, x, p, p_layer, wout, wgate, wproj):
    bsz, seq, dm = x.shape
    tm = FINAL_ROW_TILE
    assert seq % tm == 0
    row = lambda b, i: (b, i, 0)
    return pl.pallas_call(
        _final_kernel,
        grid=(bsz, seq // tm),
        in_specs=[pl.BlockSpec((None, tm, o.shape[-1]), row), pl.BlockSpec((None, tm, zg.shape[-1]), row),
                  pl.BlockSpec((None, tm, dm), row), pl.BlockSpec((None, None, tm, p.shape[-1]), lambda b, i: (p_layer, b, i, 0)),
                  _const_spec(wout.shape), _const_spec(wgate.shape), _const_spec(wproj.shape)],
        out_specs=pl.BlockSpec((None, tm, dm), row),
        out_shape=jax.ShapeDtypeStruct((bsz, seq, dm), F32),
        compiler_params=pltpu.CompilerParams(dimension_semantics=("parallel", "parallel"),
                                             vmem_limit_bytes=VMEM_LIMIT_BYTES),
        name="final_rowwise",
    )(o, zg, x, p, wout, wgate, wproj)


def _ssm_params(lam_re, lam_im, log_dt, b_re, b_im, c_re, c_im):
    lr, li = lam_re.astype(F32), lam_im.astype(F32)
    dt = jnp.exp(log_dt.astype(F32))[:, None]
    mag = jnp.exp(lr * dt)
    ab_re, ab_im = mag * jnp.cos(li * dt), mag * jnp.sin(li * dt)
    den = lr * lr + li * li
    nr, ni = ab_re - 1.0, ab_im
    f_re = ((nr * lr + ni * li) / den)[..., None]
    f_im = ((ni * lr - nr * li) / den)[..., None]
    br, bi = b_re.astype(F32), b_im.astype(F32)
    bb_re, bb_im = f_re * br - f_im * bi, f_re * bi + f_im * br
    n_tiles = lam_re.shape[0] // GROUPS_PER_TILE
    eye = jnp.eye(GROUPS_PER_TILE, dtype=F32)

    def pack_b(m):
        m = m.reshape(n_tiles, GROUPS_PER_TILE, SSM_STATE, SSM_GROUP)
        return jnp.einsum('jgpc,gh->jgchp', m, eye).reshape(n_tiles, LANES, STATE_LANES)

    def pack_c(m):
        m = m.reshape(n_tiles, GROUPS_PER_TILE, SSM_GROUP, SSM_STATE)
        return jnp.einsum('jgcp,gh->jgphc', m, eye).reshape(n_tiles, STATE_LANES, LANES)

    bbig = jnp.concatenate([pack_b(bb_re), pack_b(bb_im)], axis=2).astype(BF16)
    cbig = jnp.concatenate([pack_c(c_re.astype(F32)), pack_c(-c_im.astype(F32))], axis=1).astype(BF16)
    a_re = ab_re.reshape(n_tiles, 1, STATE_LANES)
    a_im = ab_im.reshape(n_tiles, 1, STATE_LANES)
    return bbig, cbig, a_re, a_im


def _bias_ids(tk, tq, seq):
    kk = jnp.arange(tk, dtype=jnp.int32)[:, None]
    qq = jnp.arange(tq, dtype=jnp.int32)[None, :]
    rel0 = qq - kk
    rel1 = rel0 + tk
    ids0 = jnp.where(rel0 >= 0, _bucket_of(rel0), -1)
    far = np.arange(tk // 2 + 1, max(seq, tk + 2), dtype=np.float32)
    far = REL_MAX_EXACT + (np.log(far / REL_MAX_EXACT) / math.log(REL_MAX_DIST / REL_MAX_EXACT)
                           * (REL_BUCKETS - REL_MAX_EXACT)).astype(np.int32)
    assert far.min() > REL_BUCKETS
    return jnp.stack([ids0, _bucket_of(rel1)])


def kernel(x, p, a_norm_g, a_w_in, a_lam_re, a_lam_im, a_log_dt, a_b_re, a_b_im, a_c_re, a_c_im, a_d,
           a_w_glu, a_w_out, kv_norm_g, w_k, w_v, k_norm_g, b_norm_g, b_w_in, b_q_norm_g, b_lam_q1,
           b_lam_k1, b_lam_q2, b_lam_k2, b_subln_g, b_w_out, rel_bias, ple_w_proj, ple_w_gate):
    assert a_norm_g.shape[0] == 1 and b_norm_g.shape[0] == 1 and p.shape[0] == 2
    seq = x.shape[1]
    row = lambda v: v.reshape(1, -1).astype(F32)

    bbig, cbig, a_re, a_im = _ssm_params(a_lam_re[0], a_lam_im[0], a_log_dt[0], a_b_re[0], a_b_im[0],
                                         a_c_re[0], a_c_im[0])
    y, zg = _ssm_call(x, row(a_norm_g[0]), a_w_in[0].astype(BF16), bbig, cbig, a_re, a_im, row(a_d[0]))

    q_scale = HEAD_DIM ** -0.5 * LOG2E
    x2, qz, k, vt, zg1 = _mid_call(
        y, zg, x, p, 0, a_w_glu[0].astype(BF16), a_w_out[0].astype(BF16),
        ple_w_gate[0].astype(BF16), ple_w_proj[0].astype(BF16),
        row(kv_norm_g), w_k.astype(BF16), w_v.astype(BF16), k_norm_g.astype(F32).reshape(HEAD_DIM, 1),
        row(b_norm_g[0]), b_w_in[0].astype(BF16), (b_q_norm_g[0].astype(F32) * q_scale).reshape(HEAD_DIM, 1))

    layer_idx = 1
    lam_init = 0.8 - 0.6 * math.exp(-0.3 * layer_idx)
    lam = (jnp.exp(jnp.sum(b_lam_q1[0].astype(F32) * b_lam_k1[0].astype(F32)))
           - jnp.exp(jnp.sum(b_lam_q2[0].astype(F32) * b_lam_k2[0].astype(F32))) + lam_init).reshape(1, 1)
    bias = _bias_call(rel_bias.astype(F32), _bias_ids(ATTN_TQ, ATTN_TQ, seq))
    o = _attn_flat_call(lam, qz, k, vt, bias, b_subln_g[0].astype(F32).reshape(V_DIM, 1), 1.0 - lam_init)
    return _final_call(o, zg1, x2, p, 1, b_w_out[0].astype(BF16), ple_w_gate[1].astype(BF16),
                       ple_w_proj[1].astype(BF16))
```

```python
import functools
import math

import jax
import jax.numpy as jnp
import numpy as np
from jax import lax
from jax.experimental import pallas as pl
from jax.experimental.pallas import tpu as pltpu

F32 = jnp.float32
BF16 = jnp.bfloat16

SUBLANES = 8
LANES = 128
VMEM_LIMIT_BYTES = 56 * 1024 * 1024

EPS = 1e-6
NEG_INF = -1e30
LOG2E = 1.4426950408889634

SSM_GROUP = 16
SSM_STATE = 64
GROUPS_PER_TILE = LANES // SSM_GROUP
STATE_LANES = GROUPS_PER_TILE * SSM_STATE
HEAD_DIM = 64
V_DIM = 2 * HEAD_DIM
V_AUG = V_DIM + 2 * SUBLANES
REL_BUCKETS = 32
REL_MAX_EXACT = REL_BUCKETS // 2
REL_MAX_DIST = 128

SSM_TIME_CHUNK = 64
ROW_TILE = 512
FINAL_ROW_TILE = 512
ATTN_TQ = 512


def _rms(x, g=None):
    y = x * lax.rsqrt(jnp.mean(x * x, axis=-1, keepdims=True) + EPS)
    return y if g is None else y * g


def _dot(a, b):
    return jnp.dot(a, b, preferred_element_type=F32)


def _sigmoid(x):
    return 1.0 / (1.0 + jnp.exp(-x))


def _gelu_tanh(x):
    c = math.sqrt(2.0 / math.pi)
    return 0.5 * x * (1.0 + jnp.tanh(c * (x + 0.044715 * (x * x * x))))


def _const_spec(shape):
    nd = len(shape)
    return pl.BlockSpec(shape, lambda *_: (0,) * nd, pipeline_mode=pl.Buffered(1))


def _ssm_kernel(x_ref, g_ref, win_ref, bbig_ref, cbig_ref, are_ref, aim_ref, d_ref,
                y_ref, zg_ref,
                hr_ref, hi_ref, us_ref, xs_ref, hs_ref, ys_ref, *, pitch):
    bsz, tc, dm = x_ref.shape
    e = y_ref.shape[-1]
    n_tiles = e // LANES
    rows = bsz * tc

    @pl.when(pl.program_id(0) == 0)
    def _():
        hr_ref[...] = jnp.zeros_like(hr_ref)
        hi_ref[...] = jnp.zeros_like(hi_ref)

    x = x_ref[...].reshape(rows, dm)
    h = _rms(x, g_ref[...]).astype(BF16)
    z = _dot(h, win_ref[:, e:])
    zg_ref[...] = (z * _sigmoid(z)).astype(BF16).reshape(bsz, tc, e)
    u = _dot(h, win_ref[:, :e])
    for j in range(n_tiles):
        for b in range(bsz):
            us_ref[j, b * pitch:b * pitch + tc, :] = u[b * tc:(b + 1) * tc, j * LANES:(j + 1) * LANES]

    for j in range(n_tiles):
        xs, hs, ys = xs_ref.at[j % 2], hs_ref.at[j % 2], ys_ref.at[j % 2]
        u_tb = jnp.concatenate(
            [us_ref[j, pl.ds(t, bsz, stride=pitch), :] for t in range(tc)], axis=0)
        xs[...] = _dot(u_tb.astype(BF16), bbig_ref[j])
        ar = jnp.broadcast_to(are_ref[j], (bsz, STATE_LANES))
        ai = jnp.broadcast_to(aim_ref[j], (bsz, STATE_LANES))
        sr, si = hr_ref[j], hi_ref[j]
        for t in range(tc):
            r0 = t * bsz
            xr = xs[r0:r0 + bsz, :STATE_LANES]
            xi = xs[r0:r0 + bsz, STATE_LANES:]
            sr, si = ar * sr - ai * si + xr, ar * si + ai * sr + xi
            hs[r0:r0 + bsz, :STATE_LANES] = sr
            hs[r0:r0 + bsz, STATE_LANES:] = si
        hr_ref[j] = sr
        hi_ref[j] = si

        y_tb = _dot(hs[...].astype(BF16), cbig_ref[j])
        y_tb = y_tb + d_ref[:, j * LANES:(j + 1) * LANES] * u_tb
        ys[...] = _gelu_tanh(y_tb)
        for b in range(bsz):
            y_ref[b, :, j * LANES:(j + 1) * LANES] = ys[pl.ds(b, tc, stride=bsz), :].astype(BF16)


def _ssm_call(x, norm_g, w_in, bbig, cbig, a_re, a_im, d):
    bsz, seq, dm = x.shape
    e = d.shape[-1]
    tc = SSM_TIME_CHUNK
    assert bsz == SUBLANES and seq % tc == 0 and e % LANES == 0 and tc % (2 * SUBLANES) == 0
    pitch = tc + SUBLANES
    n_tiles = e // LANES
    rows = bsz * tc
    out_shape = [jax.ShapeDtypeStruct((bsz, seq, e), BF16)] * 2
    blk = lambda i: (0, i, 0)
    return pl.pallas_call(
        functools.partial(_ssm_kernel, pitch=pitch),
        grid=(seq // tc,),
        in_specs=[pl.BlockSpec((bsz, tc, dm), blk),
                  _const_spec(norm_g.shape), _const_spec(w_in.shape), _const_spec(bbig.shape),
                  _const_spec(cbig.shape), _const_spec(a_re.shape), _const_spec(a_im.shape),
                  _const_spec(d.shape)],
        out_specs=[pl.BlockSpec((bsz, tc, e), blk)] * 2,
        out_shape=out_shape,
        scratch_shapes=[pltpu.VMEM((n_tiles, bsz, STATE_LANES), F32),
                        pltpu.VMEM((n_tiles, bsz, STATE_LANES), F32),
                        pltpu.VMEM((n_tiles, bsz * pitch, LANES), F32),
                        pltpu.VMEM((2, rows, 2 * STATE_LANES), F32),
                        pltpu.VMEM((2, rows, 2 * STATE_LANES), F32),
                        pltpu.VMEM((2, rows, LANES), F32)],
        compiler_params=pltpu.CompilerParams(dimension_semantics=("arbitrary",),
                                             vmem_limit_bytes=VMEM_LIMIT_BYTES),
        name="ssm_front",
    )(x, norm_g, w_in, bbig, cbig, a_re, a_im, d)


def _half_head_norm_t(part, gain_col):
    return part * lax.rsqrt(jnp.mean(part * part, axis=0, keepdims=True) + EPS) * gain_col


def _ple(x1, p, wgate_ref, wproj_ref):
    gate = _sigmoid(_dot(_rms(x1).astype(BF16), wgate_ref[...]))
    return x1 + gate * _dot(p.astype(BF16), wproj_ref[...])


def _mid_kernel(y_ref, zg_ref, x_ref, p_ref,
                wglu_ref, wout_ref, wgate_ref, wproj_ref,
                kvg_ref, wk_ref, wv_ref, kng_ref,
                bng_ref, bwin_ref, qng_ref,
                x2_ref, qz_ref, k_ref, vt_ref, zg1_ref):
    e = y_ref.shape[-1]
    n_heads = k_ref.shape[0]
    aw = n_heads * V_DIM
    gab = _dot(y_ref[...], wglu_ref[...])
    gl = gab[:, :e] * _sigmoid(gab[:, e:]) * zg_ref[...].astype(F32)
    x1 = x_ref[...] + _dot(gl.astype(BF16), wout_ref[...])
    x2 = _ple(x1, p_ref[...], wgate_ref, wproj_ref)
    x2_ref[...] = x2

    hk = _rms(x2, kvg_ref[...]).astype(BF16)
    k = _dot(hk, wk_ref[...])
    v = _dot(hk, wv_ref[...])
    hq = _rms(x2, bng_ref[...]).astype(BF16)
    q = _dot(hq, bwin_ref[:, :aw])
    z1 = _dot(hq, bwin_ref[:, aw:])
    zg1_ref[...] = (z1 * _sigmoid(z1)).astype(BF16)
    pad_rows = (lax.broadcasted_iota(jnp.int32, (V_AUG - V_DIM, v.shape[0]), 0) == 0).astype(BF16)
    for hh in range(n_heads):
        sl = slice(hh * V_DIM, (hh + 1) * V_DIM)
        kt = k[:, sl].T
        kn = jnp.concatenate([_half_head_norm_t(kt[c * HEAD_DIM:(c + 1) * HEAD_DIM], kng_ref[...])
                              for c in range(2)], axis=0)
        k_ref[hh] = kn.T.astype(BF16)
        qt = q[:, sl].T
        for c in range(2):
            part = _half_head_norm_t(qt[c * HEAD_DIM:(c + 1) * HEAD_DIM], qng_ref[...])
            qz_ref[hh, c, c * HEAD_DIM:(c + 1) * HEAD_DIM, :] = part.astype(BF16)
            qz_ref[hh, c, (1 - c) * HEAD_DIM:(2 - c) * HEAD_DIM, :] = jnp.zeros_like(part, dtype=BF16)
        vt_ref[hh, :V_DIM, :] = v[:, sl].T.astype(BF16)
        vt_ref[hh, V_DIM:, :] = pad_rows


def _mid_call(y, zg, x, p, p_layer, wglu, wout, wgate, wproj, kvg, wk, wv, kng, bng, bwin, qng):
    bsz, seq, dm = x.shape
    e = y.shape[-1]
    aw = wk.shape[-1]
    n_heads = aw // V_DIM
    tm = ROW_TILE
    assert seq % tm == 0
    row = lambda b, i: (b, i, 0)
    out_shape = [jax.ShapeDtypeStruct((bsz, seq, dm), F32),
                 jax.ShapeDtypeStruct((bsz, n_heads, 2, V_DIM, seq), BF16),
                 jax.ShapeDtypeStruct((bsz, n_heads, seq, V_DIM), BF16),
                 jax.ShapeDtypeStruct((bsz, n_heads, V_AUG, seq), BF16),
                 jax.ShapeDtypeStruct((bsz, seq, aw), BF16)]
    t_spec = lambda rows: pl.BlockSpec((None, n_heads, rows, tm), lambda b, i: (b, 0, 0, i))
    weights = (wglu, wout, wgate, wproj, kvg, wk, wv, kng, bng, bwin, qng)
    return pl.pallas_call(
        _mid_kernel,
        grid=(bsz, seq // tm),
        in_specs=[pl.BlockSpec((None, tm, e), row), pl.BlockSpec((None, tm, e), row),
                  pl.BlockSpec((None, tm, dm), row), pl.BlockSpec((None, None, tm, p.shape[-1]), lambda b, i: (p_layer, b, i, 0))]
                 + [_const_spec(w.shape) for w in weights],
        out_specs=[pl.BlockSpec((None, tm, dm), row),
                   pl.BlockSpec((None, n_heads, 2, V_DIM, tm), lambda b, i: (b, 0, 0, 0, i)),
                   pl.BlockSpec((None, n_heads, tm, V_DIM), lambda b, i: (b, 0, i, 0)), t_spec(V_AUG),
                   pl.BlockSpec((None, tm, aw), row)],
        out_shape=out_shape,
        compiler_params=pltpu.CompilerParams(dimension_semantics=("parallel", "parallel"),
                                             vmem_limit_bytes=VMEM_LIMIT_BYTES),
        name="mid_rowwise",
    )(y, zg, x, p, *weights)


def _bucket_of(rel):
    n = jnp.maximum(rel, 0)
    nf = jnp.maximum(n, 1).astype(F32)
    large = REL_MAX_EXACT + jnp.trunc(jnp.log(nf / REL_MAX_EXACT) / math.log(REL_MAX_DIST / REL_MAX_EXACT)
                                      * (REL_BUCKETS - REL_MAX_EXACT)).astype(jnp.int32)
    large = jnp.minimum(large, REL_BUCKETS - 1)
    return jnp.where(n < REL_MAX_EXACT, n, large)


def _bias_kernel(table_ref, ids_ref, out_ref):
    hh = pl.program_id(0)
    n_tiles, tile, _ = ids_ref.shape
    far = table_ref[REL_BUCKETS - 1, hh]
    for t in range(n_tiles):
        for a in range(tile // LANES):
            for b in range(tile // LANES):
                centre = t * tile + LANES * (b - a)
                blk = (t, slice(a * LANES, (a + 1) * LANES), slice(b * LANES, (b + 1) * LANES))
                if centre + (LANES - 1) < 0:
                    out_ref[blk] = jnp.full((LANES, LANES), NEG_INF, F32)
                elif centre - (LANES - 1) > tile // 2:
                    out_ref[blk] = jnp.zeros((LANES, LANES), F32)
                else:
                    ids = ids_ref[blk]
                    acc = jnp.full(ids.shape, NEG_INF, F32)
                    for bkt in range(REL_BUCKETS):
                        acc = jnp.where(ids == bkt, (table_ref[bkt, hh] - far) * LOG2E, acc)
                    out_ref[blk] = acc


def _bias_call(table, ids):
    n_heads = table.shape[1]
    return pl.pallas_call(
        _bias_kernel,
        grid=(n_heads,),
        in_specs=[pl.BlockSpec(memory_space=pltpu.SMEM), _const_spec(ids.shape)],
        out_specs=pl.BlockSpec((None,) + ids.shape, lambda hh: (hh, 0, 0, 0)),
        out_shape=jax.ShapeDtypeStruct((n_heads,) + ids.shape, F32),
        name="rel_bias_tiles",
    )(table, ids)


def _attn_order(nq):
    far = [(j, i) for j in range(nq) for i in range(j + 2, nq)]
    near = [(0, 0)] + [e for i in range(1, nq) for e in ((i - 1, i), (i, i))]
    return np.asarray(far, np.int32).T.copy(), np.asarray(near, np.int32).T.copy()


def _attn_flat_kernel(ftab_ref, ntab_ref, lam_ref, qz_ref, k_ref, vt_ref, bias_ref, sg_ref, o_ref,
                      m_ref, acc_ref, sa_ref, sb_ref, ma_ref, mb_ref,
                      *, out_scale, tq, n_far, n_near):
    tk = tq
    sub = 2 * LANES
    bufa, bufb = (sa_ref, ma_ref), (sb_ref, mb_ref)

    half = tq // 2
    assert half == sub
    colmax = lambda v: jnp.max(v, axis=0, keepdims=True)

    def scores(buf, pair, kind):
        s_ref, bm_ref = buf
        j, i = pair
        kb = k_ref[pl.ds(pl.multiple_of(j * tk, tk), tk), :]
        qsl = pl.ds(pl.multiple_of(i * tq, tq), tq)
        for c in range(2):
            qc = qz_ref[c, :, qsl]
            if kind == "far":
                s = _dot(kb, qc)
                s_ref[c] = s
                bm_ref[c] = colmax(s)
                continue
            top = _dot(kb[:half], qc)
            if kind == "off1":
                bot = _dot(kb[half:], qc)
                bot_l = bot[:, :half] + bias_ref[1, half:, :half]
                bot_r = bot[:, half:]
                s_ref[c, half:, :half] = bot_l
                m_l = jnp.maximum(colmax(top[:, :half]), colmax(bot_l))
            else:
                top = top + bias_ref[0, :half, :]
                bot_r = _dot(kb[half:], qc[:, half:]) + bias_ref[0, half:, half:]
                m_l = colmax(top[:, :half])
            s_ref[c, :half, :] = top
            s_ref[c, half:, half:] = bot_r
            m_r = jnp.maximum(colmax(top[:, half:]), colmax(bot_r))
            bm_ref[c] = jnp.concatenate([m_l, m_r], axis=1)

    def update(buf, pair, diag=False, first=False):
        s_ref, bm_ref = buf
        j, i = pair
        vb = vt_ref[:, pl.ds(pl.multiple_of(j * tk, tk), tk)]
        q0 = pl.multiple_of(i * tq, tq)
        qsl = pl.ds(q0, tq)
        for c in range(2):
            if first:
                m_new = bm_ref[c]
            else:
                m_old = m_ref[c, :, qsl]
                m_new = jnp.maximum(m_old, bm_ref[c])
                alpha = jnp.exp2(m_old - m_new)
            m_ref[c, :, qsl] = m_new
            for n in range(tq // sub):
                qs = slice(n * sub, (n + 1) * sub)
                asl = pl.ds(pl.multiple_of(q0 + n * sub, sub), sub)
                pv = None
                for kt in range(tk // sub):
                    if diag and kt > n:
                        continue
                    ks = slice(kt * sub, (kt + 1) * sub)
                    p = jnp.exp2(s_ref[c, ks, qs] - m_new[:, qs])
                    d = _dot(vb[:, ks], p.astype(BF16))
                    pv = d if pv is None else pv + d
                acc_ref[c, :, asl] = pv if first else alpha[:, qs] * acc_ref[c, :, asl] + pv

    def finish(i):
        qsl = pl.ds(pl.multiple_of(i * tq, tq), tq)
        o1 = acc_ref[0, :V_DIM, qsl] * (1.0 / acc_ref[0, V_DIM:V_DIM + 1, qsl])
        o2 = acc_ref[1, :V_DIM, qsl] * (1.0 / acc_ref[1, V_DIM:V_DIM + 1, qsl])
        o = o1 - lam_ref[...] * o2
        o = o * lax.rsqrt(jnp.mean(o * o, axis=0, keepdims=True) + EPS) * (sg_ref[...] * out_scale)
        o_ref[qsl, :] = o.T.astype(BF16)

    far = lambda s: (ftab_ref[0, s], ftab_ref[1, s])
    near = lambda s: (ntab_ref[0, s], ntab_ref[1, s])

    scores(bufa, far(0), "far")

    def far_two(t, first=False):
        scores(bufb, far(2 * t + 1), "far")
        update(bufa, far(2 * t), first=first)
        scores(bufa, far(2 * t + 2), "far")
        update(bufb, far(2 * t + 1), first=first)

    def near_two(t, first=False):
        scores(bufa, near(2 * t + 1), "off1")
        update(bufb, near(2 * t), diag=True, first=first)
        finish(t)
        scores(bufb, near(2 * t + 2), "diag")
        update(bufa, near(2 * t + 1), first=first)

    def run(two, start, stop):
        def body(t, carry):
            two(start + 2 * t)
            two(start + 2 * t + 1)
            return carry
        lax.fori_loop(0, (stop - start) // 2, body, 0)
        if (stop - start) % 2:
            two(stop - 1)

    nq = qz_ref.shape[-1] // tq
    n_first = (nq - 2) // 2
    for t in range(n_first):
        far_two(t, first=True)
    run(far_two, n_first, (n_far - 1) // 2)
    scores(bufb, near(0), "diag")
    update(bufa, far(n_far - 1))
    near_two(0, first=True)
    run(near_two, 1, (n_near - 1) // 2)
    update(bufb, near(n_near - 1), diag=True)
    finish((n_near - 1) // 2)


def _attn_flat_call(lam, qz, k, vt, bias, subln_g, out_scale):
    bsz, n_heads, seq, _ = k.shape
    tq = ATTN_TQ
    nq = seq // tq
    assert seq % tq == 0 and nq % 4 == 0 and bias.shape[1:] == (2, tq, tq)
    ftab, ntab = _attn_order(nq)
    n_far, n_near = ftab.shape[1], ntab.shape[1]
    assert n_far % 2 == 1 and n_near == 2 * nq - 1
    smem = pl.BlockSpec(memory_space=pltpu.SMEM)
    per_head = lambda *blk: pl.BlockSpec((None, None) + blk, lambda b, hh: (b, hh, 0, 0))
    return pl.pallas_call(
        functools.partial(_attn_flat_kernel, out_scale=out_scale, tq=tq, n_far=n_far, n_near=n_near),
        grid=(bsz, n_heads),
        in_specs=[smem, smem, _const_spec(lam.shape),
                  pl.BlockSpec((None, None, 2, V_DIM, seq), lambda b, hh: (b, hh, 0, 0, 0)),
                  per_head(seq, V_DIM), per_head(V_AUG, seq),
                  pl.BlockSpec((None,) + bias.shape[1:], lambda b, hh: (hh, 0, 0, 0)),
                  _const_spec(subln_g.shape)],
        out_specs=pl.BlockSpec((None, seq, V_DIM), lambda b, hh: (b, 0, hh)),
        out_shape=jax.ShapeDtypeStruct((bsz, seq, n_heads * V_DIM), BF16),
        scratch_shapes=[pltpu.VMEM((2, 1, seq), F32),
                        pltpu.VMEM((2, V_AUG, seq), F32),
                        pltpu.VMEM((2, tq, tq), F32), pltpu.VMEM((2, tq, tq), F32),
                        pltpu.VMEM((2, 1, tq), F32), pltpu.VMEM((2, 1, tq), F32)],
        compiler_params=pltpu.CompilerParams(dimension_semantics=("parallel", "parallel"),
                                             vmem_limit_bytes=VMEM_LIMIT_BYTES),
        name="diff_attention",
    )(jnp.asarray(ftab), jnp.asarray(ntab), lam, qz, k, vt, bias, subln_g)


def _final_kernel(o_ref, zg_ref, x_ref, p_ref, wout_ref, wgate_ref, wproj_ref, out_ref):
    g = (o_ref[...].astype(F32) * zg_ref[...].astype(F32)).astype(BF16)
    x1 = x_ref[...] + _dot(g, wout_ref[...])
    out_ref[...] = _ple(x1, p_ref[...], wgate_ref, wproj_ref)


def _final_call(o, zg, x, p, p_layer, wout, wgate, wproj):
    bsz, seq, dm = x.shape
    tm = FINAL_ROW_TILE
    assert seq % tm == 0
    row = lambda b, i: (b, i, 0)
    return pl.pallas_call(
        _final_kernel,
        grid=(bsz, seq // tm),
        in_specs=[pl.BlockSpec((None, tm, o.shape[-1]), row), pl.BlockSpec((None, tm, zg.shape[-1]), row),
                  pl.BlockSpec((None, tm, dm), row), pl.BlockSpec((None, None, tm, p.shape[-1]), lambda b, i: (p_layer, b, i, 0)),
                  _const_spec(wout.shape), _const_spec(wgate.shape), _const_spec(wproj.shape)],
        out_specs=pl.BlockSpec((None, tm, dm), row),
        out_shape=jax.ShapeDtypeStruct((bsz, seq, dm), F32),
        compiler_params=pltpu.CompilerParams(dimension_semantics=("parallel", "parallel"),
                                             vmem_limit_bytes=VMEM_LIMIT_BYTES),
        name="final_rowwise",
    )(o, zg, x, p, wout, wgate, wproj)


def _ssm_params(lam_re, lam_im, log_dt, b_re, b_im, c_re, c_im):
    lr, li = lam_re.astype(F32), lam_im.astype(F32)
    dt = jnp.exp(log_dt.astype(F32))[:, None]
    mag = jnp.exp(lr * dt)
    ab_re, ab_im = mag * jnp.cos(li * dt), mag * jnp.sin(li * dt)
    den = lr * lr + li * li
    nr, ni = ab_re - 1.0, ab_im
    f_re = ((nr * lr + ni * li) / den)[..., None]
    f_im = ((ni * lr - nr * li) / den)[..., None]
    br, bi = b_re.astype(F32), b_im.astype(F32)
    bb_re, bb_im = f_re * br - f_im * bi, f_re * bi + f_im * br
    n_tiles = lam_re.shape[0] // GROUPS_PER_TILE
    eye = jnp.eye(GROUPS_PER_TILE, dtype=F32)

    def pack_b(m):
        m = m.reshape(n_tiles, GROUPS_PER_TILE, SSM_STATE, SSM_GROUP)
        return jnp.einsum('jgpc,gh->jgchp', m, eye).reshape(n_tiles, LANES, STATE_LANES)

    def pack_c(m):
        m = m.reshape(n_tiles, GROUPS_PER_TILE, SSM_GROUP, SSM_STATE)
        return jnp.einsum('jgcp,gh->jgphc', m, eye).reshape(n_tiles, STATE_LANES, LANES)

    bbig = jnp.concatenate([pack_b(bb_re), pack_b(bb_im)], axis=2).astype(BF16)
    cbig = jnp.concatenate([pack_c(c_re.astype(F32)), pack_c(-c_im.astype(F32))], axis=1).astype(BF16)
    a_re = ab_re.reshape(n_tiles, 1, STATE_LANES)
    a_im = ab_im.reshape(n_tiles, 1, STATE_LANES)
    return bbig, cbig, a_re, a_im


def _bias_ids(tk, tq, seq):
    kk = jnp.arange(tk, dtype=jnp.int32)[:, None]
    qq = jnp.arange(tq, dtype=jnp.int32)[None, :]
    rel0 = qq - kk
    rel1 = rel0 + tk
    ids0 = jnp.where(rel0 >= 0, _bucket_of(rel0), -1)
    far = np.arange(tk // 2 + 1, max(seq, tk + 2), dtype=np.float32)
    far = REL_MAX_EXACT + (np.log(far / REL_MAX_EXACT) / math.log(REL_MAX_DIST / REL_MAX_EXACT)
                           * (REL_BUCKETS - REL_MAX_EXACT)).astype(np.int32)
    assert far.min() > REL_BUCKETS
    return jnp.stack([ids0, _bucket_of(rel1)])


def kernel(x, p, a_norm_g, a_w_in, a_lam_re, a_lam_im, a_log_dt, a_b_re, a_b_im, a_c_re, a_c_im, a_d,
           a_w_glu, a_w_out, kv_norm_g, w_k, w_v, k_norm_g, b_norm_g, b_w_in, b_q_norm_g, b_lam_q1,
           b_lam_k1, b_lam_q2, b_lam_k2, b_subln_g, b_w_out, rel_bias, ple_w_proj, ple_w_gate):
    assert a_norm_g.shape[0] == 1 and b_norm_g.shape[0] == 1 and p.shape[0] == 2
    seq = x.shape[1]
    row = lambda v: v.reshape(1, -1).astype(F32)

    bbig, cbig, a_re, a_im = _ssm_params(a_lam_re[0], a_lam_im[0], a_log_dt[0], a_b_re[0], a_b_im[0],
                                         a_c_re[0], a_c_im[0])
    y, zg = _ssm_call(x, row(a_norm_g[0]), a_w_in[0].astype(BF16), bbig, cbig, a_re, a_im, row(a_d[0]))

    q_scale = HEAD_DIM ** -0.5 * LOG2E
    x2, qz, k, vt, zg1 = _mid_call(
        y, zg, x, p, 0, a_w_glu[0].astype(BF16), a_w_out[0].astype(BF16),
        ple_w_gate[0].astype(BF16), ple_w_proj[0].astype(BF16),
        row(kv_norm_g), w_k.astype(BF16), w_v.astype(BF16), k_norm_g.astype(F32).reshape(HEAD_DIM, 1),
        row(b_norm_g[0]), b_w_in[0].astype(BF16), (b_q_norm_g[0].astype(F32) * q_scale).reshape(HEAD_DIM, 1))

    layer_idx = 1
    lam_init = 0.8 - 0.6 * math.exp(-0.3 * layer_idx)
    lam = (jnp.exp(jnp.sum(b_lam_q1[0].astype(F32) * b_lam_k1[0].astype(F32)))
           - jnp.exp(jnp.sum(b_lam_q2[0].astype(F32) * b_lam_k2[0].astype(F32))) + lam_init).reshape(1, 1)
    bias = _bias_call(rel_bias.astype(F32), _bias_ids(ATTN_TQ, ATTN_TQ, seq))
    o = _attn_flat_call(lam, qz, k, vt, bias, b_subln_g[0].astype(F32).reshape(V_DIM, 1), 1.0 - lam_init)
    return _final_call(o, zg1, x2, p, 1, b_w_out[0].astype(BF16), ple_w_gate[1].astype(BF16),
                       ple_w_proj[1].astype(BF16))
```

```python
import functools
import math

import jax
import jax.numpy as jnp
import numpy as np
from jax import lax
from jax.experimental import pallas as pl
from jax.experimental.pallas import tpu as pltpu

F32 = jnp.float32
BF16 = jnp.bfloat16

SUBLANES = 8
LANES = 128
VMEM_LIMIT_BYTES = 56 * 1024 * 1024

EPS = 1e-6
NEG_INF = -1e30
LOG2E = 1.4426950408889634

SSM_GROUP = 16
SSM_STATE = 64
GROUPS_PER_TILE = LANES // SSM_GROUP
STATE_LANES = GROUPS_PER_TILE * SSM_STATE
HEAD_DIM = 64
V_DIM = 2 * HEAD_DIM
V_AUG = V_DIM + 2 * SUBLANES
REL_BUCKETS = 32
REL_MAX_EXACT = REL_BUCKETS // 2
REL_MAX_DIST = 128

SSM_TIME_CHUNK = 64
ROW_TILE = 512
FINAL_ROW_TILE = 1024
ATTN_TQ = 512


def _rms(x, g=None):
    y = x * lax.rsqrt(jnp.mean(x * x, axis=-1, keepdims=True) + EPS)
    return y if g is None else y * g


def _dot(a, b):
    return jnp.dot(a, b, preferred_element_type=F32)


def _sigmoid(x):
    return 1.0 / (1.0 + jnp.exp(-x))


def _gelu_tanh(x):
    c = math.sqrt(2.0 / math.pi)
    return 0.5 * x * (1.0 + jnp.tanh(c * (x + 0.044715 * (x * x * x))))


def _const_spec(shape):
    nd = len(shape)
    return pl.BlockSpec(shape, lambda *_: (0,) * nd, pipeline_mode=pl.Buffered(1))


def _ssm_kernel(x_ref, g_ref, win_ref, bbig_ref, cbig_ref, are_ref, aim_ref, d_ref,
                y_ref, zg_ref,
                hr_ref, hi_ref, us_ref, xs_ref, hs_ref, ys_ref, *, pitch):
    bsz, tc, dm = x_ref.shape
    e = y_ref.shape[-1]
    n_tiles = e // LANES
    rows = bsz * tc

    @pl.when(pl.program_id(0) == 0)
    def _():
        hr_ref[...] = jnp.zeros_like(hr_ref)
        hi_ref[...] = jnp.zeros_like(hi_ref)

    x = x_ref[...].reshape(rows, dm)
    h = _rms(x, g_ref[...]).astype(BF16)
    z = _dot(h, win_ref[:, e:])
    zg_ref[...] = (z * _sigmoid(z)).astype(BF16).reshape(bsz, tc, e)
    u = _dot(h, win_ref[:, :e])
    for j in range(n_tiles):
        for b in range(bsz):
            us_ref[j, b * pitch:b * pitch + tc, :] = u[b * tc:(b + 1) * tc, j * LANES:(j + 1) * LANES]

    for j in range(n_tiles):
        xs, hs, ys = xs_ref.at[j % 2], hs_ref.at[j % 2], ys_ref.at[j % 2]
        u_tb = jnp.concatenate(
            [us_ref[j, pl.ds(t, bsz, stride=pitch), :] for t in range(tc)], axis=0)
        xs[...] = _dot(u_tb.astype(BF16), bbig_ref[j])
        ar = jnp.broadcast_to(are_ref[j], (bsz, STATE_LANES))
        ai = jnp.broadcast_to(aim_ref[j], (bsz, STATE_LANES))
        sr, si = hr_ref[j], hi_ref[j]
        for t in range(tc):
            r0 = t * bsz
            xr = xs[r0:r0 + bsz, :STATE_LANES]
            xi = xs[r0:r0 + bsz, STATE_LANES:]
            sr, si = ar * sr - ai * si + xr, ar * si + ai * sr + xi
            hs[r0:r0 + bsz, :STATE_LANES] = sr
            hs[r0:r0 + bsz, STATE_LANES:] = si
        hr_ref[j] = sr
        hi_ref[j] = si

        y_tb = _dot(hs[...].astype(BF16), cbig_ref[j])
        y_tb = y_tb + d_ref[:, j * LANES:(j + 1) * LANES] * u_tb
        ys[...] = _gelu_tanh(y_tb)
        for b in range(bsz):
            y_ref[b, :, j * LANES:(j + 1) * LANES] = ys[pl.ds(b, tc, stride=bsz), :].astype(BF16)


def _ssm_call(x, norm_g, w_in, bbig, cbig, a_re, a_im, d):
    bsz, seq, dm = x.shape
    e = d.shape[-1]
    tc = SSM_TIME_CHUNK
    assert bsz == SUBLANES and seq % tc == 0 and e % LANES == 0 and tc % (2 * SUBLANES) == 0
    pitch = tc + SUBLANES
    n_tiles = e // LANES
    rows = bsz * tc
    out_shape = [jax.ShapeDtypeStruct((bsz, seq, e), BF16)] * 2
    blk = lambda i: (0, i, 0)
    return pl.pallas_call(
        functools.partial(_ssm_kernel, pitch=pitch),
        grid=(seq // tc,),
        in_specs=[pl.BlockSpec((bsz, tc, dm), blk),
                  _const_spec(norm_g.shape), _const_spec(w_in.shape), _const_spec(bbig.shape),
                  _const_spec(cbig.shape), _const_spec(a_re.shape), _const_spec(a_im.shape),
                  _const_spec(d.shape)],
        out_specs=[pl.BlockSpec((bsz, tc, e), blk)] * 2,
        out_shape=out_shape,
        scratch_shapes=[pltpu.VMEM((n_tiles, bsz, STATE_LANES), F32),
                        pltpu.VMEM((n_tiles, bsz, STATE_LANES), F32),
                        pltpu.VMEM((n_tiles, bsz * pitch, LANES), F32),
                        pltpu.VMEM((2, rows, 2 * STATE_LANES), F32),
                        pltpu.VMEM((2, rows, 2 * STATE_LANES), F32),
                        pltpu.VMEM((2, rows, LANES), F32)],
        compiler_params=pltpu.CompilerParams(dimension_semantics=("arbitrary",),
                                             vmem_limit_bytes=VMEM_LIMIT_BYTES),
        name="ssm_front",
    )(x, norm_g, w_in, bbig, cbig, a_re, a_im, d)


def _half_head_norm_t(part, gain_col):
    return part * lax.rsqrt(jnp.mean(part * part, axis=0, keepdims=True) + EPS) * gain_col


def _ple(x1, p, wgate_ref, wproj_ref):
    gate = _sigmoid(_dot(_rms(x1).astype(BF16), wgate_ref[...]))
    return x1 + gate * _dot(p.astype(BF16), wproj_ref[...])


def _mid_kernel(y_ref, zg_ref, x_ref, p_ref,
                wglu_ref, wout_ref, wgate_ref, wproj_ref,
                kvg_ref, wk_ref, wv_ref, kng_ref,
                bng_ref, bwin_ref, qng_ref,
                x2_ref, qz_ref, k_ref, vt_ref, zg1_ref):
    e = y_ref.shape[-1]
    n_heads = k_ref.shape[0]
    aw = n_heads * V_DIM
    gab = _dot(y_ref[...], wglu_ref[...])
    gl = gab[:, :e] * _sigmoid(gab[:, e:]) * zg_ref[...].astype(F32)
    x1 = x_ref[...] + _dot(gl.astype(BF16), wout_ref[...])
    x2 = _ple(x1, p_ref[...], wgate_ref, wproj_ref)
    x2_ref[...] = x2

    hk = _rms(x2, kvg_ref[...]).astype(BF16)
    k = _dot(hk, wk_ref[...])
    v = _dot(hk, wv_ref[...])
    hq = _rms(x2, bng_ref[...]).astype(BF16)
    q = _dot(hq, bwin_ref[:, :aw])
    z1 = _dot(hq, bwin_ref[:, aw:])
    zg1_ref[...] = (z1 * _sigmoid(z1)).astype(BF16)
    pad_rows = (lax.broadcasted_iota(jnp.int32, (V_AUG - V_DIM, v.shape[0]), 0) == 0).astype(BF16)
    for hh in range(n_heads):
        sl = slice(hh * V_DIM, (hh + 1) * V_DIM)
        kt = k[:, sl].T
        kn = jnp.concatenate([_half_head_norm_t(kt[c * HEAD_DIM:(c + 1) * HEAD_DIM], kng_ref[...])
                              for c in range(2)], axis=0)
        k_ref[hh] = kn.T.astype(BF16)
        qt = q[:, sl].T
        for c in range(2):
            part = _half_head_norm_t(qt[c * HEAD_DIM:(c + 1) * HEAD_DIM], qng_ref[...])
            qz_ref[hh, c, c * HEAD_DIM:(c + 1) * HEAD_DIM, :] = part.astype(BF16)
            qz_ref[hh, c, (1 - c) * HEAD_DIM:(2 - c) * HEAD_DIM, :] = jnp.zeros_like(part, dtype=BF16)
        vt_ref[hh, :V_DIM, :] = v[:, sl].T.astype(BF16)
        vt_ref[hh, V_DIM:, :] = pad_rows


def _mid_call(y, zg, x, p, p_layer, wglu, wout, wgate, wproj, kvg, wk, wv, kng, bng, bwin, qng):
    bsz, seq, dm = x.shape
    e = y.shape[-1]
    aw = wk.shape[-1]
    n_heads = aw // V_DIM
    tm = ROW_TILE
    assert seq % tm == 0
    row = lambda b, i: (b, i, 0)
    out_shape = [jax.ShapeDtypeStruct((bsz, seq, dm), F32),
                 jax.ShapeDtypeStruct((bsz, n_heads, 2, V_DIM, seq), BF16),
                 jax.ShapeDtypeStruct((bsz, n_heads, seq, V_DIM), BF16),
                 jax.ShapeDtypeStruct((bsz, n_heads, V_AUG, seq), BF16),
                 jax.ShapeDtypeStruct((bsz, seq, aw), BF16)]
    t_spec = lambda rows: pl.BlockSpec((None, n_heads, rows, tm), lambda b, i: (b, 0, 0, i))
    weights = (wglu, wout, wgate, wproj, kvg, wk, wv, kng, bng, bwin, qng)
    return pl.pallas_call(
        _mid_kernel,
        grid=(bsz, seq // tm),
        in_specs=[pl.BlockSpec((None, tm, e), row), pl.BlockSpec((None, tm, e), row),
                  pl.BlockSpec((None, tm, dm), row), pl.BlockSpec((None, None, tm, p.shape[-1]), lambda b, i: (p_layer, b, i, 0))]
                 + [_const_spec(w.shape) for w in weights],
        out_specs=[pl.BlockSpec((None, tm, dm), row),
                   pl.BlockSpec((None, n_heads, 2, V_DIM, tm), lambda b, i: (b, 0, 0, 0, i)),
                   pl.BlockSpec((None, n_heads, tm, V_DIM), lambda b, i: (b, 0, i, 0)), t_spec(V_AUG),
                   pl.BlockSpec((None, tm, aw), row)],
        out_shape=out_shape,
        compiler_params=pltpu.CompilerParams(dimension_semantics=("parallel", "parallel"),
                                             vmem_limit_bytes=VMEM_LIMIT_BYTES),
        name="mid_rowwise",
    )(y, zg, x, p, *weights)


def _bucket_of(rel):
    n = jnp.maximum(rel, 0)
    nf = jnp.maximum(n, 1).astype(F32)
    large = REL_MAX_EXACT + jnp.trunc(jnp.log(nf / REL_MAX_EXACT) / math.log(REL_MAX_DIST / REL_MAX_EXACT)
                                      * (REL_BUCKETS - REL_MAX_EXACT)).astype(jnp.int32)
    large = jnp.minimum(large, REL_BUCKETS - 1)
    return jnp.where(n < REL_MAX_EXACT, n, large)


def _bias_kernel(table_ref, ids_ref, out_ref):
    hh = pl.program_id(0)
    n_tiles, tile, _ = ids_ref.shape
    far = table_ref[REL_BUCKETS - 1, hh]
    for t in range(n_tiles):
        for a in range(tile // LANES):
            for b in range(tile // LANES):
                centre = t * tile + LANES * (b - a)
                blk = (t, slice(a * LANES, (a + 1) * LANES), slice(b * LANES, (b + 1) * LANES))
                if centre + (LANES - 1) < 0:
                    out_ref[blk] = jnp.full((LANES, LANES), NEG_INF, F32)
                elif centre - (LANES - 1) > tile // 2:
                    out_ref[blk] = jnp.zeros((LANES, LANES), F32)
                else:
                    ids = ids_ref[blk]
                    acc = jnp.full(ids.shape, NEG_INF, F32)
                    for bkt in range(REL_BUCKETS):
                        acc = jnp.where(ids == bkt, (table_ref[bkt, hh] - far) * LOG2E, acc)
                    out_ref[blk] = acc


def _bias_call(table, ids):
    n_heads = table.shape[1]
    return pl.pallas_call(
        _bias_kernel,
        grid=(n_heads,),
        in_specs=[pl.BlockSpec(memory_space=pltpu.SMEM), _const_spec(ids.shape)],
        out_specs=pl.BlockSpec((None,) + ids.shape, lambda hh: (hh, 0, 0, 0)),
        out_shape=jax.ShapeDtypeStruct((n_heads,) + ids.shape, F32),
        name="rel_bias_tiles",
    )(table, ids)


def _attn_order(nq):
    far = [(j, i) for j in range(nq) for i in range(j + 2, nq)]
    near = [(0, 0)] + [e for i in range(1, nq) for e in ((i - 1, i), (i, i))]
    return np.asarray(far, np.int32).T.copy(), np.asarray(near, np.int32).T.copy()


def _attn_flat_kernel(ftab_ref, ntab_ref, lam_ref, qz_ref, k_ref, vt_ref, bias_ref, sg_ref, o_ref,
                      m_ref, acc_ref, sa_ref, sb_ref, ma_ref, mb_ref,
                      *, out_scale, tq, n_far, n_near):
    tk = tq
    sub = 2 * LANES
    bufa, bufb = (sa_ref, ma_ref), (sb_ref, mb_ref)

    half = tq // 2
    assert half == sub
    colmax = lambda v: jnp.max(v, axis=0, keepdims=True)

    def scores(buf, pair, kind):
        s_ref, bm_ref = buf
        j, i = pair
        kb = k_ref[pl.ds(pl.multiple_of(j * tk, tk), tk), :]
        qsl = pl.ds(pl.multiple_of(i * tq, tq), tq)
        for c in range(2):
            qc = qz_ref[c, :, qsl]
            if kind == "far":
                s = _dot(kb, qc)
                s_ref[c] = s
                bm_ref[c] = colmax(s)
                continue
            top = _dot(kb[:half], qc)
            if kind == "off1":
                bot = _dot(kb[half:], qc)
                bot_l = bot[:, :half] + bias_ref[1, half:, :half]
                bot_r = bot[:, half:]
                s_ref[c, half:, :half] = bot_l
                m_l = jnp.maximum(colmax(top[:, :half]), colmax(bot_l))
            else:
                top = top + bias_ref[0, :half, :]
                bot_r = _dot(kb[half:], qc[:, half:]) + bias_ref[0, half:, half:]
                m_l = colmax(top[:, :half])
            s_ref[c, :half, :] = top
            s_ref[c, half:, half:] = bot_r
            m_r = jnp.maximum(colmax(top[:, half:]), colmax(bot_r))
            bm_ref[c] = jnp.concatenate([m_l, m_r], axis=1)

    def update(buf, pair, diag=False, first=False):
        s_ref, bm_ref = buf
        j, i = pair
        vb = vt_ref[:, pl.ds(pl.multiple_of(j * tk, tk), tk)]
        q0 = pl.multiple_of(i * tq, tq)
        qsl = pl.ds(q0, tq)
        for c in range(2):
            if first:
                m_new = bm_ref[c]
            else:
                m_old = m_ref[c, :, qsl]
                m_new = jnp.maximum(m_old, bm_ref[c])
                alpha = jnp.exp2(m_old - m_new)
            m_ref[c, :, qsl] = m_new
            for n in range(tq // sub):
                qs = slice(n * sub, (n + 1) * sub)
                asl = pl.ds(pl.multiple_of(q0 + n * sub, sub), sub)
                pv = None
                for kt in range(tk // sub):
                    if diag and kt > n:
                        continue
                    ks = slice(kt * sub, (kt + 1) * sub)
                    p = jnp.exp2(s_ref[c, ks, qs] - m_new[:, qs])
                    d = _dot(vb[:, ks], p.astype(BF16))
                    pv = d if pv is None else pv + d
                acc_ref[c, :, asl] = pv if first else alpha[:, qs] * acc_ref[c, :, asl] + pv

    def finish(i):
        qsl = pl.ds(pl.multiple_of(i * tq, tq), tq)
        o1 = acc_ref[0, :V_DIM, qsl] * (1.0 / acc_ref[0, V_DIM:V_DIM + 1, qsl])
        o2 = acc_ref[1, :V_DIM, qsl] * (1.0 / acc_ref[1, V_DIM:V_DIM + 1, qsl])
        o = o1 - lam_ref[...] * o2
        o = o * lax.rsqrt(jnp.mean(o * o, axis=0, keepdims=True) + EPS) * (sg_ref[...] * out_scale)
        o_ref[qsl, :] = o.T.astype(BF16)

    far = lambda s: (ftab_ref[0, s], ftab_ref[1, s])
    near = lambda s: (ntab_ref[0, s], ntab_ref[1, s])

    scores(bufa, far(0), "far")

    def far_two(t, first=False):
        scores(bufb, far(2 * t + 1), "far")
        update(bufa, far(2 * t), first=first)
        scores(bufa, far(2 * t + 2), "far")
        update(bufb, far(2 * t + 1), first=first)

    def near_two(t, first=False):
        scores(bufa, near(2 * t + 1), "off1")
        update(bufb, near(2 * t), diag=True, first=first)
        finish(t)
        scores(bufb, near(2 * t + 2), "diag")
        update(bufa, near(2 * t + 1), first=first)

    def run(two, start, stop):
        def body(t, carry):
            two(start + 2 * t)
            two(start + 2 * t + 1)
            return carry
        lax.fori_loop(0, (stop - start) // 2, body, 0)
        if (stop - start) % 2:
            two(stop - 1)

    nq = qz_ref.shape[-1] // tq
    n_first = (nq - 2) // 2
    for t in range(n_first):
        far_two(t, first=True)
    run(far_two, n_first, (n_far - 1) // 2)
    scores(bufb, near(0), "diag")
    update(bufa, far(n_far - 1))
    near_two(0, first=True)
    run(near_two, 1, (n_near - 1) // 2)
    update(bufb, near(n_near - 1), diag=True)
    finish((n_near - 1) // 2)


def _attn_flat_call(lam, qz, k, vt, bias, subln_g, out_scale):
    bsz, n_heads, seq, _ = k.shape
    tq = ATTN_TQ
    nq = seq // tq
    assert seq % tq == 0 and nq % 4 == 0 and bias.shape[1:] == (2, tq, tq)
    ftab, ntab = _attn_order(nq)
    n_far, n_near = ftab.shape[1], ntab.shape[1]
    assert n_far % 2 == 1 and n_near == 2 * nq - 1
    smem = pl.BlockSpec(memory_space=pltpu.SMEM)
    per_head = lambda *blk: pl.BlockSpec((None, None) + blk, lambda b, hh: (b, hh, 0, 0))
    return pl.pallas_call(
        functools.partial(_attn_flat_kernel, out_scale=out_scale, tq=tq, n_far=n_far, n_near=n_near),
        grid=(bsz, n_heads),
        in_specs=[smem, smem, _const_spec(lam.shape),
                  pl.BlockSpec((None, None, 2, V_DIM, seq), lambda b, hh: (b, hh, 0, 0, 0)),
                  per_head(seq, V_DIM), per_head(V_AUG, seq),
                  pl.BlockSpec((None,) + bias.shape[1:], lambda b, hh: (hh, 0, 0, 0)),
                  _const_spec(subln_g.shape)],
        out_specs=pl.BlockSpec((None, seq, V_DIM), lambda b, hh: (b, 0, hh)),
        out_shape=jax.ShapeDtypeStruct((bsz, seq, n_heads * V_DIM), BF16),
        scratch_shapes=[pltpu.VMEM((2, 1, seq), F32),
                        pltpu.VMEM((2, V_AUG, seq), F32),
                        pltpu.VMEM((2, tq, tq), F32), pltpu.VMEM((2, tq, tq), F32),
                        pltpu.VMEM((2, 1, tq), F32), pltpu.VMEM((2, 1, tq), F32)],
        compiler_params=pltpu.CompilerParams(dimension_semantics=("parallel", "parallel"),
                                             vmem_limit_bytes=VMEM_LIMIT_BYTES),
        name="diff_attention",
    )(jnp.asarray(ftab), jnp.asarray(ntab), lam, qz, k, vt, bias, subln_g)


def _final_kernel(o_ref, zg_ref, x_ref, p_ref, wout_ref, wgate_ref, wproj_ref, out_ref):
    g = (o_ref[...].astype(F32) * zg_ref[...].astype(F32)).astype(BF16)
    x1 = x_ref[...] + _dot(g, wout_ref[...])
    out_ref[...] = _ple(x1, p_ref[...], wgate_ref, wproj_ref)


def _final_call(o, zg, x, p, p_layer, wout, wgate, wproj):
    bsz, seq, dm = x.shape
    tm = FINAL_ROW_TILE
    assert seq % tm == 0
    row = lambda b, i: (b, i, 0)
    return pl.pallas_call(
        _final_kernel,
        grid=(bsz, seq // tm),
        in_specs=[pl.BlockSpec((None, tm, o.shape[-1]), row), pl.BlockSpec((None, tm, zg.shape[-1]), row),
                  pl.BlockSpec((None, tm, dm), row), pl.BlockSpec((None, None, tm, p.shape[-1]), lambda b, i: (p_layer, b, i, 0)),
                  _const_spec(wout.shape), _const_spec(wgate.shape), _const_spec(wproj.shape)],
        out_specs=pl.BlockSpec((None, tm, dm), row),
        out_shape=jax.ShapeDtypeStruct((bsz, seq, dm), F32),
        compiler_params=pltpu.CompilerParams(dimension_semantics=("parallel", "parallel"),
                                             vmem_limit_bytes=VMEM_LIMIT_BYTES),
        name="final_rowwise",
    )(o, zg, x, p, wout, wgate, wproj)


def _ssm_params(lam_re, lam_im, log_dt, b_re, b_im, c_re, c_im):
    lr, li = lam_re.astype(F32), lam_im.astype(F32)
    dt = jnp.exp(log_dt.astype(F32))[:, None]
    mag = jnp.exp(lr * dt)
    ab_re, ab_im = mag * jnp.cos(li * dt), mag * jnp.sin(li * dt)
    den = lr * lr + li * li
    nr, ni = ab_re - 1.0, ab_im
    f_re = ((nr * lr + ni * li) / den)[..., None]
    f_im = ((ni * lr - nr * li) / den)[..., None]
    br, bi = b_re.astype(F32), b_im.astype(F32)
    bb_re, bb_im = f_re * br - f_im * bi, f_re * bi + f_im * br
    n_tiles = lam_re.shape[0] // GROUPS_PER_TILE
    eye = jnp.eye(GROUPS_PER_TILE, dtype=F32)

    def pack_b(m):
        m = m.reshape(n_tiles, GROUPS_PER_TILE, SSM_STATE, SSM_GROUP)
        return jnp.einsum('jgpc,gh->jgchp', m, eye).reshape(n_tiles, LANES, STATE_LANES)

    def pack_c(m):
        m = m.reshape(n_tiles, GROUPS_PER_TILE, SSM_GROUP, SSM_STATE)
        return jnp.einsum('jgcp,gh->jgphc', m, eye).reshape(n_tiles, STATE_LANES, LANES)

    bbig = jnp.concatenate([pack_b(bb_re), pack_b(bb_im)], axis=2).astype(BF16)
    cbig = jnp.concatenate([pack_c(c_re.astype(F32)), pack_c(-c_im.astype(F32))], axis=1).astype(BF16)
    a_re = ab_re.reshape(n_tiles, 1, STATE_LANES)
    a_im = ab_im.reshape(n_tiles, 1, STATE_LANES)
    return bbig, cbig, a_re, a_im


def _bias_ids(tk, tq, seq):
    kk = jnp.arange(tk, dtype=jnp.int32)[:, None]
    qq = jnp.arange(tq, dtype=jnp.int32)[None, :]
    rel0 = qq - kk
    rel1 = rel0 + tk
    ids0 = jnp.where(rel0 >= 0, _bucket_of(rel0), -1)
    far = np.arange(tk // 2 + 1, max(seq, tk + 2), dtype=np.float32)
    far = REL_MAX_EXACT + (np.log(far / REL_MAX_EXACT) / math.log(REL_MAX_DIST / REL_MAX_EXACT)
                           * (REL_BUCKETS - REL_MAX_EXACT)).astype(np.int32)
    assert far.min() > REL_BUCKETS
    return jnp.stack([ids0, _bucket_of(rel1)])


def kernel(x, p, a_norm_g, a_w_in, a_lam_re, a_lam_im, a_log_dt, a_b_re, a_b_im, a_c_re, a_c_im, a_d,
           a_w_glu, a_w_out, kv_norm_g, w_k, w_v, k_norm_g, b_norm_g, b_w_in, b_q_norm_g, b_lam_q1,
           b_lam_k1, b_lam_q2, b_lam_k2, b_subln_g, b_w_out, rel_bias, ple_w_proj, ple_w_gate):
    assert a_norm_g.shape[0] == 1 and b_norm_g.shape[0] == 1 and p.shape[0] == 2
    seq = x.shape[1]
    row = lambda v: v.reshape(1, -1).astype(F32)

    bbig, cbig, a_re, a_im = _ssm_params(a_lam_re[0], a_lam_im[0], a_log_dt[0], a_b_re[0], a_b_im[0],
                                         a_c_re[0], a_c_im[0])
    y, zg = _ssm_call(x, row(a_norm_g[0]), a_w_in[0].astype(BF16), bbig, cbig, a_re, a_im, row(a_d[0]))

    q_scale = HEAD_DIM ** -0.5 * LOG2E
    x2, qz, k, vt, zg1 = _mid_call(
        y, zg, x, p, 0, a_w_glu[0].astype(BF16), a_w_out[0].astype(BF16),
        ple_w_gate[0].astype(BF16), ple_w_proj[0].astype(BF16),
        row(kv_norm_g), w_k.astype(BF16), w_v.astype(BF16), k_norm_g.astype(F32).reshape(HEAD_DIM, 1),
        row(b_norm_g[0]), b_w_in[0].astype(BF16), (b_q_norm_g[0].astype(F32) * q_scale).reshape(HEAD_DIM, 1))

    layer_idx = 1
    lam_init = 0.8 - 0.6 * math.exp(-0.3 * layer_idx)
    lam = (jnp.exp(jnp.sum(b_lam_q1[0].astype(F32) * b_lam_k1[0].astype(F32)))
           - jnp.exp(jnp.sum(b_lam_q2[0].astype(F32) * b_lam_k2[0].astype(F32))) + lam_init).reshape(1, 1)
    bias = _bias_call(rel_bias.astype(F32), _bias_ids(ATTN_TQ, ATTN_TQ, seq))
    o = _attn_flat_call(lam, qz, k, vt, bias, b_subln_g[0].astype(F32).reshape(V_DIM, 1), 1.0 - lam_init)
    return _final_call(o, zg1, x2, p, 1, b_w_out[0].astype(BF16), ple_w_gate[1].astype(BF16),
                       ple_w_proj[1].astype(BF16))
```
